```python
import math
import jax, jax.numpy as jnp
from jax import lax
import numpy as np

D_MODEL = 1024
BATCH = 16
SEQ = 4096
DEPTH = 1

ATT_HEADS = 16
ATT_KV_HEADS = 2
ATT_HEAD_DIM = 64
WINDOW = 128
ATT_BLOCK = 128
ROPE_THETA = 10000.0
ML_HEADS = 4
ML_DQK = 128
ML_DV = 256
ML_CHUNK = 64
CONV_WIDTH = 4
MEM_LEN = 256
X_HEADS = 4
X_HEAD_DIM = D_MODEL // X_HEADS
N_GROUPS = 8
EXPERTS_PER_GROUP = 8
N_EXPERTS = N_GROUPS * EXPERTS_PER_GROUP
TOP_K = 2
D_EXPERT = 256
MOE_BLOCK = 128
ALPHA = (2 * DEPTH) ** 0.25
BETA = (8 * DEPTH) ** -0.25
LN_EPS = 1e-5
RMS_EPS = 1e-6

ATT_Q_W = ATT_HEADS * ATT_HEAD_DIM
ATT_KV_W = ATT_KV_HEADS * ATT_HEAD_DIM
ML_QK_W = ML_HEADS * ML_DQK
ML_V_W = ML_HEADS * ML_DV
IN_WIDTHS = (ATT_Q_W, ATT_KV_W, ATT_KV_W, ML_QK_W, ML_QK_W, ML_V_W, ML_V_W, ML_HEADS, ML_HEADS, D_MODEL, D_MODEL)
IN_WIDTH = ATT_Q_W + 2 * ATT_KV_W + 2 * ML_QK_W + 2 * ML_V_W + 2 * ML_HEADS + 2 * D_MODEL

kernel_name = 'hybrid_swa_mlstm_hmoe_deepnorm'


def _split_columns(t, widths):
    parts, start = [], 0
    for w in widths:
        parts.append(t[..., start:start + w])
        start += w
    return parts


def layer_norm(x, g, b):
    xf = x.astype(jnp.float32)
    mu = jnp.mean(xf, -1, keepdims=True)
    var = jnp.mean(jnp.square(xf - mu), -1, keepdims=True)
    return ((xf - mu) * lax.rsqrt(var + LN_EPS) * g.astype(jnp.float32) + b.astype(jnp.float32)).astype(x.dtype)


def rope(x, positions):
    half = x.shape[-1] // 2
    inv = ROPE_THETA ** (-jnp.arange(half, dtype=jnp.float32) / half)
    ang = positions.astype(jnp.float32)[..., None] * inv
    cos = jnp.cos(ang)[:, :, None, :]
    sin = jnp.sin(ang)[:, :, None, :]
    xf = x.astype(jnp.float32)
    x1, x2 = xf[..., :half], xf[..., half:]
    return jnp.concatenate([x1 * cos - x2 * sin, x2 * cos + x1 * sin], -1).astype(x.dtype)


def sliding_window_attention(q, k, v, sinks):
    B, S, Hq, Dh = q.shape
    Hkv = k.shape[2]
    G = Hq // Hkv
    nb = S // ATT_BLOCK
    qb = q.reshape(B, nb, ATT_BLOCK, Hkv, G, Dh)

    def with_prev(t):
        tb = t.reshape(B, nb, ATT_BLOCK, Hkv, Dh)
        prev = jnp.pad(tb[:, :-1], ((0, 0), (1, 0), (0, 0), (0, 0), (0, 0)))
        return jnp.concatenate([prev, tb], axis=2)

    kb, vb = with_prev(k), with_prev(v)
    scores = jnp.einsum('bnqhgd,bnkhd->bnhgqk', qb, kb, preferred_element_type=jnp.float32) * (Dh ** -0.5)
    blk = jnp.arange(nb)[:, None, None]
    qpos = blk * ATT_BLOCK + jnp.arange(ATT_BLOCK)[None, :, None]
    kpos = (blk - 1) * ATT_BLOCK + jnp.arange(2 * ATT_BLOCK)[None, None, :]
    diff = qpos - kpos
    valid = (diff >= 0) & (diff < WINDOW) & (kpos >= 0)
    scores = jnp.where(valid[None, :, None, None], scores, -jnp.inf)
    sink = sinks.astype(jnp.float32).reshape(1, 1, Hkv, G, 1)
    m = jnp.maximum(jnp.max(scores, -1), sink)
    e = jnp.exp(scores - m[..., None])
    den = jnp.sum(e, -1) + jnp.exp(sink - m)
    probs = e / den[..., None]
    out = jnp.einsum('bnhgqk,bnkhd->bnqhgd', probs.astype(v.dtype), vb)
    return out.reshape(B, S, Hq * Dh)


def causal_depthwise_conv(x, w, b):
    out = lax.conv_general_dilated(x, w[:, None, :].astype(x.dtype), window_strides=(1,),
                                   padding=[(CONV_WIDTH - 1, 0)],
                                   dimension_numbers=('NWC', 'WIO', 'NWC'),
                                   feature_group_count=x.shape[-1])
    return out + b.astype(x.dtype)


def mlstm_chunkwise(q, k, v, log_i, log_f):
    B, H, S, dqk = q.shape
    dv = v.shape[-1]
    L = ML_CHUNK
    nc = S // L

    def to_chunks(t):
        return jnp.moveaxis(t.reshape((B, H, nc, L) + t.shape[3:]), 2, 0)

    xs = (to_chunks(q), to_chunks(k), to_chunks(v), to_chunks(log_i), to_chunks(log_f))
    causal = jnp.tril(jnp.ones((L, L), dtype=bool))

    def step(carry, inp):
        C, n, m = carry
        qx, kx, vx, ix, fx = inp
        b = jnp.cumsum(fx, axis=-1)
        d = jnp.where(causal, b[..., :, None] - b[..., None, :] + ix[..., None, :], -jnp.inf)
        inter = b + m[..., None]
        m_t = jnp.maximum(inter, jnp.max(d, -1))
        w_intra = jnp.exp(d - m_t[..., None])
        w_inter = jnp.exp(inter - m_t)
        s = jnp.einsum('bhtd,bhsd->bhts', qx, kx) * w_intra
        num = w_inter[..., None] * jnp.einsum('bhtd,bhde->bhte', qx, C) + jnp.einsum('bhts,bhse->bhte', s, vx)
        den = w_inter * jnp.einsum('bhtd,bhd->bht', qx, n) + jnp.sum(s, -1)
        h = num / jnp.maximum(jnp.abs(den), jnp.exp(-m_t))[..., None]
        b_end = b[..., -1]
        d_end = b_end[..., None] - b + ix
        m_new = jnp.maximum(b_end + m, jnp.max(d_end, -1))
        w_state = jnp.exp(b_end + m - m_new)
        w_k = jnp.exp(d_end - m_new[..., None])
        C_new = w_state[..., None, None] * C + jnp.einsum('bhs,bhsd,bhse->bhde', w_k, kx, vx)
        n_new = w_state[..., None] * n + jnp.einsum('bhs,bhsd->bhd', w_k, kx)
        return (C_new, n_new, m_new), h

    init = (jnp.zeros((B, H, dqk, dv), jnp.float32), jnp.zeros((B, H, dqk), jnp.float32), jnp.zeros((B, H), jnp.float32))
    _, h = lax.scan(step, init, xs)
    return jnp.moveaxis(h, 0, 2).reshape(B, H, S, dv).transpose(0, 2, 1, 3)


def hybrid_mixer(x, positions, w_in, attn_sinks, conv_w, conv_b, b_igate, b_fgate, ml_norm_g,
                 w_att_branch, w_ml_branch, w_mix_out):
    B, S, _ = x.shape
    f32 = jnp.float32
    proj = x @ w_in
    aq, ak, av, mq, mk, mv, mo, mi, mf, ga, gm = _split_columns(proj, IN_WIDTHS)
    aq = rope(aq.reshape(B, S, ATT_HEADS, ATT_HEAD_DIM), positions)
    ak = rope(ak.reshape(B, S, ATT_KV_HEADS, ATT_HEAD_DIM), positions)
    av = av.reshape(B, S, ATT_KV_HEADS, ATT_HEAD_DIM)
    a_out = sliding_window_attention(aq, ak, av, attn_sinks) @ w_att_branch
    qk_conv = jax.nn.silu(causal_depthwise_conv(jnp.concatenate([mq, mk], -1), conv_w, conv_b))
    mq, mk = qk_conv[..., :ML_QK_W], qk_conv[..., ML_QK_W:]

    def heads_first(t, d):
        return t.reshape(B, S, ML_HEADS, d).transpose(0, 2, 1, 3).astype(f32)

    q = heads_first(mq, ML_DQK)
    k = heads_first(mk, ML_DQK) * (ML_DQK ** -0.5)
    v = heads_first(mv, ML_DV)
    log_i = (mi.astype(f32) + b_igate.astype(f32)).transpose(0, 2, 1)
    log_f = jax.nn.log_sigmoid(mf.astype(f32) + b_fgate.astype(f32)).transpose(0, 2, 1)
    hm = mlstm_chunkwise(q, k, v, log_i, log_f)
    hm = hm * lax.rsqrt(jnp.mean(hm * hm, -1, keepdims=True) + RMS_EPS)
    hm = hm.reshape(B, S, ML_V_W) * ml_norm_g.astype(f32)
    hm = (hm * jax.nn.sigmoid(mo.astype(f32))).astype(x.dtype)
    m_out = hm @ w_ml_branch
    y = jax.nn.sigmoid(ga) * a_out + jax.nn.sigmoid(gm) * m_out
    return y @ w_mix_out


def memory_cross_attention(x, mem, w_xq, w_xkv, w_xo):
    B, S, _ = x.shape
    M = mem.shape[1]
    q = (x @ w_xq).reshape(B, S, X_HEADS, X_HEAD_DIM)
    kv = mem @ w_xkv
    k = kv[..., :D_MODEL].reshape(B, M, X_HEADS, X_HEAD_DIM)
    v = kv[..., D_MODEL:].reshape(B, M, X_HEADS, X_HEAD_DIM)
    s = jnp.einsum('bshd,bmhd->bhsm', q, k, preferred_element_type=jnp.float32) * (X_HEAD_DIM ** -0.5)
    p = jax.nn.softmax(s, axis=-1)
    o = jnp.einsum('bhsm,bmhd->bshd', p.astype(v.dtype), v).reshape(B, S, D_MODEL)
    return o @ w_xo


def hierarchical_moe(x, w_router_group, b_router_group, w_router_expert, b_router_expert, w_gate, w_up, w_down):
    B, S, D = x.shape
    N = B * S
    xf = x.reshape(N, D)
    g_logits = (xf @ w_router_group).astype(jnp.float32) + b_router_group.astype(jnp.float32)
    g_prob = jax.nn.softmax(g_logits, axis=-1)
    g_sel = jnp.argmax(g_logits, axis=-1)
    p_group = jnp.take_along_axis(g_prob, g_sel[:, None], axis=-1)
    e_logits = ((xf @ w_router_expert).astype(jnp.float32) + b_router_expert.astype(jnp.float32)).reshape(N, N_GROUPS, EXPERTS_PER_GROUP)
    e_in_group = jnp.take_along_axis(e_logits, g_sel[:, None, None], axis=1)[:, 0]
    top_val, top_idx = lax.top_k(e_in_group, TOP_K)
    weights = p_group * jax.nn.softmax(top_val, axis=-1)
    expert_id = g_sel[:, None] * EXPERTS_PER_GROUP + top_idx
    A = N * TOP_K
    flat_e = expert_id.reshape(A)
    flat_tok = jnp.arange(A, dtype=jnp.int32) // TOP_K
    flat_w = weights.reshape(A)
    order = jnp.argsort(flat_e)
    se, stok, sw = flat_e[order], flat_tok[order], flat_w[order]
    counts = jnp.zeros((N_EXPERTS,), jnp.int32).at[flat_e].add(1)
    padded = ((counts + MOE_BLOCK - 1) // MOE_BLOCK) * MOE_BLOCK
    pad_end = jnp.cumsum(padded)
    pad_start = pad_end - padded
    cnt_start = jnp.cumsum(counts) - counts
    dest = pad_start[se] + (jnp.arange(A, dtype=jnp.int32) - cnt_start[se])
    n_blocks = -(-A // MOE_BLOCK) + N_EXPERTS
    R = n_blocks * MOE_BLOCK
    row_tok = jnp.zeros((R,), jnp.int32).at[dest].set(stok)
    row_w = jnp.zeros((R,), jnp.float32).at[dest].set(sw)
    block_start = jnp.arange(n_blocks, dtype=jnp.int32) * MOE_BLOCK
    block_expert = jnp.minimum(jnp.searchsorted(pad_end, block_start, side='right'), N_EXPERTS - 1)

    def expert_block(args):
        toks, e = args
        xb = xf[toks]
        hb = jax.nn.silu(xb @ w_gate[e]) * (xb @ w_up[e])
        return hb @ w_down[e]

    y_rows = lax.map(expert_block, (row_tok.reshape(n_blocks, MOE_BLOCK), block_expert))
    y_rows = y_rows.reshape(R, D) * row_w[:, None].astype(x.dtype)
    y = jax.ops.segment_sum(y_rows, row_tok, num_segments=N)
    return y.reshape(B, S, D)


def setup_inputs(seed: int = 0) -> dict:
    key = jax.random.key(seed)
    ks = jax.random.split(key, 32)
    nrm = jax.random.normal
    f32 = jnp.float32
    inv = D_MODEL ** -0.5
    positions = jnp.broadcast_to(jnp.arange(SEQ, dtype=jnp.int32)[None, :], (BATCH, SEQ))
    return {
        'x': nrm(ks[0], (BATCH, SEQ, D_MODEL), f32),
        'mem': nrm(ks[1], (BATCH, MEM_LEN, D_MODEL), f32),
        'positions': positions,
        'w_in': nrm(ks[2], (DEPTH, D_MODEL, IN_WIDTH), f32) * inv,
        'attn_sinks': nrm(ks[3], (DEPTH, ATT_HEADS), f32) * 0.5,
        'conv_w': nrm(ks[4], (DEPTH, CONV_WIDTH, 2 * ML_QK_W), f32) * (CONV_WIDTH ** -0.5),
        'conv_b': nrm(ks[5], (DEPTH, 2 * ML_QK_W), f32) * 0.02,
        'b_igate': nrm(ks[6], (DEPTH, ML_HEADS), f32) * 0.1,
        'b_fgate': jnp.linspace(3.0, 6.0, ML_HEADS, dtype=f32)[None, :] + nrm(ks[7], (DEPTH, ML_HEADS), f32) * 0.1,
        'ml_norm_g': 1.0 + nrm(ks[8], (DEPTH, ML_V_W), f32) * 0.02,
        'w_att_branch': nrm(ks[9], (DEPTH, ATT_Q_W, D_MODEL), f32) * (ATT_Q_W ** -0.5),
        'w_ml_branch': nrm(ks[10], (DEPTH, ML_V_W, D_MODEL), f32) * (ML_V_W ** -0.5),
        'w_mix_out': nrm(ks[11], (DEPTH, D_MODEL, D_MODEL), f32) * inv * BETA,
        'ln1_g': 1.0 + nrm(ks[12], (DEPTH, D_MODEL), f32) * 0.02,
        'ln1_b': nrm(ks[13], (DEPTH, D_MODEL), f32) * 0.02,
        'w_xq': nrm(ks[14], (DEPTH, D_MODEL, D_MODEL), f32) * inv,
        'w_xkv': nrm(ks[15], (DEPTH, D_MODEL, 2 * D_MODEL), f32) * inv,
        'w_xo': nrm(ks[16], (DEPTH, D_MODEL, D_MODEL), f32) * inv * BETA,
        'ln2_g': 1.0 + nrm(ks[17], (DEPTH, D_MODEL), f32) * 0.02,
        'ln2_b': nrm(ks[18], (DEPTH, D_MODEL), f32) * 0.02,
        'w_router_group': nrm(ks[19], (DEPTH, D_MODEL, N_GROUPS), f32) * inv,
        'b_router_group': nrm(ks[20], (DEPTH, N_GROUPS), f32) * 0.01,
        'w_router_expert': nrm(ks[21], (DEPTH, D_MODEL, N_EXPERTS), f32) * inv,
        'b_router_expert': nrm(ks[22], (DEPTH, N_EXPERTS), f32) * 0.01,
        'w_gate': nrm(ks[23], (DEPTH, N_EXPERTS, D_MODEL, D_EXPERT), f32) * inv,
        'w_up': nrm(ks[24], (DEPTH, N_EXPERTS, D_MODEL, D_EXPERT), f32) * inv,
        'w_down': nrm(ks[25], (DEPTH, N_EXPERTS, D_EXPERT, D_MODEL), f32) * (D_EXPERT ** -0.5) * BETA,
        'ln3_g': 1.0 + nrm(ks[26], (DEPTH, D_MODEL), f32) * 0.02,
        'ln3_b': nrm(ks[27], (DEPTH, D_MODEL), f32) * 0.02,
    }


def reference(x, mem, positions, w_in, attn_sinks, conv_w, conv_b, b_igate, b_fgate, ml_norm_g,
              w_att_branch, w_ml_branch, w_mix_out, ln1_g, ln1_b, w_xq, w_xkv, w_xo, ln2_g, ln2_b,
              w_router_group, b_router_group, w_router_expert, b_router_expert, w_gate, w_up, w_down,
              ln3_g, ln3_b):
    h = x
    for l in range(DEPTH):
        mix = hybrid_mixer(h, positions, w_in[l], attn_sinks[l], conv_w[l], conv_b[l], b_igate[l], b_fgate[l],
                           ml_norm_g[l], w_att_branch[l], w_ml_branch[l], w_mix_out[l])
        h = layer_norm(ALPHA * h + mix, ln1_g[l], ln1_b[l])
        xa = memory_cross_attention(h, mem, w_xq[l], w_xkv[l], w_xo[l])
        h = layer_norm(ALPHA * h + xa, ln2_g[l], ln2_b[l])
        ff = hierarchical_moe(h, w_router_group[l], b_router_group[l], w_router_expert[l], b_router_expert[l],
                              w_gate[l], w_up[l], w_down[l])
        h = layer_norm(ALPHA * h + ff, ln3_g[l], ln3_b[l])
    return h
```

```python
import functools

import jax
import jax.numpy as jnp
from jax import lax
from jax.experimental import pallas as pl
from jax.experimental.pallas import tpu as pltpu

F32 = jnp.float32
BF16 = jnp.bfloat16
I32 = jnp.int32

D_MODEL = 1024
ATT_HEADS = 16
ATT_KV_HEADS = 2
ATT_HEAD_DIM = 64
ATT_BLOCK = 128
ROPE_THETA = 10000.0
ML_HEADS = 4
ML_DQK = 128
ML_DV = 256
ML_CHUNK = 64
CONV_WIDTH = 4
X_HEADS = 4
X_HEAD_DIM = D_MODEL // X_HEADS
N_GROUPS = 8
EXPERTS_PER_GROUP = 8
N_EXPERTS = N_GROUPS * EXPERTS_PER_GROUP
TOP_K = 2
D_EXPERT = 256
MOE_BLOCK = 128
LN_EPS = 1e-5
RMS_EPS = 1e-6

ATT_Q_W = ATT_HEADS * ATT_HEAD_DIM
ATT_KV_W = ATT_KV_HEADS * ATT_HEAD_DIM
ML_QK_W = ML_HEADS * ML_DQK
ML_V_W = ML_HEADS * ML_DV

LANES = 128
SUBLANES = 8
VMEM_LIMIT_BYTES = 56 * 1024 * 1024

ROUTE_E_LANE = 0
ROUTE_W_LANE = 2
ROUTE_RANK_LANE = 4
ROUTER_EXPERT_LANE0 = N_GROUPS


def _params(*semantics):
    return pltpu.CompilerParams(dimension_semantics=semantics, vmem_limit_bytes=VMEM_LIMIT_BYTES)


def _const_spec(shape):
    zeros = (0,) * len(shape)
    return pl.BlockSpec(shape, lambda *_: zeros, pipeline_mode=pl.Buffered(1))


def _layer_norm(z, g, b):
    mu = jnp.mean(z, axis=-1, keepdims=True)
    zc = z - mu
    var = jnp.mean(zc * zc, axis=-1, keepdims=True)
    return zc * lax.rsqrt(var + LN_EPS) * g + b


def _inproj_kernel(x_ref, pos_ref, inv_ref, wrope_ref, wv_ref, wmqk_ref, wmv_ref, wg_ref,
                   convw_ref, convb_ref, gbias_ref,
                   q_out, kv_out, mq_out, mk_out, mv_out, g_out, prev_ref):
    t = x_ref.shape[0]
    xb = x_ref[...].astype(BF16)

    ang = pos_ref[...].astype(F32) * inv_ref[...]
    cos = jnp.cos(ang)
    sin = jnp.sin(ang)
    lane = lax.broadcasted_iota(I32, (t, LANES), 1)
    first_half = (lane % ATT_HEAD_DIM) < (ATT_HEAD_DIM // 2)
    sin_signed = jnp.where(first_half, -sin, sin)

    def rope(z):
        partner = jnp.where(first_half, pltpu.roll(z, LANES - ATT_HEAD_DIM // 2, 1),
                            pltpu.roll(z, ATT_HEAD_DIM // 2, 1))
        return z * cos + partner * sin_signed

    qk = jnp.dot(xb, wrope_ref[...], preferred_element_type=F32)
    q_scale = ATT_HEAD_DIM ** -0.5
    for g in range(ATT_Q_W // LANES):
        sl = slice(g * LANES, (g + 1) * LANES)
        q_out[:, sl] = (rope(qk[:, sl]) * q_scale).astype(BF16)
    for g in range(2 * ATT_KV_W // LANES):
        sl = slice(ATT_Q_W + g * LANES, ATT_Q_W + (g + 1) * LANES)
        kv_out[:, g * LANES:(g + 1) * LANES] = rope(qk[:, sl]).astype(BF16)
    kv_out[:, 2 * ATT_KV_W:] = jnp.dot(xb, wv_ref[...], preferred_element_type=F32).astype(BF16)

    @pl.when(pl.program_id(1) == 0)
    def _():
        prev_ref[...] = jnp.zeros_like(prev_ref)

    pre = jnp.dot(xb, wmqk_ref[...], preferred_element_type=F32)
    prev8 = prev_ref[...]
    row8 = lax.broadcasted_iota(I32, (SUBLANES, 2 * ML_QK_W), 0)
    w_now = convw_ref[CONV_WIDTH - 1:CONV_WIDTH, :]
    acc = pre * w_now + convb_ref[...]
    top = pre[0:SUBLANES] * w_now + convb_ref[...]
    for d in range(1, CONV_WIDTH):
        w_d = convw_ref[CONV_WIDTH - 1 - d:CONV_WIDTH - d, :]
        shifted = pltpu.roll(pre, d, 0)
        acc = acc + shifted * w_d
        top = top + jnp.where(row8 < d, pltpu.roll(prev8, d, 0), shifted[0:SUBLANES]) * w_d
    prev_ref[...] = pre[t - SUBLANES:t]
    conv = jnp.concatenate([top, acc[SUBLANES:]], axis=0)
    act = conv * jax.nn.sigmoid(conv)
    mq_out[...] = act[:, :ML_QK_W].astype(BF16)
    mk_out[...] = (act[:, ML_QK_W:] * (ML_DQK ** -0.5)).astype(BF16)

    mv_out[...] = jnp.dot(xb, wmv_ref[...], preferred_element_type=F32).astype(BF16)

    gates = jnp.dot(xb, wg_ref[...], preferred_element_type=F32) + gbias_ref[...]
    g_out[...] = jnp.where(lane < ML_HEADS, gates, jax.nn.log_sigmoid(gates))


def _in_proj(x2, pos2, w_in, conv_w, conv_b, b_igate, b_fgate, batch, seq):
    n = x2.shape[0]
    t = min(512, seq)
    s_tiles = seq // t

    def dup(w):
        h = ATT_HEAD_DIM
        return jnp.concatenate([w[:, :h], w[:, :h], w[:, h:], w[:, h:]], axis=1)

    o_k = ATT_Q_W
    o_v = o_k + ATT_KV_W
    o_mq = o_v + ATT_KV_W
    o_mv = o_mq + 2 * ML_QK_W
    o_mo = o_mv + ML_V_W
    o_mi = o_mo + ML_V_W
    w_rope = jnp.concatenate([w_in[:, :o_k], dup(w_in[:, o_k:o_v])], axis=1).astype(BF16)
    w_v = dup(w_in[:, o_v:o_mq]).astype(BF16)
    w_mqk = w_in[:, o_mq:o_mv].astype(BF16)
    w_mv = w_in[:, o_mv:o_mo].astype(BF16)
    w_g = jnp.pad(w_in[:, o_mi:o_mi + 2 * ML_HEADS], ((0, 0), (0, LANES - 2 * ML_HEADS))).astype(BF16)
    gbias = jnp.pad(jnp.concatenate([b_igate, b_fgate]).astype(F32), (0, LANES - 2 * ML_HEADS))[None, :]
    half = ATT_HEAD_DIM // 2
    inv = ROPE_THETA ** (-jnp.arange(half, dtype=F32) / half)
    inv = jnp.tile(inv, LANES // half)[None, :]

    row = lambda w: pl.BlockSpec((t, w), lambda b, s: (b * s_tiles + s, 0))
    outs = pl.pallas_call(
        _inproj_kernel,
        grid=(batch, s_tiles),
        in_specs=[row(D_MODEL), row(1), _const_spec((1, LANES)),
                  _const_spec(w_rope.shape), _const_spec(w_v.shape), _const_spec(w_mqk.shape),
                  _const_spec(w_mv.shape), _const_spec(w_g.shape),
                  _const_spec((CONV_WIDTH, 2 * ML_QK_W)), _const_spec((1, 2 * ML_QK_W)),
                  _const_spec((1, LANES))],
        out_specs=[row(ATT_Q_W), row(4 * ATT_KV_W), row(ML_QK_W), row(ML_QK_W), row(ML_V_W), row(LANES)],
        out_shape=[jax.ShapeDtypeStruct((n, ATT_Q_W), BF16),
                   jax.ShapeDtypeStruct((n, 4 * ATT_KV_W), BF16),
                   jax.ShapeDtypeStruct((n, ML_QK_W), BF16),
                   jax.ShapeDtypeStruct((n, ML_QK_W), BF16),
                   jax.ShapeDtypeStruct((n, ML_V_W), BF16),
                   jax.ShapeDtypeStruct((n, LANES), F32)],
        scratch_shapes=[pltpu.VMEM((SUBLANES, 2 * ML_QK_W), F32)],
        compiler_params=_params("arbitrary", "arbitrary"),
        name="in_proj",
    )(x2, pos2, inv, w_rope, w_v, w_mqk, w_mv, w_g, conv_w.astype(F32), conv_b.astype(F32)[None, :], gbias)
    return outs


def _swa_kernel(sink_ref, q_ref, kvc_ref, kvp_ref, o_ref, kv_buf):
    tq = q_ref.shape[0]
    nblk = tq // ATT_BLOCK
    first = pl.program_id(1) == 0
    kw = 2 * ATT_KV_W
    kv_buf[0:ATT_BLOCK, :] = kvp_ref[...]
    kv_buf[ATT_BLOCK:, :] = kvc_ref[...]

    nk = 2 * ATT_BLOCK
    qi = lax.broadcasted_iota(I32, (ATT_BLOCK, 2 * nk), 0)
    col = lax.broadcasted_iota(I32, (ATT_BLOCK, 2 * nk), 1)
    kc = col % nk
    band = (kc > qi) & (kc <= qi + ATT_BLOCK)
    head_a = col < nk
    lane = lax.broadcasted_iota(I32, (nk, LANES), 1)
    lo = lane < ATT_HEAD_DIM

    def block(j, carry):
        r0 = pl.multiple_of(j * ATT_BLOCK, ATT_BLOCK)
        valid = band & (kc >= jnp.where(first & (j == 0), ATT_BLOCK, 0))
        kv = kv_buf[pl.ds(r0, nk), :]
        for kvh in range(ATT_KV_HEADS):
            kk = kv[:, kvh * LANES:(kvh + 1) * LANES]
            vv = kv[:, kw + kvh * LANES:kw + (kvh + 1) * LANES]
            zero = jnp.zeros_like(kk)
            k2 = jnp.concatenate([jnp.where(lo, kk, zero), jnp.where(lo, zero, kk)], axis=0)
            v2 = jnp.concatenate([jnp.where(lo, vv, zero), jnp.where(lo, zero, vv)], axis=0)
            pairs = ATT_HEADS // ATT_KV_HEADS // 2
            for pp in range(pairs):
                p = kvh * pairs + pp
                qp = q_ref[pl.ds(r0, ATT_BLOCK), p * LANES:(p + 1) * LANES]
                sc = lax.dot_general(qp, k2, (((1,), (1,)), ((), ())), preferred_element_type=F32)
                sc = jnp.where(valid, sc, -jnp.inf)
                sink_a = sink_ref[2 * p]
                sink_b = sink_ref[2 * p + 1]
                m_a = jnp.maximum(jnp.max(sc[:, :nk], axis=1, keepdims=True), sink_a)
                m_b = jnp.maximum(jnp.max(sc[:, nk:], axis=1, keepdims=True), sink_b)
                e = jnp.exp(sc - jnp.where(head_a, m_a, m_b))
                den_a = jnp.sum(e[:, :nk], axis=1, keepdims=True) + jnp.exp(sink_a - m_a)
                den_b = jnp.sum(e[:, nk:], axis=1, keepdims=True) + jnp.exp(sink_b - m_b)
                probs = (e * jnp.where(head_a, 1.0 / den_a, 1.0 / den_b)).astype(BF16)
                out = jnp.dot(probs, v2, preferred_element_type=F32)
                o_ref[pl.ds(r0, ATT_BLOCK), p * LANES:(p + 1) * LANES] = out.astype(BF16)
        return carry

    lax.fori_loop(0, nblk, block, 0)


def _swa(q, kv, sinks, batch, seq):
    n = q.shape[0]
    tq = min(512, seq)
    s_tiles = seq // tq
    per = tq // ATT_BLOCK
    blocks_per_seq = seq // ATT_BLOCK
    grid_spec = pltpu.PrefetchScalarGridSpec(
        num_scalar_prefetch=1,
        grid=(batch, s_tiles),
        in_specs=[pl.BlockSpec((tq, ATT_Q_W), lambda b, s, *_: (b * s_tiles + s, 0)),
                  pl.BlockSpec((tq, 4 * ATT_KV_W), lambda b, s, *_: (b * s_tiles + s, 0)),
                  pl.BlockSpec((ATT_BLOCK, 4 * ATT_KV_W),
                               lambda b, s, *_: (b * blocks_per_seq + jnp.maximum(s * per - 1, 0), 0))],
        out_specs=pl.BlockSpec((tq, ATT_Q_W), lambda b, s, *_: (b * s_tiles + s, 0)),
        scratch_shapes=[pltpu.VMEM((tq + ATT_BLOCK, 4 * ATT_KV_W), BF16)],
    )
    return pl.pallas_call(
        _swa_kernel,
        grid_spec=grid_spec,
        out_shape=jax.ShapeDtypeStruct((n, ATT_Q_W), BF16),
        compiler_params=_params("arbitrary", "arbitrary"),
        name="swa",
    )(sinks.astype(F32), q, kv, kv)


def _mlstm_kernel(q_ref, k_ref, v_ref, g_ref, o_ref, c_ref, n_ref, m_ref):
    tm = q_ref.shape[0]
    L = ML_CHUNK

    @pl.when(pl.program_id(1) == 0)
    def _():
        c_ref[...] = jnp.zeros_like(c_ref)
        n_ref[...] = jnp.zeros_like(n_ref)
        m_ref[...] = jnp.zeros_like(m_ref)

    ti = lax.broadcasted_iota(I32, (L, L), 0)
    si = lax.broadcasted_iota(I32, (L, L), 1)
    causal = si <= ti
    rowi = lax.broadcasted_iota(I32, (L, LANES), 0)

    def chunk(c, carry):
        r0 = pl.multiple_of(c * L, L)
        g = g_ref[pl.ds(r0, L), :]
        cs = g
        d = 1
        while d < L:
            cs = cs + jnp.where(rowi >= d, pltpu.roll(cs, d, 0), 0.0)
            d *= 2
        g_t = g.T
        cs_t = cs.T
        for h in range(ML_HEADS):
            i_col = g[:, h:h + 1]
            b_col = cs[:, ML_HEADS + h:ML_HEADS + h + 1]
            i_row = g_t[h:h + 1, :]
            b_row = cs_t[ML_HEADS + h:ML_HEADS + h + 1, :]
            m_prev = m_ref[h, 0:1, 0:1]
            qh = q_ref[pl.ds(r0, L), h * ML_DQK:(h + 1) * ML_DQK]
            kh = k_ref[pl.ds(r0, L), h * ML_DQK:(h + 1) * ML_DQK]
            vh = v_ref[pl.ds(r0, L), h * ML_DV:(h + 1) * ML_DV]
            c_prev = c_ref[h]
            n_prev = n_ref[h, 0:1, :]

            dmat = jnp.where(causal, b_col - b_row + i_row, -jnp.inf)
            inter = b_col + m_prev
            m_t = jnp.maximum(inter, jnp.max(dmat, axis=1, keepdims=True))
            w_intra = jnp.exp(dmat - m_t)
            w_inter = jnp.exp(inter - m_t)
            s = lax.dot_general(qh, kh, (((1,), (1,)), ((), ())), preferred_element_type=F32) * w_intra
            num = (w_inter * jnp.dot(qh, c_prev.astype(BF16), preferred_element_type=F32)
                   + jnp.dot(s.astype(BF16), vh, preferred_element_type=F32))
            den = (w_inter * jnp.sum(qh.astype(F32) * n_prev, axis=1, keepdims=True)
                   + jnp.sum(s, axis=1, keepdims=True))
            hval = num / jnp.maximum(jnp.abs(den), jnp.exp(-m_t))
            hval = hval * lax.rsqrt(jnp.mean(hval * hval, axis=1, keepdims=True) + RMS_EPS)
            o_ref[pl.ds(r0, L), h * ML_DV:(h + 1) * ML_DV] = hval.astype(BF16)

            b_end = b_col[L - 1:L, :]
            d_end = b_end - b_col + i_col
            m_new = jnp.maximum(b_end + m_prev, jnp.max(d_end, axis=0, keepdims=True))
            w_state = jnp.exp(b_end + m_prev - m_new)
            kw = kh.astype(F32) * jnp.exp(d_end - m_new)
            c_ref[h] = w_state * c_prev + jnp.dot(kw.T.astype(BF16), vh, preferred_element_type=F32)
            n_ref[h, 0:1, :] = w_state * n_prev + jnp.sum(kw, axis=0, keepdims=True)
            m_ref[h] = jnp.broadcast_to(m_new, m_ref.shape[1:])
        return carry

    lax.fori_loop(0, tm // L, chunk, 0)


def _mlstm(mq, mk, mv, gates, batch, seq):
    n = mq.shape[0]
    tm = min(512, seq)
    s_tiles = seq // tm
    row = lambda w: pl.BlockSpec((tm, w), lambda b, s: (b * s_tiles + s, 0))
    return pl.pallas_call(
        _mlstm_kernel,
        grid=(batch, s_tiles),
        in_specs=[row(ML_QK_W), row(ML_QK_W), row(ML_V_W), row(LANES)],
        out_specs=row(ML_V_W),
        out_shape=jax.ShapeDtypeStruct((n, ML_V_W), BF16),
        scratch_shapes=[pltpu.VMEM((ML_HEADS, ML_DQK, ML_DV), F32),
                        pltpu.VMEM((ML_HEADS, SUBLANES, ML_DQK), F32),
                        pltpu.VMEM((ML_HEADS, SUBLANES, LANES), F32)],
        compiler_params=_params("arbitrary", "arbitrary"),
        name="mlstm",
    )(mq, mk, mv, gates)


def _memkv_kernel(mem_ref, w_ref, k_out, v_out):
    kv = jnp.dot(mem_ref[...].astype(BF16), w_ref[...], preferred_element_type=F32)
    k_out[...] = kv[:, :D_MODEL].astype(BF16)
    v_out[...] = kv[:, D_MODEL:].astype(BF16)


def _mem_kv(mem2, w_xkv, batch, mem_len):
    blk = pl.BlockSpec((mem_len, D_MODEL), lambda b: (b, 0))
    return pl.pallas_call(
        _memkv_kernel,
        grid=(batch,),
        in_specs=[blk, _const_spec((D_MODEL, 2 * D_MODEL))],
        out_specs=[blk, blk],
        out_shape=[jax.ShapeDtypeStruct((batch * mem_len, D_MODEL), BF16)] * 2,
        compiler_params=_params("arbitrary"),
        name="mem_kv",
    )(mem2, w_xkv.astype(BF16))


def _post_kernel(alpha, x_ref, att_ref, hm_ref, kx_ref, vx_ref, wgate_ref, watt_ref, wml_ref, wmix_ref,
                 wxq_ref, wxo_ref, wr_hi_ref, wr_lo_ref, vec_ref, rbias_ref,
                 h_out, route_out, count_out, base_ref):
    t = x_ref.shape[0]
    first = (pl.program_id(0) == 0) & (pl.program_id(1) == 0)

    @pl.when(first)
    def _():
        base_ref[...] = jnp.zeros_like(base_ref)

    x = x_ref[...]
    xb = x.astype(BF16)
    norm_g, ln1_g, ln1_b, ln2_g, ln2_b = (vec_ref[i:i + 1, :] for i in range(5))

    mo = jnp.dot(xb, wgate_ref[:, 0:D_MODEL], preferred_element_type=F32)
    hm = (hm_ref[...].astype(F32) * norm_g * jax.nn.sigmoid(mo)).astype(BF16)
    m_out = jnp.dot(hm, wml_ref[...], preferred_element_type=F32)
    gm = jnp.dot(xb, wgate_ref[:, 2 * D_MODEL:3 * D_MODEL], preferred_element_type=F32)
    y = jax.nn.sigmoid(gm) * m_out
    a_out = jnp.dot(att_ref[...], watt_ref[...], preferred_element_type=F32)
    ga = jnp.dot(xb, wgate_ref[:, D_MODEL:2 * D_MODEL], preferred_element_type=F32)
    y = y + jax.nn.sigmoid(ga) * a_out
    mix = jnp.dot(y.astype(BF16), wmix_ref[...], preferred_element_type=F32)
    h1 = _layer_norm(alpha * x + mix, ln1_g, ln1_b)

    q = (jnp.dot(h1.astype(BF16), wxq_ref[...], preferred_element_type=F32) * (X_HEAD_DIM ** -0.5)).astype(BF16)
    heads = []
    for h in range(X_HEADS):
        sl = slice(h * X_HEAD_DIM, (h + 1) * X_HEAD_DIM)
        sc = lax.dot_general(q[:, sl], kx_ref[:, sl], (((1,), (1,)), ((), ())), preferred_element_type=F32)
        e = jnp.exp(sc - jnp.max(sc, axis=1, keepdims=True))
        probs = (e * (1.0 / jnp.sum(e, axis=1, keepdims=True))).astype(BF16)
        heads.append(jnp.dot(probs, vx_ref[:, sl], preferred_element_type=F32).astype(BF16))
    xa = jnp.dot(jnp.concatenate(heads, axis=1), wxo_ref[...], preferred_element_type=F32)
    h2 = _layer_norm(alpha * h1 + xa, ln2_g, ln2_b)
    h_out[...] = h2

    h_hi = h2.astype(BF16)
    h_lo = (h2 - h_hi.astype(F32)).astype(BF16)
    logits = (jnp.dot(h_hi, wr_hi_ref[...], preferred_element_type=F32)
              + jnp.dot(h_lo, wr_hi_ref[...], preferred_element_type=F32)
              + jnp.dot(h_hi, wr_lo_ref[...], preferred_element_type=F32)) + rbias_ref[...]

    lane = lax.broadcasted_iota(I32, (t, LANES), 1)
    neg = -jnp.inf
    is_group = lane < N_GROUPS
    gl = jnp.where(is_group, logits, neg)
    g_max = jnp.max(gl, axis=1, keepdims=True)
    lane_f = lane.astype(F32)
    no_lane = float(LANES)
    g_sel = jnp.min(jnp.where(gl == g_max, lane_f, no_lane), axis=1, keepdims=True)
    p_group = 1.0 / jnp.sum(jnp.where(is_group, jnp.exp(logits - g_max), 0.0), axis=1, keepdims=True)
    lane_group = ((lane - ROUTER_EXPERT_LANE0) >> 3).astype(F32)
    in_group = (lane >= ROUTER_EXPERT_LANE0) & (lane_group == g_sel)
    el = jnp.where(in_group, logits, neg)
    v1 = jnp.max(el, axis=1, keepdims=True)
    i1 = jnp.min(jnp.where(el == v1, lane_f, no_lane), axis=1, keepdims=True)
    el2 = jnp.where(lane_f == i1, neg, el)
    v2 = jnp.max(el2, axis=1, keepdims=True)
    i2 = jnp.min(jnp.where(el2 == v2, lane_f, no_lane), axis=1, keepdims=True)
    e2 = jnp.exp(v2 - v1)
    w1 = p_group * (1.0 / (1.0 + e2))
    w2 = p_group * (e2 / (1.0 + e2))

    onehot = jnp.where(lane_f == i1, 1.0, 0.0) + jnp.where(lane_f == i2, 1.0, 0.0)
    tri = jnp.where(lax.broadcasted_iota(I32, (t, t), 1) < lax.broadcasted_iota(I32, (t, t), 0), 1.0, 0.0).astype(BF16)
    before = jnp.dot(tri, onehot.astype(BF16), preferred_element_type=F32) + base_ref[0:1, :]
    rank1 = jnp.sum(jnp.where(lane_f == i1, before, 0.0), axis=1, keepdims=True)
    rank2 = jnp.sum(jnp.where(lane_f == i2, before, 0.0), axis=1, keepdims=True)
    total = base_ref[0:1, :] + jnp.sum(onehot, axis=0, keepdims=True)
    base_ref[...] = jnp.broadcast_to(total, base_ref.shape)
    count_out[...] = jnp.broadcast_to(total, count_out.shape)

    rec = jnp.where(lane == ROUTE_E_LANE, i1 - ROUTER_EXPERT_LANE0, 0.0)
    rec = jnp.where(lane == ROUTE_E_LANE + 1, i2 - ROUTER_EXPERT_LANE0, rec)
    rec = jnp.where(lane == ROUTE_W_LANE, w1, rec)
    rec = jnp.where(lane == ROUTE_W_LANE + 1, w2, rec)
    rec = jnp.where(lane == ROUTE_RANK_LANE, rank1, rec)
    rec = jnp.where(lane == ROUTE_RANK_LANE + 1, rank2, rec)
    route_out[...] = rec


def _post(alpha, x2, att, hm, kx, vx, w_gates, w_att, w_ml, w_mix, w_xq, w_xo, w_router, b_router, vecs,
          batch, seq, mem_len):
    n = x2.shape[0]
    t = min(512, seq)
    s_tiles = seq // t
    row = lambda w: pl.BlockSpec((t, w), lambda b, s: (b * s_tiles + s, 0))
    memblk = pl.BlockSpec((mem_len, D_MODEL), lambda b, s: (b, 0))
    w_hi = w_router.astype(BF16)
    w_lo = (w_router - w_hi.astype(F32)).astype(BF16)
    sq = _const_spec((D_MODEL, D_MODEL))
    return pl.pallas_call(
        functools.partial(_post_kernel, alpha),
        grid=(batch, s_tiles),
        in_specs=[row(D_MODEL), row(ATT_Q_W), row(ML_V_W), memblk, memblk,
                  _const_spec((D_MODEL, 3 * D_MODEL)), sq, sq, sq, sq, sq,
                  _const_spec((D_MODEL, LANES)), _const_spec((D_MODEL, LANES)),
                  _const_spec((SUBLANES, D_MODEL)), _const_spec((1, LANES))],
        out_specs=[row(D_MODEL), row(LANES), _const_spec((SUBLANES, LANES))],
        out_shape=[jax.ShapeDtypeStruct((n, D_MODEL), F32),
                   jax.ShapeDtypeStruct((n, LANES), F32),
                   jax.ShapeDtypeStruct((SUBLANES, LANES), F32)],
        scratch_shapes=[pltpu.VMEM((SUBLANES, LANES), F32)],
        compiler_params=_params("arbitrary", "arbitrary"),
        name="post",
    )(x2, att, hm, kx, vx, w_gates, w_att, w_ml, w_mix, w_xq, w_xo, w_hi, w_lo, vecs, b_router)


def _scatter_kernel(dest_ref, h_ref, xs_in, xs_out, sem):
    del xs_in
    ts = h_ref.shape[0]
    base = pl.program_id(0) * (ts * TOP_K)

    def copy(t, d):
        return pltpu.make_async_copy(h_ref.at[pl.ds(t, 1), :], xs_out.at[pl.ds(d, 1), :], sem)

    def issue(t, carry):
        for k in range(TOP_K):
            copy(t, dest_ref[base + TOP_K * t + k]).start()
        return carry

    lax.fori_loop(0, ts, issue, 0)

    def drain(t, carry):
        for k in range(TOP_K):
            copy(t, dest_ref[base + TOP_K * t + k]).wait()
        return carry

    lax.fori_loop(0, ts, drain, 0)


def _moe_scatter(dest, h2, rows):
    n = h2.shape[0]
    ts = min(256, n)
    grid_spec = pltpu.PrefetchScalarGridSpec(
        num_scalar_prefetch=1,
        grid=(n // ts,),
        in_specs=[pl.BlockSpec((ts, D_MODEL), lambda i, *_: (i, 0)),
                  pl.BlockSpec(memory_space=pl.ANY)],
        out_specs=pl.BlockSpec(memory_space=pl.ANY),
        scratch_shapes=[pltpu.SemaphoreType.DMA(())],
    )
    return pl.pallas_call(
        _scatter_kernel,
        grid_spec=grid_spec,
        out_shape=jax.ShapeDtypeStruct((rows, D_MODEL), F32),
        input_output_aliases={2: 0},
        compiler_params=_params("arbitrary"),
        name="moe_scatter",
    )(dest, h2, jnp.zeros((rows, D_MODEL), F32))


def _expert_kernel(bexp_ref, nused_ref, x_ref, wgu_ref, wd_ref, y_ref):
    del bexp_ref
    i = pl.program_id(0)

    @pl.when(i < nused_ref[0])
    def _():
        xb = x_ref[...].astype(BF16)
        gu = jnp.dot(xb, wgu_ref[0], preferred_element_type=F32)
        gate = gu[:, :D_EXPERT]
        hb = (gate * jax.nn.sigmoid(gate) * gu[:, D_EXPERT:]).astype(BF16)
        y_ref[...] = jnp.dot(hb, wd_ref[0], preferred_element_type=F32)

    @pl.when(i >= nused_ref[0])
    def _():
        y_ref[...] = jnp.zeros_like(y_ref)


def _moe_experts(block_expert, n_used, xs, w_gu, w_down):
    rows = xs.shape[0]
    n_blocks = rows // MOE_BLOCK
    grid_spec = pltpu.PrefetchScalarGridSpec(
        num_scalar_prefetch=2,
        grid=(n_blocks,),
        in_specs=[pl.BlockSpec((MOE_BLOCK, D_MODEL), lambda i, *_: (i, 0)),
                  pl.BlockSpec((1, D_MODEL, 2 * D_EXPERT), lambda i, be, nu: (be[i], 0, 0)),
                  pl.BlockSpec((1, D_EXPERT, D_MODEL), lambda i, be, nu: (be[i], 0, 0))],
        out_specs=pl.BlockSpec((MOE_BLOCK, D_MODEL), lambda i, *_: (i, 0)),
    )
    return pl.pallas_call(
        _expert_kernel,
        grid_spec=grid_spec,
        out_shape=jax.ShapeDtypeStruct((rows, D_MODEL), F32),
        compiler_params=_params("arbitrary"),
        name="moe_experts",
    )(block_expert, n_used, xs, w_gu, w_down)


def _combine_kernel(alpha, dest_ref, h_ref, route_ref, vec_ref, y_hbm, o_ref, ybuf, sem):
    tc = h_ref.shape[0]
    base = pl.program_id(0) * (tc * TOP_K)

    def copy(t, k, d):
        return pltpu.make_async_copy(y_hbm.at[pl.ds(d, 1), :], ybuf.at[k, pl.ds(t, 1), :], sem)

    def issue(t, carry):
        for k in range(TOP_K):
            copy(t, k, dest_ref[base + TOP_K * t + k]).start()
        return carry

    lax.fori_loop(0, tc, issue, 0)

    def drain(t, carry):
        for k in range(TOP_K):
            copy(t, k, dest_ref[base + TOP_K * t + k]).wait()
        return carry

    lax.fori_loop(0, tc, drain, 0)

    route = route_ref[...]
    ff = jnp.zeros((tc, D_MODEL), F32)
    for k in range(TOP_K):
        ff = ff + ybuf[k] * route[:, ROUTE_W_LANE + k:ROUTE_W_LANE + k + 1]
    o_ref[...] = _layer_norm(alpha * h_ref[...] + ff, vec_ref[0:1, :], vec_ref[1:2, :])


def _moe_combine(alpha, dest, h2, route, vecs, y):
    n = h2.shape[0]
    tc = min(256, n)
    grid_spec = pltpu.PrefetchScalarGridSpec(
        num_scalar_prefetch=1,
        grid=(n // tc,),
        in_specs=[pl.BlockSpec((tc, D_MODEL), lambda i, *_: (i, 0)),
                  pl.BlockSpec((tc, LANES), lambda i, *_: (i, 0)),
                  pl.BlockSpec((SUBLANES, D_MODEL), lambda i, *_: (0, 0)),
                  pl.BlockSpec(memory_space=pl.ANY)],
        out_specs=pl.BlockSpec((tc, D_MODEL), lambda i, *_: (i, 0)),
        scratch_shapes=[pltpu.VMEM((TOP_K, tc, D_MODEL), F32), pltpu.SemaphoreType.DMA(())],
    )
    return pl.pallas_call(
        functools.partial(_combine_kernel, alpha),
        grid_spec=grid_spec,
        out_shape=jax.ShapeDtypeStruct((n, D_MODEL), F32),
        compiler_params=_params("arbitrary"),
        name="moe_combine",
    )(dest, h2, route, vecs, y)


def _pad_rows(vectors):
    rows = [v.astype(F32)[None, :] for v in vectors]
    rows.append(jnp.zeros((SUBLANES - len(rows), vectors[0].shape[0]), F32))
    return jnp.concatenate(rows, axis=0)


def _layer(alpha, h, mem2, pos2, batch, seq, mem_len, w_in, attn_sinks, conv_w, conv_b, b_igate, b_fgate,
           ml_norm_g, w_att_branch, w_ml_branch, w_mix_out, ln1_g, ln1_b, w_xq, w_xkv, w_xo, ln2_g, ln2_b,
           w_router_group, b_router_group, w_router_expert, b_router_expert, w_gate, w_up, w_down,
           ln3_g, ln3_b):
    n = h.shape[0]
    q, kv, mq, mk, mv, gates = _in_proj(h, pos2, w_in, conv_w, conv_b, b_igate, b_fgate, batch, seq)
    att = _swa(q, kv, attn_sinks, batch, seq)
    hm = _mlstm(mq, mk, mv, gates, batch, seq)
    kx, vx = _mem_kv(mem2, w_xkv, batch, mem_len)

    o_mo = ATT_Q_W + 2 * ATT_KV_W + 2 * ML_QK_W + ML_V_W
    o_ga = o_mo + ML_V_W + 2 * ML_HEADS
    w_gates = jnp.concatenate([w_in[:, o_mo:o_mo + ML_V_W], w_in[:, o_ga:]], axis=1).astype(BF16)
    pad = LANES - N_GROUPS - N_EXPERTS
    w_router = jnp.pad(jnp.concatenate([w_router_group, w_router_expert], axis=1).astype(F32), ((0, 0), (0, pad)))
    b_router = jnp.pad(jnp.concatenate([b_router_group, b_router_expert]).astype(F32), (0, pad))[None, :]
    vecs = _pad_rows([ml_norm_g, ln1_g, ln1_b, ln2_g, ln2_b])
    h2, route, counts = _post(alpha, h, att, hm, kx, vx, w_gates, w_att_branch.astype(BF16),
                              w_ml_branch.astype(BF16), w_mix_out.astype(BF16), w_xq.astype(BF16),
                              w_xo.astype(BF16), w_router, b_router, vecs, batch, seq, mem_len)

    counts = counts[0, ROUTER_EXPERT_LANE0:ROUTER_EXPERT_LANE0 + N_EXPERTS].astype(I32)
    padded = ((counts + MOE_BLOCK - 1) // MOE_BLOCK) * MOE_BLOCK
    pad_end = jnp.cumsum(padded)
    pad_start = pad_end - padded
    n_blocks = -(-(n * TOP_K) // MOE_BLOCK) + N_EXPERTS
    block_start = jnp.arange(n_blocks, dtype=I32) * MOE_BLOCK
    block_expert = jnp.minimum(jnp.searchsorted(pad_end, block_start, side='right'), N_EXPERTS - 1).astype(I32)
    n_used = (pad_end[-1:] // MOE_BLOCK).astype(I32)
    expert_id = route[:, ROUTE_E_LANE:ROUTE_E_LANE + TOP_K].astype(I32)
    rank = route[:, ROUTE_RANK_LANE:ROUTE_RANK_LANE + TOP_K].astype(I32)
    dest = (pad_start[expert_id] + rank).reshape(n * TOP_K)

    xs = _moe_scatter(dest, h2, n_blocks * MOE_BLOCK)
    w_gu = jnp.concatenate([w_gate, w_up], axis=2).astype(BF16)
    y = _moe_experts(block_expert, n_used, xs, w_gu, w_down.astype(BF16))
    return _moe_combine(alpha, dest, h2, route, _pad_rows([ln3_g, ln3_b]), y)


def kernel(x, mem, positions, w_in, attn_sinks, conv_w, conv_b, b_igate, b_fgate, ml_norm_g, w_att_branch, w_ml_branch, w_mix_out, ln1_g, ln1_b, w_xq, w_xkv, w_xo, ln2_g, ln2_b, w_router_group, b_router_group, w_router_expert, b_router_expert, w_gate, w_up, w_down, ln3_g, ln3_b):
    batch, seq, d = x.shape
    mem_len = mem.shape[1]
    depth = w_in.shape[0]
    alpha = (2 * depth) ** 0.25
    h = x.reshape(batch * seq, d)
    mem2 = mem.reshape(batch * mem_len, d)
    pos2 = positions.reshape(batch * seq, 1)
    stacked = (w_in, attn_sinks, conv_w, conv_b, b_igate, b_fgate, ml_norm_g, w_att_branch, w_ml_branch,
               w_mix_out, ln1_g, ln1_b, w_xq, w_xkv, w_xo, ln2_g, ln2_b, w_router_group, b_router_group,
               w_router_expert, b_router_expert, w_gate, w_up, w_down, ln3_g, ln3_b)
    for l in range(depth):
        h = _layer(alpha, h, mem2, pos2, batch, seq, mem_len, *(w[l] for w in stacked))
    return h.reshape(batch, seq, d)
```

```python
import functools

import jax
import jax.numpy as jnp
from jax import lax
from jax.experimental import pallas as pl
from jax.experimental.pallas import tpu as pltpu

F32 = jnp.float32
BF16 = jnp.bfloat16
I32 = jnp.int32

D_MODEL = 1024
ATT_HEADS = 16
ATT_KV_HEADS = 2
ATT_HEAD_DIM = 64
ATT_BLOCK = 128
ROPE_THETA = 10000.0
ML_HEADS = 4
ML_DQK = 128
ML_DV = 256
ML_CHUNK = 64
CONV_WIDTH = 4
X_HEADS = 4
X_HEAD_DIM = D_MODEL // X_HEADS
N_GROUPS = 8
EXPERTS_PER_GROUP = 8
N_EXPERTS = N_GROUPS * EXPERTS_PER_GROUP
TOP_K = 2
D_EXPERT = 256
MOE_BLOCK = 128
LN_EPS = 1e-5
RMS_EPS = 1e-6

ATT_Q_W = ATT_HEADS * ATT_HEAD_DIM
ATT_KV_W = ATT_KV_HEADS * ATT_HEAD_DIM
ML_QK_W = ML_HEADS * ML_DQK
ML_V_W = ML_HEADS * ML_DV

LANES = 128
SUBLANES = 8
VMEM_LIMIT_BYTES = 56 * 1024 * 1024

ROUTER_EXPERT_LANE0 = N_GROUPS
REC_GROUP_LANE = 0
REC_RANK_LANE = 1
REC_WEIGHT_LANE0 = 8
ROW_W = D_MODEL + LANES
GROUP_BLOCK = 512


def _params(*semantics):
    return pltpu.CompilerParams(dimension_semantics=semantics, vmem_limit_bytes=VMEM_LIMIT_BYTES)


def _const_spec(shape):
    zeros = (0,) * len(shape)
    return pl.BlockSpec(shape, lambda *_: zeros, pipeline_mode=pl.Buffered(1))


def _layer_norm(z, g, b):
    mu = jnp.mean(z, axis=-1, keepdims=True)
    zc = z - mu
    var = jnp.mean(zc * zc, axis=-1, keepdims=True)
    return zc * lax.rsqrt(var + LN_EPS) * g + b


def _inproj_kernel(x_ref, pos_ref, inv_ref, wrope_ref, wv_ref, wmqk_ref, wmv_ref, wg_ref,
                   convw_ref, convb_ref, gbias_ref,
                   q_out, kv_out, mq_out, mk_out, mv_out, g_out, prev_ref):
    t = x_ref.shape[0]
    xb = x_ref[...].astype(BF16)

    ang = pos_ref[...].astype(F32) * inv_ref[...]
    cos = jnp.cos(ang)
    sin = jnp.sin(ang)
    lane = lax.broadcasted_iota(I32, (t, LANES), 1)
    first_half = (lane % ATT_HEAD_DIM) < (ATT_HEAD_DIM // 2)
    sin_signed = jnp.where(first_half, -sin, sin)

    def rope(z):
        partner = jnp.where(first_half, pltpu.roll(z, LANES - ATT_HEAD_DIM // 2, 1),
                            pltpu.roll(z, ATT_HEAD_DIM // 2, 1))
        return z * cos + partner * sin_signed

    qk = jnp.dot(xb, wrope_ref[...], preferred_element_type=F32)
    q_scale = ATT_HEAD_DIM ** -0.5
    for g in range(ATT_Q_W // LANES):
        sl = slice(g * LANES, (g + 1) * LANES)
        q_out[:, sl] = (rope(qk[:, sl]) * q_scale).astype(BF16)
    for g in range(2 * ATT_KV_W // LANES):
        sl = slice(ATT_Q_W + g * LANES, ATT_Q_W + (g + 1) * LANES)
        kv_out[:, g * LANES:(g + 1) * LANES] = rope(qk[:, sl]).astype(BF16)
    kv_out[:, 2 * ATT_KV_W:] = jnp.dot(xb, wv_ref[...], preferred_element_type=F32).astype(BF16)

    @pl.when(pl.program_id(1) == 0)
    def _():
        prev_ref[...] = jnp.zeros_like(prev_ref)

    pre = jnp.dot(xb, wmqk_ref[...], preferred_element_type=F32)
    prev8 = prev_ref[...]
    row8 = lax.broadcasted_iota(I32, (SUBLANES, 2 * ML_QK_W), 0)
    w_now = convw_ref[CONV_WIDTH - 1:CONV_WIDTH, :]
    acc = pre * w_now + convb_ref[...]
    top = pre[0:SUBLANES] * w_now + convb_ref[...]
    for d in range(1, CONV_WIDTH):
        w_d = convw_ref[CONV_WIDTH - 1 - d:CONV_WIDTH - d, :]
        shifted = pltpu.roll(pre, d, 0)
        acc = acc + shifted * w_d
        top = top + jnp.where(row8 < d, pltpu.roll(prev8, d, 0), shifted[0:SUBLANES]) * w_d
    prev_ref[...] = pre[t - SUBLANES:t]
    conv = jnp.concatenate([top, acc[SUBLANES:]], axis=0)
    act = conv * jax.nn.sigmoid(conv)
    mq_out[...] = act[:, :ML_QK_W].astype(BF16)
    mk_out[...] = (act[:, ML_QK_W:] * (ML_DQK ** -0.5)).astype(BF16)

    mv_out[...] = jnp.dot(xb, wmv_ref[...], preferred_element_type=F32).astype(BF16)

    gates = jnp.dot(xb, wg_ref[...], preferred_element_type=F32) + gbias_ref[...]
    g_out[...] = jnp.where(lane < ML_HEADS, gates, jax.nn.log_sigmoid(gates))


def _in_proj(x2, pos2, w_in, conv_w, conv_b, b_igate, b_fgate, batch, seq):
    n = x2.shape[0]
    t = min(512, seq)
    s_tiles = seq // t

    def dup(w):
        h = ATT_HEAD_DIM
        return jnp.concatenate([w[:, :h], w[:, :h], w[:, h:], w[:, h:]], axis=1)

    o_k = ATT_Q_W
    o_v = o_k + ATT_KV_W
    o_mq = o_v + ATT_KV_W
    o_mv = o_mq + 2 * ML_QK_W
    o_mo = o_mv + ML_V_W
    o_mi = o_mo + ML_V_W
    w_rope = jnp.concatenate([w_in[:, :o_k], dup(w_in[:, o_k:o_v])], axis=1).astype(BF16)
    w_v = dup(w_in[:, o_v:o_mq]).astype(BF16)
    w_mqk = w_in[:, o_mq:o_mv].astype(BF16)
    w_mv = w_in[:, o_mv:o_mo].astype(BF16)
    w_g = jnp.pad(w_in[:, o_mi:o_mi + 2 * ML_HEADS], ((0, 0), (0, LANES - 2 * ML_HEADS))).astype(BF16)
    gbias = jnp.pad(jnp.concatenate([b_igate, b_fgate]).astype(F32), (0, LANES - 2 * ML_HEADS))[None, :]
    half = ATT_HEAD_DIM // 2
    inv = ROPE_THETA ** (-jnp.arange(half, dtype=F32) / half)
    inv = jnp.tile(inv, LANES // half)[None, :]

    row = lambda w: pl.BlockSpec((t, w), lambda b, s: (b * s_tiles + s, 0))
    outs = pl.pallas_call(
        _inproj_kernel,
        grid=(batch, s_tiles),
        in_specs=[row(D_MODEL), row(1), _const_spec((1, LANES)),
                  _const_spec(w_rope.shape), _const_spec(w_v.shape), _const_spec(w_mqk.shape),
                  _const_spec(w_mv.shape), _const_spec(w_g.shape),
                  _const_spec((CONV_WIDTH, 2 * ML_QK_W)), _const_spec((1, 2 * ML_QK_W)),
                  _const_spec((1, LANES))],
        out_specs=[row(ATT_Q_W), row(4 * ATT_KV_W), row(ML_QK_W), row(ML_QK_W), row(ML_V_W), row(LANES)],
        out_shape=[jax.ShapeDtypeStruct((n, ATT_Q_W), BF16),
                   jax.ShapeDtypeStruct((n, 4 * ATT_KV_W), BF16),
                   jax.ShapeDtypeStruct((n, ML_QK_W), BF16),
                   jax.ShapeDtypeStruct((n, ML_QK_W), BF16),
                   jax.ShapeDtypeStruct((n, ML_V_W), BF16),
                   jax.ShapeDtypeStruct((n, LANES), F32)],
        scratch_shapes=[pltpu.VMEM((SUBLANES, 2 * ML_QK_W), F32)],
        compiler_params=_params("arbitrary", "arbitrary"),
        name="in_proj",
    )(x2, pos2, inv, w_rope, w_v, w_mqk, w_mv, w_g, conv_w.astype(F32), conv_b.astype(F32)[None, :], gbias)
    return outs


def _swa_kernel(sink_ref, q_ref, kvc_ref, kvp_ref, o_ref, kv_buf):
    tq = q_ref.shape[0]
    nblk = tq // ATT_BLOCK
    first = pl.program_id(1) == 0
    kw = 2 * ATT_KV_W
    kv_buf[0:ATT_BLOCK, :] = kvp_ref[...]
    kv_buf[ATT_BLOCK:, :] = kvc_ref[...]

    nk = 2 * ATT_BLOCK
    qi = lax.broadcasted_iota(I32, (ATT_BLOCK, 2 * nk), 0)
    col = lax.broadcasted_iota(I32, (ATT_BLOCK, 2 * nk), 1)
    kc = col % nk
    band = (kc > qi) & (kc <= qi + ATT_BLOCK)
    head_a = col < nk
    lane = lax.broadcasted_iota(I32, (nk, LANES), 1)
    lo = lane < ATT_HEAD_DIM

    def block(j, carry):
        r0 = pl.multiple_of(j * ATT_BLOCK, ATT_BLOCK)
        valid = band & (kc >= jnp.where(first & (j == 0), ATT_BLOCK, 0))
        kv = kv_buf[pl.ds(r0, nk), :]
        for kvh in range(ATT_KV_HEADS):
            kk = kv[:, kvh * LANES:(kvh + 1) * LANES]
            vv = kv[:, kw + kvh * LANES:kw + (kvh + 1) * LANES]
            zero = jnp.zeros_like(kk)
            k2 = jnp.concatenate([jnp.where(lo, kk, zero), jnp.where(lo, zero, kk)], axis=0)
            v2 = jnp.concatenate([jnp.where(lo, vv, zero), jnp.where(lo, zero, vv)], axis=0)
            pairs = ATT_HEADS // ATT_KV_HEADS // 2
            for pp in range(pairs):
                p = kvh * pairs + pp
                qp = q_ref[pl.ds(r0, ATT_BLOCK), p * LANES:(p + 1) * LANES]
                sc = lax.dot_general(qp, k2, (((1,), (1,)), ((), ())), preferred_element_type=F32)
                sc = jnp.where(valid, sc, -jnp.inf)
                sink_a = sink_ref[2 * p]
                sink_b = sink_ref[2 * p + 1]
                m_a = jnp.maximum(jnp.max(sc[:, :nk], axis=1, keepdims=True), sink_a)
                m_b = jnp.maximum(jnp.max(sc[:, nk:], axis=1, keepdims=True), sink_b)
                e = jnp.exp(sc - jnp.where(head_a, m_a, m_b))
                den_a = jnp.sum(e[:, :nk], axis=1, keepdims=True) + jnp.exp(sink_a - m_a)
                den_b = jnp.sum(e[:, nk:], axis=1, keepdims=True) + jnp.exp(sink_b - m_b)
                probs = (e * jnp.where(head_a, 1.0 / den_a, 1.0 / den_b)).astype(BF16)
                out = jnp.dot(probs, v2, preferred_element_type=F32)
                o_ref[pl.ds(r0, ATT_BLOCK), p * LANES:(p + 1) * LANES] = out.astype(BF16)
        return carry

    lax.fori_loop(0, nblk, block, 0)


def _swa(q, kv, sinks, batch, seq):
    n = q.shape[0]
    tq = min(512, seq)
    s_tiles = seq // tq
    per = tq // ATT_BLOCK
    blocks_per_seq = seq // ATT_BLOCK
    grid_spec = pltpu.PrefetchScalarGridSpec(
        num_scalar_prefetch=1,
        grid=(batch, s_tiles),
        in_specs=[pl.BlockSpec((tq, ATT_Q_W), lambda b, s, *_: (b * s_tiles + s, 0)),
                  pl.BlockSpec((tq, 4 * ATT_KV_W), lambda b, s, *_: (b * s_tiles + s, 0)),
                  pl.BlockSpec((ATT_BLOCK, 4 * ATT_KV_W),
                               lambda b, s, *_: (b * blocks_per_seq + jnp.maximum(s * per - 1, 0), 0))],
        out_specs=pl.BlockSpec((tq, ATT_Q_W), lambda b, s, *_: (b * s_tiles + s, 0)),
        scratch_shapes=[pltpu.VMEM((tq + ATT_BLOCK, 4 * ATT_KV_W), BF16)],
    )
    return pl.pallas_call(
        _swa_kernel,
        grid_spec=grid_spec,
        out_shape=jax.ShapeDtypeStruct((n, ATT_Q_W), BF16),
        compiler_params=_params("arbitrary", "arbitrary"),
        name="swa",
    )(sinks.astype(F32), q, kv, kv)


def _mlstm_kernel(q_ref, k_ref, v_ref, g_ref, o_ref, c_ref, n_ref, m_ref):
    tm = q_ref.shape[0]
    L = ML_CHUNK

    @pl.when(pl.program_id(1) == 0)
    def _():
        c_ref[...] = jnp.zeros_like(c_ref)
        n_ref[...] = jnp.zeros_like(n_ref)
        m_ref[...] = jnp.zeros_like(m_ref)

    ti = lax.broadcasted_iota(I32, (L, L), 0)
    si = lax.broadcasted_iota(I32, (L, L), 1)
    causal = si <= ti
    rowi = lax.broadcasted_iota(I32, (L, LANES), 0)

    def chunk(c, carry):
        r0 = pl.multiple_of(c * L, L)
        g = g_ref[pl.ds(r0, L), :]
        cs = g
        d = 1
        while d < L:
            cs = cs + jnp.where(rowi >= d, pltpu.roll(cs, d, 0), 0.0)
            d *= 2
        g_t = g.T
        cs_t = cs.T
        for h in range(ML_HEADS):
            i_col = g[:, h:h + 1]
            b_col = cs[:, ML_HEADS + h:ML_HEADS + h + 1]
            i_row = g_t[h:h + 1, :]
            b_row = cs_t[ML_HEADS + h:ML_HEADS + h + 1, :]
            m_prev = m_ref[h, 0:1, 0:1]
            qh = q_ref[pl.ds(r0, L), h * ML_DQK:(h + 1) * ML_DQK]
            kh = k_ref[pl.ds(r0, L), h * ML_DQK:(h + 1) * ML_DQK]
            vh = v_ref[pl.ds(r0, L), h * ML_DV:(h + 1) * ML_DV]
            c_prev = c_ref[h]
            n_prev = n_ref[h, 0:1, :]

            dmat = jnp.where(causal, b_col - b_row + i_row, -jnp.inf)
            inter = b_col + m_prev
            m_t = jnp.maximum(inter, jnp.max(dmat, axis=1, keepdims=True))
            w_intra = jnp.exp(dmat - m_t)
            w_inter = jnp.exp(inter - m_t)
            s = lax.dot_general(qh, kh, (((1,), (1,)), ((), ())), preferred_element_type=F32) * w_intra
            num = (w_inter * jnp.dot(qh, c_prev.astype(BF16), preferred_element_type=F32)
                   + jnp.dot(s.astype(BF16), vh, preferred_element_type=F32))
            den = (w_inter * jnp.sum(qh.astype(F32) * n_prev, axis=1, keepdims=True)
                   + jnp.sum(s, axis=1, keepdims=True))
            hval = num / jnp.maximum(jnp.abs(den), jnp.exp(-m_t))
            hval = hval * lax.rsqrt(jnp.mean(hval * hval, axis=1, keepdims=True) + RMS_EPS)
            o_ref[pl.ds(r0, L), h * ML_DV:(h + 1) * ML_DV] = hval.astype(BF16)

            b_end = b_col[L - 1:L, :]
            d_end = b_end - b_col + i_col
            m_new = jnp.maximum(b_end + m_prev, jnp.max(d_end, axis=0, keepdims=True))
            w_state = jnp.exp(b_end + m_prev - m_new)
            kw = kh.astype(F32) * jnp.exp(d_end - m_new)
            c_ref[h] = w_state * c_prev + jnp.dot(kw.T.astype(BF16), vh, preferred_element_type=F32)
            n_ref[h, 0:1, :] = w_state * n_prev + jnp.sum(kw, axis=0, keepdims=True)
            m_ref[h] = jnp.broadcast_to(m_new, m_ref.shape[1:])
        return carry

    lax.fori_loop(0, tm // L, chunk, 0)


def _mlstm(mq, mk, mv, gates, batch, seq):
    n = mq.shape[0]
    tm = min(512, seq)
    s_tiles = seq // tm
    row = lambda w: pl.BlockSpec((tm, w), lambda b, s: (b * s_tiles + s, 0))
    return pl.pallas_call(
        _mlstm_kernel,
        grid=(batch, s_tiles),
        in_specs=[row(ML_QK_W), row(ML_QK_W), row(ML_V_W), row(LANES)],
        out_specs=row(ML_V_W),
        out_shape=jax.ShapeDtypeStruct((n, ML_V_W), BF16),
        scratch_shapes=[pltpu.VMEM((ML_HEADS, ML_DQK, ML_DV), F32),
                        pltpu.VMEM((ML_HEADS, SUBLANES, ML_DQK), F32),
                        pltpu.VMEM((ML_HEADS, SUBLANES, LANES), F32)],
        compiler_params=_params("arbitrary", "arbitrary"),
        name="mlstm",
    )(mq, mk, mv, gates)


def _memkv_kernel(mem_ref, w_ref, k_out, v_out):
    kv = jnp.dot(mem_ref[...].astype(BF16), w_ref[...], preferred_element_type=F32)
    k_out[...] = kv[:, :D_MODEL].astype(BF16)
    v_out[...] = kv[:, D_MODEL:].astype(BF16)


def _mem_kv(mem2, w_xkv, batch, mem_len):
    blk = pl.BlockSpec((mem_len, D_MODEL), lambda b: (b, 0))
    return pl.pallas_call(
        _memkv_kernel,
        grid=(batch,),
        in_specs=[blk, _const_spec((D_MODEL, 2 * D_MODEL))],
        out_specs=[blk, blk],
        out_shape=[jax.ShapeDtypeStruct((batch * mem_len, D_MODEL), BF16)] * 2,
        compiler_params=_params("arbitrary"),
        name="mem_kv",
    )(mem2, w_xkv.astype(BF16))


def _post_kernel(alpha, x_ref, att_ref, hm_ref, kx_ref, vx_ref, wgate_ref, watt_ref, wml_ref, wmix_ref,
                 wxq_ref, wxo_ref, wr_hi_ref, wr_lo_ref, vec_ref, rbias_ref,
                 h_out, count_out, base_ref):
    t = x_ref.shape[0]
    first = (pl.program_id(0) == 0) & (pl.program_id(1) == 0)

    @pl.when(first)
    def _():
        base_ref[...] = jnp.zeros_like(base_ref)

    x = x_ref[...]
    xb = x.astype(BF16)
    norm_g, ln1_g, ln1_b, ln2_g, ln2_b = (vec_ref[i:i + 1, :] for i in range(5))

    mo = jnp.dot(xb, wgate_ref[:, 0:D_MODEL], preferred_element_type=F32)
    hm = (hm_ref[...].astype(F32) * norm_g * jax.nn.sigmoid(mo)).astype(BF16)
    m_out = jnp.dot(hm, wml_ref[...], preferred_element_type=F32)
    gm = jnp.dot(xb, wgate_ref[:, 2 * D_MODEL:3 * D_MODEL], preferred_element_type=F32)
    y = jax.nn.sigmoid(gm) * m_out
    a_out = jnp.dot(att_ref[...], watt_ref[...], preferred_element_type=F32)
    ga = jnp.dot(xb, wgate_ref[:, D_MODEL:2 * D_MODEL], preferred_element_type=F32)
    y = y + jax.nn.sigmoid(ga) * a_out
    mix = jnp.dot(y.astype(BF16), wmix_ref[...], preferred_element_type=F32)
    h1 = _layer_norm(alpha * x + mix, ln1_g, ln1_b)

    q = (jnp.dot(h1.astype(BF16), wxq_ref[...], preferred_element_type=F32) * (X_HEAD_DIM ** -0.5)).astype(BF16)
    heads = []
    for h in range(X_HEADS):
        sl = slice(h * X_HEAD_DIM, (h + 1) * X_HEAD_DIM)
        sc = lax.dot_general(q[:, sl], kx_ref[:, sl], (((1,), (1,)), ((), ())), preferred_element_type=F32)
        e = jnp.exp(sc - jnp.max(sc, axis=1, keepdims=True))
        probs = (e * (1.0 / jnp.sum(e, axis=1, keepdims=True))).astype(BF16)
        heads.append(jnp.dot(probs, vx_ref[:, sl], preferred_element_type=F32).astype(BF16))
    xa = jnp.dot(jnp.concatenate(heads, axis=1), wxo_ref[...], preferred_element_type=F32)
    h2 = _layer_norm(alpha * h1 + xa, ln2_g, ln2_b)
    h_out[:, :D_MODEL] = h2

    h_hi = h2.astype(BF16)
    h_lo = (h2 - h_hi.astype(F32)).astype(BF16)
    logits = (jnp.dot(h_hi, wr_hi_ref[...], preferred_element_type=F32)
              + jnp.dot(h_lo, wr_hi_ref[...], preferred_element_type=F32)
              + jnp.dot(h_hi, wr_lo_ref[...], preferred_element_type=F32)) + rbias_ref[...]

    lane = lax.broadcasted_iota(I32, (t, LANES), 1)
    neg = -jnp.inf
    is_group = lane < N_GROUPS
    gl = jnp.where(is_group, logits, neg)
    g_max = jnp.max(gl, axis=1, keepdims=True)
    lane_f = lane.astype(F32)
    no_lane = float(LANES)
    g_sel = jnp.min(jnp.where(gl == g_max, lane_f, no_lane), axis=1, keepdims=True)
    p_group = 1.0 / jnp.sum(jnp.where(is_group, jnp.exp(logits - g_max), 0.0), axis=1, keepdims=True)
    lane_group = ((lane - ROUTER_EXPERT_LANE0) >> 3).astype(F32)
    in_group = (lane >= ROUTER_EXPERT_LANE0) & (lane_group == g_sel)
    el = jnp.where(in_group, logits, neg)
    v1 = jnp.max(el, axis=1, keepdims=True)
    i1 = jnp.min(jnp.where(el == v1, lane_f, no_lane), axis=1, keepdims=True)
    el2 = jnp.where(lane_f == i1, neg, el)
    v2 = jnp.max(el2, axis=1, keepdims=True)
    i2 = jnp.min(jnp.where(el2 == v2, lane_f, no_lane), axis=1, keepdims=True)
    e2 = jnp.exp(v2 - v1)
    w1 = p_group * (1.0 / (1.0 + e2))
    w2 = p_group * (e2 / (1.0 + e2))

    onehot = jnp.where(lane_f == g_sel, 1.0, 0.0)
    tri = jnp.where(lax.broadcasted_iota(I32, (t, t), 1) < lax.broadcasted_iota(I32, (t, t), 0), 1.0, 0.0).astype(BF16)
    before = jnp.dot(tri, onehot.astype(BF16), preferred_element_type=F32) + base_ref[0:1, :]
    rank = jnp.sum(jnp.where(lane_f == g_sel, before, 0.0), axis=1, keepdims=True)
    total = base_ref[0:1, :] + jnp.sum(onehot, axis=0, keepdims=True)
    base_ref[...] = jnp.broadcast_to(total, base_ref.shape)
    count_out[...] = jnp.broadcast_to(total, count_out.shape)

    logit_lane = lane_f + (ROUTER_EXPERT_LANE0 - REC_WEIGHT_LANE0) + EXPERTS_PER_GROUP * g_sel
    rec = jnp.where(logit_lane == i1, w1, jnp.where(logit_lane == i2, w2, 0.0))
    rec = jnp.where((lane >= REC_WEIGHT_LANE0) & (lane < REC_WEIGHT_LANE0 + EXPERTS_PER_GROUP), rec, 0.0)
    rec = jnp.where(lane == REC_GROUP_LANE, g_sel, rec)
    rec = jnp.where(lane == REC_RANK_LANE, rank, rec)
    h_out[:, D_MODEL:] = rec


def _post(alpha, x2, att, hm, kx, vx, w_gates, w_att, w_ml, w_mix, w_xq, w_xo, w_router, b_router, vecs,
          batch, seq, mem_len):
    n = x2.shape[0]
    t = min(512, seq)
    s_tiles = seq // t
    row = lambda w: pl.BlockSpec((t, w), lambda b, s: (b * s_tiles + s, 0))
    memblk = pl.BlockSpec((mem_len, D_MODEL), lambda b, s: (b, 0))
    w_hi = w_router.astype(BF16)
    w_lo = (w_router - w_hi.astype(F32)).astype(BF16)
    sq = _const_spec((D_MODEL, D_MODEL))
    return pl.pallas_call(
        functools.partial(_post_kernel, alpha),
        grid=(batch, s_tiles),
        in_specs=[row(D_MODEL), row(ATT_Q_W), row(ML_V_W), memblk, memblk,
                  _const_spec((D_MODEL, 3 * D_MODEL)), sq, sq, sq, sq, sq,
                  _const_spec((D_MODEL, LANES)), _const_spec((D_MODEL, LANES)),
                  _const_spec((SUBLANES, D_MODEL)), _const_spec((1, LANES))],
        out_specs=[row(ROW_W), _const_spec((SUBLANES, LANES))],
        out_shape=[jax.ShapeDtypeStruct((n, ROW_W), F32),
                   jax.ShapeDtypeStruct((SUBLANES, LANES), F32)],
        scratch_shapes=[pltpu.VMEM((SUBLANES, LANES), F32)],
        compiler_params=_params("arbitrary", "arbitrary"),
        name="post",
    )(x2, att, hm, kx, vx, w_gates, w_att, w_ml, w_mix, w_xq, w_xo, w_hi, w_lo, vecs, b_router)


DMA_UNROLL = 8


def _scatter_kernel(dest_ref, padend_ref, h_hbm, xs_out, zero_buf, sem, zero_sem):
    i = pl.program_id(0)
    steps = pl.num_programs(0)
    ts = h_hbm.shape[0] // steps

    @pl.when(i == 0)
    def _():
        zero_buf[...] = jnp.zeros_like(zero_buf)

        def clear(start):
            start = pl.multiple_of(start, GROUP_BLOCK)
            return pltpu.make_async_copy(zero_buf, xs_out.at[pl.ds(start, GROUP_BLOCK), :], zero_sem)

        def nonempty(g):
            return padend_ref[g] > (padend_ref[g - 1] if g else 0)

        used = padend_ref[N_GROUPS - 1]

        def unused(b):
            return used + b * GROUP_BLOCK < xs_out.shape[0]

        for phase in ("start", "wait"):
            for g in range(N_GROUPS):
                @pl.when(nonempty(g))
                def _():
                    getattr(clear(padend_ref[g] - GROUP_BLOCK), phase)()

                @pl.when(unused(g))
                def _():
                    getattr(clear(used + g * GROUP_BLOCK), phase)()

    def tile_done():
        return pltpu.make_async_copy(h_hbm.at[pl.ds(0, ts), :], xs_out.at[pl.ds(0, ts), :], sem)

    def issue(j, carry):
        for u in range(DMA_UNROLL):
            t = i * ts + j * DMA_UNROLL + u
            pltpu.make_async_copy(h_hbm.at[pl.ds(t, 1), :], xs_out.at[pl.ds(dest_ref[t], 1), :], sem).start()
        return carry

    lax.fori_loop(0, ts // DMA_UNROLL, issue, 0)

    @pl.when(i > 0)
    def _():
        tile_done().wait()

    @pl.when(i == steps - 1)
    def _():
        tile_done().wait()


def _moe_scatter(dest, pad_end, h2aug, rows):
    n = h2aug.shape[0]
    ts = min(1024, n)
    grid_spec = pltpu.PrefetchScalarGridSpec(
        num_scalar_prefetch=2,
        grid=(n // ts,),
        in_specs=[pl.BlockSpec(memory_space=pl.ANY)],
        out_specs=pl.BlockSpec(memory_space=pl.ANY),
        scratch_shapes=[pltpu.VMEM((GROUP_BLOCK, ROW_W), F32),
                        pltpu.SemaphoreType.DMA(()), pltpu.SemaphoreType.DMA(())],
    )
    return pl.pallas_call(
        _scatter_kernel,
        grid_spec=grid_spec,
        out_shape=jax.ShapeDtypeStruct((rows, ROW_W), F32),
        compiler_params=_params("arbitrary"),
        name="moe_scatter",
    )(dest, pad_end, h2aug)


def _group_kernel(bgrp_ref, nused_ref, x_ref, wgu_ref, wd_ref, y_ref):
    del bgrp_ref
    i = pl.program_id(0)

    @pl.when(i < nused_ref[0])
    def _():
        xb = x_ref[:, :D_MODEL].astype(BF16)
        acc = None
        for e in range(EXPERTS_PER_GROUP):
            gu = jnp.dot(xb, wgu_ref[0, e], preferred_element_type=F32)
            gate = gu[:, :D_EXPERT]
            hb = (gate * jax.nn.sigmoid(gate) * gu[:, D_EXPERT:]).astype(BF16)
            ye = jnp.dot(hb, wd_ref[0, e], preferred_element_type=F32)
            lane = D_MODEL + REC_WEIGHT_LANE0 + e
            w = x_ref[:, lane:lane + 1]
            term = jnp.where(w != 0.0, w * ye, 0.0)
            acc = term if acc is None else acc + term
        y_ref[...] = acc

    @pl.when(i >= nused_ref[0])
    def _():
        y_ref[...] = jnp.zeros_like(y_ref)


def _moe_groups(block_group, n_used, xs, w_gu, w_down):
    rows = xs.shape[0]
    e = EXPERTS_PER_GROUP
    grid_spec = pltpu.PrefetchScalarGridSpec(
        num_scalar_prefetch=2,
        grid=(rows // GROUP_BLOCK,),
        in_specs=[pl.BlockSpec((GROUP_BLOCK, ROW_W), lambda i, *_: (i, 0)),
                  pl.BlockSpec((1, e, D_MODEL, 2 * D_EXPERT), lambda i, bg, nu: (bg[i], 0, 0, 0)),
                  pl.BlockSpec((1, e, D_EXPERT, D_MODEL), lambda i, bg, nu: (bg[i], 0, 0, 0))],
        out_specs=pl.BlockSpec((GROUP_BLOCK, D_MODEL), lambda i, *_: (i, 0)),
    )
    return pl.pallas_call(
        _group_kernel,
        grid_spec=grid_spec,
        out_shape=jax.ShapeDtypeStruct((rows, D_MODEL), F32),
        compiler_params=_params("arbitrary"),
        name="moe_groups",
    )(block_group, n_used, xs, w_gu, w_down)


def _combine_kernel(alpha, dest_ref, h_ref, vec_ref, y_hbm, o_ref, ybuf, sems):
    tc = h_ref.shape[0]
    i = pl.program_id(0)
    steps = pl.num_programs(0)

    def gather(tile, slot):
        def issue(j, carry):
            for u in range(DMA_UNROLL):
                t = j * DMA_UNROLL + u
                pltpu.make_async_copy(y_hbm.at[pl.ds(dest_ref[tile * tc + t], 1), :],
                                      ybuf.at[slot, pl.ds(t, 1), :], sems.at[slot]).start()
            return carry
        lax.fori_loop(0, tc // DMA_UNROLL, issue, 0)

    @pl.when(i == 0)
    def _():
        gather(0, 0)

    @pl.when(i + 1 < steps)
    def _():
        gather(i + 1, (i + 1) % 2)

    slot = i % 2
    pltpu.make_async_copy(y_hbm.at[pl.ds(0, tc), :], ybuf.at[slot], sems.at[slot]).wait()
    o_ref[...] = _layer_norm(alpha * h_ref[...] + ybuf[slot], vec_ref[0:1, :], vec_ref[1:2, :])


def _moe_combine(alpha, dest, h2aug, vecs, y):
    n = h2aug.shape[0]
    tc = min(512, n)
    grid_spec = pltpu.PrefetchScalarGridSpec(
        num_scalar_prefetch=1,
        grid=(n // tc,),
        in_specs=[pl.BlockSpec((tc, D_MODEL), lambda i, *_: (i, 0)),
                  pl.BlockSpec((SUBLANES, D_MODEL), lambda i, *_: (0, 0)),
                  pl.BlockSpec(memory_space=pl.ANY)],
        out_specs=pl.BlockSpec((tc, D_MODEL), lambda i, *_: (i, 0)),
        scratch_shapes=[pltpu.VMEM((2, tc, D_MODEL), F32), pltpu.SemaphoreType.DMA((2,))],
    )
    return pl.pallas_call(
        functools.partial(_combine_kernel, alpha),
        grid_spec=grid_spec,
        out_shape=jax.ShapeDtypeStruct((n, D_MODEL), F32),
        compiler_params=_params("arbitrary"),
        name="moe_combine",
    )(dest, h2aug, vecs, y)


def _pad_rows(vectors):
    rows = [v.astype(F32)[None, :] for v in vectors]
    rows.append(jnp.zeros((SUBLANES - len(rows), vectors[0].shape[0]), F32))
    return jnp.concatenate(rows, axis=0)


def _layer(alpha, h, mem2, pos2, batch, seq, mem_len, w_in, attn_sinks, conv_w, conv_b, b_igate, b_fgate,
           ml_norm_g, w_att_branch, w_ml_branch, w_mix_out, ln1_g, ln1_b, w_xq, w_xkv, w_xo, ln2_g, ln2_b,
           w_router_group, b_router_group, w_router_expert, b_router_expert, w_gate, w_up, w_down,
           ln3_g, ln3_b):
    n = h.shape[0]
    q, kv, mq, mk, mv, gates = _in_proj(h, pos2, w_in, conv_w, conv_b, b_igate, b_fgate, batch, seq)
    att = _swa(q, kv, attn_sinks, batch, seq)
    hm = _mlstm(mq, mk, mv, gates, batch, seq)
    kx, vx = _mem_kv(mem2, w_xkv, batch, mem_len)

    o_mo = ATT_Q_W + 2 * ATT_KV_W + 2 * ML_QK_W + ML_V_W
    o_ga = o_mo + ML_V_W + 2 * ML_HEADS
    w_gates = jnp.concatenate([w_in[:, o_mo:o_mo + ML_V_W], w_in[:, o_ga:]], axis=1).astype(BF16)
    pad = LANES - N_GROUPS - N_EXPERTS
    w_router = jnp.pad(jnp.concatenate([w_router_group, w_router_expert], axis=1).astype(F32), ((0, 0), (0, pad)))
    b_router = jnp.pad(jnp.concatenate([b_router_group, b_router_expert]).astype(F32), (0, pad))[None, :]
    vecs = _pad_rows([ml_norm_g, ln1_g, ln1_b, ln2_g, ln2_b])
    h2aug, counts = _post(alpha, h, att, hm, kx, vx, w_gates, w_att_branch.astype(BF16),
                          w_ml_branch.astype(BF16), w_mix_out.astype(BF16), w_xq.astype(BF16),
                          w_xo.astype(BF16), w_router, b_router, vecs, batch, seq, mem_len)

    counts = counts[0, :N_GROUPS].astype(I32)
    padded = ((counts + GROUP_BLOCK - 1) // GROUP_BLOCK) * GROUP_BLOCK
    pad_end = jnp.cumsum(padded).astype(I32)
    pad_start = pad_end - padded
    n_blocks = n // GROUP_BLOCK + N_GROUPS
    block_start = jnp.arange(n_blocks, dtype=I32) * GROUP_BLOCK
    block_group = jnp.sum((pad_end[None, :] <= block_start[:, None]).astype(I32), axis=1)
    block_group = jnp.minimum(block_group, N_GROUPS - 1)
    n_used = pad_end[-1:] // GROUP_BLOCK
    group = h2aug[:, D_MODEL + REC_GROUP_LANE].astype(I32)
    rank = h2aug[:, D_MODEL + REC_RANK_LANE].astype(I32)
    dest = pad_start[group] + rank

    xs = _moe_scatter(dest, pad_end, h2aug, n_blocks * GROUP_BLOCK)
    e = EXPERTS_PER_GROUP
    w_gu = jnp.concatenate([w_gate, w_up], axis=2).astype(BF16).reshape(N_GROUPS, e, D_MODEL, 2 * D_EXPERT)
    w_dn = w_down.astype(BF16).reshape(N_GROUPS, e, D_EXPERT, D_MODEL)
    y = _moe_groups(block_group, n_used, xs, w_gu, w_dn)
    return _moe_combine(alpha, dest, h2aug, _pad_rows([ln3_g, ln3_b]), y)


def kernel(x, mem, positions, w_in, attn_sinks, conv_w, conv_b, b_igate, b_fgate, ml_norm_g, w_att_branch, w_ml_branch, w_mix_out, ln1_g, ln1_b, w_xq, w_xkv, w_xo, ln2_g, ln2_b, w_router_group, b_router_group, w_router_expert, b_router_expert, w_gate, w_up, w_down, ln3_g, ln3_b):
    batch, seq, d = x.shape
    mem_len = mem.shape[1]
    depth = w_in.shape[0]
    alpha = (2 * depth) ** 0.25
    h = x.reshape(batch * seq, d)
    mem2 = mem.reshape(batch * mem_len, d)
    pos2 = positions.reshape(batch * seq, 1)
    stacked = (w_in, attn_sinks, conv_w, conv_b, b_igate, b_fgate, ml_norm_g, w_att_branch, w_ml_branch,
               w_mix_out, ln1_g, ln1_b, w_xq, w_xkv, w_xo, ln2_g, ln2_b, w_router_group, b_router_group,
               w_router_expert, b_router_expert, w_gate, w_up, w_down, ln3_g, ln3_b)
    for l in range(depth):
        h = _layer(alpha, h, mem2, pos2, batch, seq, mem_len, *(w[l] for w in stacked))
    return h.reshape(batch, seq, d)
```

```python
import functools

import jax
import jax.numpy as jnp
from jax import lax
from jax.experimental import pallas as pl
from jax.experimental.pallas import tpu as pltpu

F32 = jnp.float32
BF16 = jnp.bfloat16
I32 = jnp.int32

D_MODEL = 1024
ATT_HEADS = 16
ATT_KV_HEADS = 2
ATT_HEAD_DIM = 64
ATT_BLOCK = 128
ROPE_THETA = 10000.0
ML_HEADS = 4
ML_DQK = 128
ML_DV = 256
ML_CHUNK = 64
CONV_WIDTH = 4
X_HEADS = 4
X_HEAD_DIM = D_MODEL // X_HEADS
N_GROUPS = 8
EXPERTS_PER_GROUP = 8
N_EXPERTS = N_GROUPS * EXPERTS_PER_GROUP
TOP_K = 2
D_EXPERT = 256
MOE_BLOCK = 128
LN_EPS = 1e-5
RMS_EPS = 1e-6

ATT_Q_W = ATT_HEADS * ATT_HEAD_DIM
ATT_KV_W = ATT_KV_HEADS * ATT_HEAD_DIM
ML_QK_W = ML_HEADS * ML_DQK
ML_V_W = ML_HEADS * ML_DV

LANES = 128
SUBLANES = 8
VMEM_LIMIT_BYTES = 56 * 1024 * 1024

ROUTER_EXPERT_LANE0 = N_GROUPS
REC_GROUP_LANE = 0
REC_RANK_LANE = 1
REC_WEIGHT_LANE0 = 8
ROW_W = D_MODEL + LANES
GROUP_BLOCK = 512
LOG2_E = 1.4426950408889634
ML_SEQS_PER_STEP = 2


def _params(*semantics):
    return pltpu.CompilerParams(dimension_semantics=semantics, vmem_limit_bytes=VMEM_LIMIT_BYTES)


def _const_spec(shape):
    zeros = (0,) * len(shape)
    return pl.BlockSpec(shape, lambda *_: zeros, pipeline_mode=pl.Buffered(1))


def _layer_norm(z, g, b):
    mu = jnp.mean(z, axis=-1, keepdims=True)
    zc = z - mu
    var = jnp.mean(zc * zc, axis=-1, keepdims=True)
    return zc * lax.rsqrt(var + LN_EPS) * g + b


def _inproj_kernel(x_ref, pos_ref, inv_ref, wrope_ref, wv_ref, wmqk_ref, wmv_ref, wg_ref,
                   convw_ref, convb_ref, gbias_ref,
                   q_out, kv_out, mq_out, mk_out, mv_out, g_out, prev_ref):
    t = x_ref.shape[0]
    xb = x_ref[...].astype(BF16)

    half = ATT_HEAD_DIM // 2
    ang = pos_ref[...].astype(F32) * inv_ref[0:1, :]
    sin_cos = jnp.sin(ang + inv_ref[1:2, :])
    lane = lax.broadcasted_iota(I32, (t, LANES), 1)
    first_half = (lane % ATT_HEAD_DIM) < half
    cos = jnp.where(first_half, pltpu.roll(sin_cos, LANES - half, 1), sin_cos)
    sin_signed = jnp.where(first_half, -sin_cos, pltpu.roll(sin_cos, half, 1))

    def rope(z):
        partner = jnp.where(first_half, pltpu.roll(z, LANES - ATT_HEAD_DIM // 2, 1),
                            pltpu.roll(z, ATT_HEAD_DIM // 2, 1))
        return z * cos + partner * sin_signed

    qk = jnp.dot(xb, wrope_ref[...], preferred_element_type=F32)
    q_scale = ATT_HEAD_DIM ** -0.5 * LOG2_E
    for g in range(ATT_Q_W // LANES):
        sl = slice(g * LANES, (g + 1) * LANES)
        q_out[:, sl] = (rope(qk[:, sl]) * q_scale).astype(BF16)
    for g in range(2 * ATT_KV_W // LANES):
        sl = slice(ATT_Q_W + g * LANES, ATT_Q_W + (g + 1) * LANES)
        kv_out[:, g * LANES:(g + 1) * LANES] = rope(qk[:, sl]).astype(BF16)
    kv_out[:, 2 * ATT_KV_W:] = jnp.dot(xb, wv_ref[...], preferred_element_type=F32).astype(BF16)

    @pl.when(pl.program_id(1) == 0)
    def _():
        prev_ref[...] = jnp.zeros_like(prev_ref)

    pre = jnp.dot(xb, wmqk_ref[...], preferred_element_type=F32)
    prev8 = prev_ref[...]
    row8 = lax.broadcasted_iota(I32, (SUBLANES, 2 * ML_QK_W), 0)
    w_now = convw_ref[CONV_WIDTH - 1:CONV_WIDTH, :]
    acc = pre * w_now + convb_ref[...]
    top = pre[0:SUBLANES] * w_now + convb_ref[...]
    for d in range(1, CONV_WIDTH):
        w_d = convw_ref[CONV_WIDTH - 1 - d:CONV_WIDTH - d, :]
        shifted = pltpu.roll(pre, d, 0)
        acc = acc + shifted * w_d
        top = top + jnp.where(row8 < d, pltpu.roll(prev8, d, 0), shifted[0:SUBLANES]) * w_d
    prev_ref[...] = pre[t - SUBLANES:t]
    conv = jnp.concatenate([top, acc[SUBLANES:]], axis=0)
    act = conv * jax.nn.sigmoid(conv)
    mq_out[...] = act[:, :ML_QK_W].astype(BF16)
    mk_out[...] = (act[:, ML_QK_W:] * (ML_DQK ** -0.5)).astype(BF16)

    mv_out[...] = jnp.dot(xb, wmv_ref[...], preferred_element_type=F32).astype(BF16)

    gates = jnp.dot(xb, wg_ref[...], preferred_element_type=F32) + gbias_ref[...]
    g_out[...] = jnp.where(lane < ML_HEADS, gates, jax.nn.log_sigmoid(gates))


def _in_proj(x2, pos2, w_in, conv_w, conv_b, b_igate, b_fgate, batch, seq):
    n = x2.shape[0]
    t = min(512, seq)
    s_tiles = seq // t

    def dup(w):
        h = ATT_HEAD_DIM
        return jnp.concatenate([w[:, :h], w[:, :h], w[:, h:], w[:, h:]], axis=1)

    o_k = ATT_Q_W
    o_v = o_k + ATT_KV_W
    o_mq = o_v + ATT_KV_W
    o_mv = o_mq + 2 * ML_QK_W
    o_mo = o_mv + ML_V_W
    o_mi = o_mo + ML_V_W
    w_rope = jnp.concatenate([w_in[:, :o_k], dup(w_in[:, o_k:o_v])], axis=1).astype(BF16)
    w_v = dup(w_in[:, o_v:o_mq]).astype(BF16)
    w_mqk = w_in[:, o_mq:o_mv].astype(BF16)
    w_mv = w_in[:, o_mv:o_mo].astype(BF16)
    w_g = jnp.pad(w_in[:, o_mi:o_mi + 2 * ML_HEADS], ((0, 0), (0, LANES - 2 * ML_HEADS))).astype(BF16)
    gbias = jnp.pad(jnp.concatenate([b_igate, b_fgate]).astype(F32), (0, LANES - 2 * ML_HEADS))[None, :]
    half = ATT_HEAD_DIM // 2
    inv = ROPE_THETA ** (-jnp.arange(half, dtype=F32) / half)
    quarter_turn = jnp.where((jnp.arange(LANES) % ATT_HEAD_DIM) < half, 0.0, jnp.pi / 2).astype(F32)
    inv = jnp.stack([jnp.tile(inv, LANES // half), quarter_turn])

    row = lambda w: pl.BlockSpec((t, w), lambda b, s: (b * s_tiles + s, 0))
    outs = pl.pallas_call(
        _inproj_kernel,
        grid=(batch, s_tiles),
        in_specs=[row(D_MODEL), row(1), _const_spec((2, LANES)),
                  _const_spec(w_rope.shape), _const_spec(w_v.shape), _const_spec(w_mqk.shape),
                  _const_spec(w_mv.shape), _const_spec(w_g.shape),
                  _const_spec((CONV_WIDTH, 2 * ML_QK_W)), _const_spec((1, 2 * ML_QK_W)),
                  _const_spec((1, LANES))],
        out_specs=[row(ATT_Q_W), row(4 * ATT_KV_W), row(ML_QK_W), row(ML_QK_W), row(ML_V_W), row(LANES)],
        out_shape=[jax.ShapeDtypeStruct((n, ATT_Q_W), BF16),
                   jax.ShapeDtypeStruct((n, 4 * ATT_KV_W), BF16),
                   jax.ShapeDtypeStruct((n, ML_QK_W), BF16),
                   jax.ShapeDtypeStruct((n, ML_QK_W), BF16),
                   jax.ShapeDtypeStruct((n, ML_V_W), BF16),
                   jax.ShapeDtypeStruct((n, LANES), F32)],
        scratch_shapes=[pltpu.VMEM((SUBLANES, 2 * ML_QK_W), F32)],
        compiler_params=_params("arbitrary", "arbitrary"),
        name="in_proj",
    )(x2, pos2, inv, w_rope, w_v, w_mqk, w_mv, w_g, conv_w.astype(F32), conv_b.astype(F32)[None, :], gbias)
    return outs


def _swa_kernel(sink_ref, q_ref, kvc_ref, kvp_ref, o_ref, kv_buf):
    tq = q_ref.shape[0]
    nblk = tq // ATT_BLOCK
    first = pl.program_id(1) == 0
    kw = 2 * ATT_KV_W
    kv_buf[0:ATT_BLOCK, :] = kvp_ref[...]
    kv_buf[ATT_BLOCK:, :] = kvc_ref[...]

    nk = 2 * ATT_BLOCK
    kc = lax.broadcasted_iota(I32, (nk, ATT_BLOCK), 0)
    qi = lax.broadcasted_iota(I32, (nk, ATT_BLOCK), 1)
    band = (kc > qi) & (kc <= qi + ATT_BLOCK)
    lo = lax.broadcasted_iota(I32, (nk, LANES), 1) < ATT_HEAD_DIM
    dim_lo = lax.broadcasted_iota(I32, (LANES, ATT_BLOCK), 0) < ATT_HEAD_DIM

    def block(j, carry):
        r0 = pl.multiple_of(j * ATT_BLOCK, ATT_BLOCK)
        valid = band & (kc >= jnp.where(first & (j == 0), ATT_BLOCK, 0))
        kv = kv_buf[pl.ds(r0, nk), :]
        for kvh in range(ATT_KV_HEADS):
            kk = kv[:, kvh * LANES:(kvh + 1) * LANES]
            vv = kv[:, kw + kvh * LANES:kw + (kvh + 1) * LANES].astype(F32)
            zero = jnp.zeros_like(kk)
            k2 = jnp.concatenate([jnp.where(lo, kk, zero), jnp.where(lo, zero, kk)], axis=0)
            v2t = jnp.concatenate([jnp.where(lo, vv, 0.0), jnp.where(lo, 0.0, vv)], axis=0).T.astype(BF16)
            pairs = ATT_HEADS // ATT_KV_HEADS // 2
            for pp in range(pairs):
                p = kvh * pairs + pp
                qp = q_ref[pl.ds(r0, ATT_BLOCK), p * LANES:(p + 1) * LANES]
                st = lax.dot_general(k2, qp, (((1,), (1,)), ((), ())), preferred_element_type=F32)
                e_heads, recip = [], []
                for hh in range(2):
                    sink = sink_ref[2 * p + hh]
                    s_h = jnp.where(valid, st[hh * nk:(hh + 1) * nk], -jnp.inf)
                    m = jnp.maximum(jnp.max(s_h, axis=0, keepdims=True), sink)
                    e = jnp.exp2(s_h - m)
                    den = jnp.sum(e, axis=0, keepdims=True) + jnp.exp2(sink - m)
                    e_heads.append(e.astype(BF16))
                    recip.append(1.0 / den)
                out_t = jnp.dot(v2t, jnp.concatenate(e_heads, axis=0), preferred_element_type=F32)
                out_t = out_t * jnp.where(dim_lo, recip[0], recip[1])
                o_ref[pl.ds(r0, ATT_BLOCK), p * LANES:(p + 1) * LANES] = out_t.T.astype(BF16)
        return carry

    lax.fori_loop(0, nblk, block, 0)


def _swa(q, kv, sinks, batch, seq):
    n = q.shape[0]
    tq = min(512, seq)
    s_tiles = seq // tq
    per = tq // ATT_BLOCK
    blocks_per_seq = seq // ATT_BLOCK
    grid_spec = pltpu.PrefetchScalarGridSpec(
        num_scalar_prefetch=1,
        grid=(batch, s_tiles),
        in_specs=[pl.BlockSpec((tq, ATT_Q_W), lambda b, s, *_: (b * s_tiles + s, 0)),
                  pl.BlockSpec((tq, 4 * ATT_KV_W), lambda b, s, *_: (b * s_tiles + s, 0)),
                  pl.BlockSpec((ATT_BLOCK, 4 * ATT_KV_W),
                               lambda b, s, *_: (b * blocks_per_seq + jnp.maximum(s * per - 1, 0), 0))],
        out_specs=pl.BlockSpec((tq, ATT_Q_W), lambda b, s, *_: (b * s_tiles + s, 0)),
        scratch_shapes=[pltpu.VMEM((tq + ATT_BLOCK, 4 * ATT_KV_W), BF16)],
    )
    return pl.pallas_call(
        _swa_kernel,
        grid_spec=grid_spec,
        out_shape=jax.ShapeDtypeStruct((n, ATT_Q_W), BF16),
        compiler_params=_params("arbitrary", "arbitrary"),
        name="swa",
    )(sinks.astype(F32) * LOG2_E, q, kv, kv)


def _mlstm_kernel(q_ref, k_ref, v_ref, g_ref, o_ref, c_ref, n_ref, m_ref):
    nb, tm = q_ref.shape[0], q_ref.shape[1]
    L = ML_CHUNK

    @pl.when(pl.program_id(1) == 0)
    def _():
        c_ref[...] = jnp.zeros_like(c_ref)
        n_ref[...] = jnp.zeros_like(n_ref)
        m_ref[...] = jnp.zeros_like(m_ref)

    causal = lax.broadcasted_iota(I32, (L, L), 1) <= lax.broadcasted_iota(I32, (L, L), 0)
    rowi = lax.broadcasted_iota(I32, (L, LANES), 0)

    def scan_rows(x, op, identity):
        d = 1
        while d < L:
            x = op(x, jnp.where(rowi >= d, pltpu.roll(x, d, 0), identity))
            d *= 2
        return x

    def chunk(c, carry):
        r0 = pl.multiple_of(c * L, L)
        heads = [(bb, h) for bb in range(nb) for h in range(ML_HEADS)]
        rows = pl.ds(r0, L)

        gate = []
        for bb in range(nb):
            g = g_ref[bb, rows, :]
            b = pltpu.roll(scan_rows(g, jnp.add, 0.0), LANES - ML_HEADS, 1)
            a = g - b
            a_max = scan_rows(a, jnp.maximum, -jnp.inf)
            m_prev = m_ref[bb, 0:1, :]
            m_t = b + jnp.maximum(m_prev, a_max)
            u = b - m_t
            b_end = b[L - 1:L, :]
            m_new = b_end + jnp.maximum(m_prev, a_max[L - 1:L, :])
            m_ref[bb, 0:1, :] = m_new
            gate.append(dict(u=u, w_inter=jnp.exp(u + m_prev), floor=jnp.exp(-m_t),
                             w_state=jnp.exp(b_end + m_prev - m_new), w_k=jnp.exp(b_end + a - m_new),
                             a_t=a.T))

        nt = (((1,), (1,)), ((), ()))
        for bb, h in heads:
            gt = gate[bb]
            idx = bb * ML_HEADS + h
            col = slice(h, h + 1)
            qh = q_ref[bb, rows, h * ML_DQK:(h + 1) * ML_DQK]
            kh = k_ref[bb, rows, h * ML_DQK:(h + 1) * ML_DQK]
            vh = v_ref[bb, rows, h * ML_DV:(h + 1) * ML_DV]
            c_prev = c_ref[idx]
            n_prev = n_ref[idx, 0:1, :]

            w_intra = jnp.exp(jnp.where(causal, gt["u"][:, col] + gt["a_t"][col, :], -jnp.inf))
            s = lax.dot_general(qh, kh, nt, preferred_element_type=F32) * w_intra
            w_inter = gt["w_inter"][:, col]
            num = (w_inter * jnp.dot(qh, c_prev.astype(BF16), preferred_element_type=F32)
                   + jnp.dot(s.astype(BF16), vh, preferred_element_type=F32))
            den = (w_inter * jnp.sum(qh.astype(F32) * n_prev, axis=1, keepdims=True)
                   + jnp.sum(s, axis=1, keepdims=True))
            hval = num / jnp.maximum(jnp.abs(den), gt["floor"][:, col])
            hval = hval * lax.rsqrt(jnp.mean(hval * hval, axis=1, keepdims=True) + RMS_EPS)
            o_ref[bb, rows, h * ML_DV:(h + 1) * ML_DV] = hval.astype(BF16)

            w_state = gt["w_state"][:, col]
            kw = kh.astype(F32) * gt["w_k"][:, col]
            c_ref[idx] = w_state * c_prev + jnp.dot(kw.T.astype(BF16), vh, preferred_element_type=F32)
            n_ref[idx, 0:1, :] = w_state * n_prev + jnp.sum(kw, axis=0, keepdims=True)
        return carry

    lax.fori_loop(0, tm // L, chunk, 0)


def _mlstm(mq, mk, mv, gates, batch, seq):
    n = mq.shape[0]
    tm = min(512, seq)
    nb = ML_SEQS_PER_STEP if batch % ML_SEQS_PER_STEP == 0 else 1
    blk = lambda w: pl.BlockSpec((nb, tm, w), lambda b, s: (b, s, 0))
    seqs = lambda t: t.reshape(batch, seq, t.shape[-1])
    out = pl.pallas_call(
        _mlstm_kernel,
        grid=(batch // nb, seq // tm),
        in_specs=[blk(ML_QK_W), blk(ML_QK_W), blk(ML_V_W), blk(LANES)],
        out_specs=blk(ML_V_W),
        out_shape=jax.ShapeDtypeStruct((batch, seq, ML_V_W), BF16),
        scratch_shapes=[pltpu.VMEM((nb * ML_HEADS, ML_DQK, ML_DV), F32),
                        pltpu.VMEM((nb * ML_HEADS, SUBLANES, ML_DQK), F32),
                        pltpu.VMEM((nb, SUBLANES, LANES), F32)],
        compiler_params=_params("arbitrary", "arbitrary"),
        name="mlstm",
    )(seqs(mq), seqs(mk), seqs(mv), seqs(gates))
    return out.reshape(n, ML_V_W)


def _memkv_kernel(mem_ref, w_ref, k_out, v_out):
    kv = jnp.dot(mem_ref[...].astype(BF16), w_ref[...], preferred_element_type=F32)
    k_out[...] = kv[:, :D_MODEL].astype(BF16)
    v_out[...] = kv[:, D_MODEL:].astype(BF16)


def _mem_kv(mem2, w_xkv, batch, mem_len):
    blk = pl.BlockSpec((mem_len, D_MODEL), lambda b: (b, 0))
    return pl.pallas_call(
        _memkv_kernel,
        grid=(batch,),
        in_specs=[blk, _const_spec((D_MODEL, 2 * D_MODEL))],
        out_specs=[blk, blk],
        out_shape=[jax.ShapeDtypeStruct((batch * mem_len, D_MODEL), BF16)] * 2,
        compiler_params=_params("arbitrary"),
        name="mem_kv",
    )(mem2, w_xkv.astype(BF16))


def _post_kernel(alpha, x_ref, att_ref, hm_ref, kx_ref, vx_ref, wgate_ref, watt_ref, wml_ref, wmix_ref,
                 wxq_ref, wxo_ref, wr_hi_ref, wr_lo_ref, vec_ref, rbias_ref,
                 h_out, count_out, base_ref):
    t = x_ref.shape[0]
    first = (pl.program_id(0) == 0) & (pl.program_id(1) == 0)

    @pl.when(first)
    def _():
        base_ref[...] = jnp.zeros_like(base_ref)

    x = x_ref[...]
    xb = x.astype(BF16)
    norm_g, ln1_g, ln1_b, ln2_g, ln2_b = (vec_ref[i:i + 1, :] for i in range(5))

    mo = jnp.dot(xb, wgate_ref[:, 0:D_MODEL], preferred_element_type=F32)
    hm = (hm_ref[...].astype(F32) * norm_g * jax.nn.sigmoid(mo)).astype(BF16)
    m_out = jnp.dot(hm, wml_ref[...], preferred_element_type=F32)
    gm = jnp.dot(xb, wgate_ref[:, 2 * D_MODEL:3 * D_MODEL], preferred_element_type=F32)
    y = jax.nn.sigmoid(gm) * m_out
    a_out = jnp.dot(att_ref[...], watt_ref[...], preferred_element_type=F32)
    ga = jnp.dot(xb, wgate_ref[:, D_MODEL:2 * D_MODEL], preferred_element_type=F32)
    y = y + jax.nn.sigmoid(ga) * a_out
    mix = jnp.dot(y.astype(BF16), wmix_ref[...], preferred_element_type=F32)
    h1 = _layer_norm(alpha * x + mix, ln1_g, ln1_b)

    q = (jnp.dot(h1.astype(BF16), wxq_ref[...], preferred_element_type=F32) * (X_HEAD_DIM ** -0.5)).astype(BF16)
    heads = []
    for h in range(X_HEADS):
        sl = slice(h * X_HEAD_DIM, (h + 1) * X_HEAD_DIM)
        sc = lax.dot_general(q[:, sl], kx_ref[:, sl], (((1,), (1,)), ((), ())), preferred_element_type=F32)
        e = jnp.exp(sc - jnp.max(sc, axis=1, keepdims=True))
        probs = (e * (1.0 / jnp.sum(e, axis=1, keepdims=True))).astype(BF16)
        heads.append(jnp.dot(probs, vx_ref[:, sl], preferred_element_type=F32).astype(BF16))
    xa = jnp.dot(jnp.concatenate(heads, axis=1), wxo_ref[...], preferred_element_type=F32)
    h2 = _layer_norm(alpha * h1 + xa, ln2_g, ln2_b)
    h_out[:, :D_MODEL] = h2

    h_hi = h2.astype(BF16)
    h_lo = (h2 - h_hi.astype(F32)).astype(BF16)
    logits = (jnp.dot(h_hi, wr_hi_ref[...], preferred_element_type=F32)
              + jnp.dot(h_lo, wr_hi_ref[...], preferred_element_type=F32)
              + jnp.dot(h_hi, wr_lo_ref[...], preferred_element_type=F32)) + rbias_ref[...]

    lane = lax.broadcasted_iota(I32, (t, LANES), 1)
    neg = -jnp.inf
    is_group = lane < N_GROUPS
    gl = jnp.where(is_group, logits, neg)
    g_max = jnp.max(gl, axis=1, keepdims=True)
    lane_f = lane.astype(F32)
    no_lane = float(LANES)
    g_sel = jnp.min(jnp.where(gl == g_max, lane_f, no_lane), axis=1, keepdims=True)
    p_group = 1.0 / jnp.sum(jnp.where(is_group, jnp.exp(logits - g_max), 0.0), axis=1, keepdims=True)
    lane_group = ((lane - ROUTER_EXPERT_LANE0) >> 3).astype(F32)
    in_group = (lane >= ROUTER_EXPERT_LANE0) & (lane_group == g_sel)
    el = jnp.where(in_group, logits, neg)
    v1 = jnp.max(el, axis=1, keepdims=True)
    i1 = jnp.min(jnp.where(el == v1, lane_f, no_lane), axis=1, keepdims=True)
    el2 = jnp.where(lane_f == i1, neg, el)
    v2 = jnp.max(el2, axis=1, keepdims=True)
    i2 = jnp.min(jnp.where(el2 == v2, lane_f, no_lane), axis=1, keepdims=True)
    e2 = jnp.exp(v2 - v1)
    w1 = p_group * (1.0 / (1.0 + e2))
    w2 = p_group * (e2 / (1.0 + e2))

    onehot = jnp.where(lane_f == g_sel, 1.0, 0.0)
    tri = jnp.where(lax.broadcasted_iota(I32, (t, t), 1) < lax.broadcasted_iota(I32, (t, t), 0), 1.0, 0.0).astype(BF16)
    before = jnp.dot(tri, onehot.astype(BF16), preferred_element_type=F32) + base_ref[0:1, :]
    rank = jnp.sum(jnp.where(lane_f == g_sel, before, 0.0), axis=1, keepdims=True)
    total = base_ref[0:1, :] + jnp.sum(onehot, axis=0, keepdims=True)
    base_ref[...] = jnp.broadcast_to(total, base_ref.shape)
    count_out[...] = jnp.broadcast_to(total, count_out.shape)

    logit_lane = lane_f + (ROUTER_EXPERT_LANE0 - REC_WEIGHT_LANE0) + EXPERTS_PER_GROUP * g_sel
    rec = jnp.where(logit_lane == i1, w1, jnp.where(logit_lane == i2, w2, 0.0))
    rec = jnp.where((lane >= REC_WEIGHT_LANE0) & (lane < REC_WEIGHT_LANE0 + EXPERTS_PER_GROUP), rec, 0.0)
    rec = jnp.where(lane == REC_GROUP_LANE, g_sel, rec)
    rec = jnp.where(lane == REC_RANK_LANE, rank, rec)
    h_out[:, D_MODEL:] = rec


def _post(alpha, x2, att, hm, kx, vx, w_gates, w_att, w_ml, w_mix, w_xq, w_xo, w_router, b_router, vecs,
          batch, seq, mem_len):
    n = x2.shape[0]
    t = min(512, seq)
    s_tiles = seq // t
    row = lambda w: pl.BlockSpec((t, w), lambda b, s: (b * s_tiles + s, 0))
    memblk = pl.BlockSpec((mem_len, D_MODEL), lambda b, s: (b, 0))
    w_hi = w_router.astype(BF16)
    w_lo = (w_router - w_hi.astype(F32)).astype(BF16)
    sq = _const_spec((D_MODEL, D_MODEL))
    return pl.pallas_call(
        functools.partial(_post_kernel, alpha),
        grid=(batch, s_tiles),
        in_specs=[row(D_MODEL), row(ATT_Q_W), row(ML_V_W), memblk, memblk,
                  _const_spec((D_MODEL, 3 * D_MODEL)), sq, sq, sq, sq, sq,
                  _const_spec((D_MODEL, LANES)), _const_spec((D_MODEL, LANES)),
                  _const_spec((SUBLANES, D_MODEL)), _const_spec((1, LANES))],
        out_specs=[row(ROW_W), _const_spec((SUBLANES, LANES))],
        out_shape=[jax.ShapeDtypeStruct((n, ROW_W), F32),
                   jax.ShapeDtypeStruct((SUBLANES, LANES), F32)],
        scratch_shapes=[pltpu.VMEM((SUBLANES, LANES), F32)],
        compiler_params=_params("arbitrary", "arbitrary"),
        name="post",
    )(x2, att, hm, kx, vx, w_gates, w_att, w_ml, w_mix, w_xq, w_xo, w_hi, w_lo, vecs, b_router)


DMA_UNROLL = 8


def _scatter_kernel(dest_ref, padend_ref, h_ref, xs_out, zero_buf, sem, zero_sem):
    i = pl.program_id(0)
    ts = h_ref.shape[0]

    @pl.when(i == 0)
    def _():
        zero_buf[...] = jnp.zeros_like(zero_buf)

        def clear(start):
            start = pl.multiple_of(start, GROUP_BLOCK)
            return pltpu.make_async_copy(zero_buf, xs_out.at[pl.ds(start, GROUP_BLOCK), :], zero_sem)

        def nonempty(g):
            return padend_ref[g] > (padend_ref[g - 1] if g else 0)

        used = padend_ref[N_GROUPS - 1]

        def unused(b):
            return used + b * GROUP_BLOCK < xs_out.shape[0]

        for phase in ("start", "wait"):
            for g in range(N_GROUPS):
                @pl.when(nonempty(g))
                def _():
                    getattr(clear(padend_ref[g] - GROUP_BLOCK), phase)()

                @pl.when(unused(g))
                def _():
                    getattr(clear(used + g * GROUP_BLOCK), phase)()

    def issue(j, carry):
        for u in range(DMA_UNROLL):
            t = j * DMA_UNROLL + u
            pltpu.make_async_copy(h_ref.at[pl.ds(t, 1), :], xs_out.at[pl.ds(dest_ref[i * ts + t], 1), :], sem).start()
        return carry

    lax.fori_loop(0, ts // DMA_UNROLL, issue, 0)

    pltpu.make_async_copy(h_ref, xs_out.at[pl.ds(0, ts), :], sem).wait()


def _moe_scatter(dest, pad_end, h2aug, rows):
    n = h2aug.shape[0]
    ts = min(1024, n)
    grid_spec = pltpu.PrefetchScalarGridSpec(
        num_scalar_prefetch=2,
        grid=(n // ts,),
        in_specs=[pl.BlockSpec((ts, ROW_W), lambda i, *_: (i, 0))],
        out_specs=pl.BlockSpec(memory_space=pl.ANY),
        scratch_shapes=[pltpu.VMEM((GROUP_BLOCK, ROW_W), F32),
                        pltpu.SemaphoreType.DMA(()), pltpu.SemaphoreType.DMA(())],
    )
    return pl.pallas_call(
        _scatter_kernel,
        grid_spec=grid_spec,
        out_shape=jax.ShapeDtypeStruct((rows, ROW_W), F32),
        compiler_params=_params("arbitrary"),
        name="moe_scatter",
    )(dest, pad_end, h2aug)


def _group_kernel(bgrp_ref, nused_ref, x_ref, wgu_ref, wd_ref, y_ref):
    del bgrp_ref
    i = pl.program_id(0)

    @pl.when(i < nused_ref[0])
    def _():
        xb = x_ref[:, :D_MODEL].astype(BF16)
        hidden = []
        for e in range(EXPERTS_PER_GROUP):
            gu = jnp.dot(xb, wgu_ref[0, e], preferred_element_type=F32)
            gate = gu[:, :D_EXPERT]
            lane = D_MODEL + REC_WEIGHT_LANE0 + e
            w = x_ref[:, lane:lane + 1]
            hw = jnp.where(w != 0.0, w * (gate * jax.nn.sigmoid(gate) * gu[:, D_EXPERT:]), 0.0)
            hidden.append(hw.astype(BF16))
        y_ref[...] = jnp.dot(jnp.concatenate(hidden, axis=1), wd_ref[0], preferred_element_type=F32)

    @pl.when(i >= nused_ref[0])
    def _():
        y_ref[...] = jnp.zeros_like(y_ref)


def _moe_groups(block_group, n_used, xs, w_gu, w_down):
    rows = xs.shape[0]
    e = EXPERTS_PER_GROUP
    grid_spec = pltpu.PrefetchScalarGridSpec(
        num_scalar_prefetch=2,
        grid=(rows // GROUP_BLOCK,),
        in_specs=[pl.BlockSpec((GROUP_BLOCK, ROW_W), lambda i, *_: (i, 0)),
                  pl.BlockSpec((1, e, D_MODEL, 2 * D_EXPERT), lambda i, bg, nu: (bg[i], 0, 0, 0)),
                  pl.BlockSpec((1, e * D_EXPERT, D_MODEL), lambda i, bg, nu: (bg[i], 0, 0))],
        out_specs=pl.BlockSpec((GROUP_BLOCK, D_MODEL), lambda i, *_: (i, 0)),
    )
    return pl.pallas_call(
        _group_kernel,
        grid_spec=grid_spec,
        out_shape=jax.ShapeDtypeStruct((rows, D_MODEL), F32),
        compiler_params=_params("arbitrary"),
        name="moe_groups",
    )(block_group, n_used, xs, w_gu, w_down)


def _combine_kernel(alpha, dest_ref, h_ref, vec_ref, y_hbm, o_ref, ybuf, sems):
    tc = h_ref.shape[0]
    i = pl.program_id(0)
    steps = pl.num_programs(0)

    def gather(tile, slot):
        def issue(j, carry):
            for u in range(DMA_UNROLL):
                t = j * DMA_UNROLL + u
                pltpu.make_async_copy(y_hbm.at[pl.ds(dest_ref[tile * tc + t], 1), :],
                                      ybuf.at[slot, pl.ds(t, 1), :], sems.at[slot]).start()
            return carry
        lax.fori_loop(0, tc // DMA_UNROLL, issue, 0)

    @pl.when(i == 0)
    def _():
        gather(0, 0)

    @pl.when(i + 1 < steps)
    def _():
        gather(i + 1, (i + 1) % 2)

    slot = i % 2
    pltpu.make_async_copy(y_hbm.at[pl.ds(0, tc), :], ybuf.at[slot], sems.at[slot]).wait()
    o_ref[...] = _layer_norm(alpha * h_ref[...] + ybuf[slot], vec_ref[0:1, :], vec_ref[1:2, :])


def _moe_combine(alpha, dest, h2aug, vecs, y):
    n = h2aug.shape[0]
    tc = min(512, n)
    grid_spec = pltpu.PrefetchScalarGridSpec(
        num_scalar_prefetch=1,
        grid=(n // tc,),
        in_specs=[pl.BlockSpec((tc, D_MODEL), lambda i, *_: (i, 0)),
                  pl.BlockSpec((SUBLANES, D_MODEL), lambda i, *_: (0, 0)),
                  pl.BlockSpec(memory_space=pl.ANY)],
        out_specs=pl.BlockSpec((tc, D_MODEL), lambda i, *_: (i, 0)),
        scratch_shapes=[pltpu.VMEM((2, tc, D_MODEL), F32), pltpu.SemaphoreType.DMA((2,))],
    )
    return pl.pallas_call(
        functools.partial(_combine_kernel, alpha),
        grid_spec=grid_spec,
        out_shape=jax.ShapeDtypeStruct((n, D_MODEL), F32),
        compiler_params=_params("arbitrary"),
        name="moe_combine",
    )(dest, h2aug, vecs, y)


def _pad_rows(vectors):
    rows = [v.astype(F32)[None, :] for v in vectors]
    rows.append(jnp.zeros((SUBLANES - len(rows), vectors[0].shape[0]), F32))
    return jnp.concatenate(rows, axis=0)


def _layer(alpha, h, mem2, pos2, batch, seq, mem_len, w_in, attn_sinks, conv_w, conv_b, b_igate, b_fgate,
           ml_norm_g, w_att_branch, w_ml_branch, w_mix_out, ln1_g, ln1_b, w_xq, w_xkv, w_xo, ln2_g, ln2_b,
           w_router_group, b_router_group, w_router_expert, b_router_expert, w_gate, w_up, w_down,
           ln3_g, ln3_b):
    n = h.shape[0]
    q, kv, mq, mk, mv, gates = _in_proj(h, pos2, w_in, conv_w, conv_b, b_igate, b_fgate, batch, seq)
    att = _swa(q, kv, attn_sinks, batch, seq)
    hm = _mlstm(mq, mk, mv, gates, batch, seq)
    kx, vx = _mem_kv(mem2, w_xkv, batch, mem_len)

    o_mo = ATT_Q_W + 2 * ATT_KV_W + 2 * ML_QK_W + ML_V_W
    o_ga = o_mo + ML_V_W + 2 * ML_HEADS
    w_gates = jnp.concatenate([w_in[:, o_mo:o_mo + ML_V_W], w_in[:, o_ga:]], axis=1).astype(BF16)
    pad = LANES - N_GROUPS - N_EXPERTS
    w_router = jnp.pad(jnp.concatenate([w_router_group, w_router_expert], axis=1).astype(F32), ((0, 0), (0, pad)))
    b_router = jnp.pad(jnp.concatenate([b_router_group, b_router_expert]).astype(F32), (0, pad))[None, :]
    vecs = _pad_rows([ml_norm_g, ln1_g, ln1_b, ln2_g, ln2_b])
    h2aug, counts = _post(alpha, h, att, hm, kx, vx, w_gates, w_att_branch.astype(BF16),
                          w_ml_branch.astype(BF16), w_mix_out.astype(BF16), w_xq.astype(BF16),
                          w_xo.astype(BF16), w_router, b_router, vecs, batch, seq, mem_len)

    counts = counts[0, :N_GROUPS].astype(I32)
    padded = ((counts + GROUP_BLOCK - 1) // GROUP_BLOCK) * GROUP_BLOCK
    pad_end = jnp.cumsum(padded).astype(I32)
    pad_start = pad_end - padded
    n_blocks = n // GROUP_BLOCK + N_GROUPS
    block_start = jnp.arange(n_blocks, dtype=I32) * GROUP_BLOCK
    block_group = jnp.sum((pad_end[None, :] <= block_start[:, None]).astype(I32), axis=1)
    block_group = jnp.minimum(block_group, N_GROUPS - 1)
    n_used = pad_end[-1:] // GROUP_BLOCK
    group = h2aug[:, D_MODEL + REC_GROUP_LANE].astype(I32)
    rank = h2aug[:, D_MODEL + REC_RANK_LANE].astype(I32)
    dest = pad_start[group] + rank

    xs = _moe_scatter(dest, pad_end, h2aug, n_blocks * GROUP_BLOCK)
    e = EXPERTS_PER_GROUP
    w_gu = jnp.concatenate([w_gate, w_up], axis=2).astype(BF16).reshape(N_GROUPS, e, D_MODEL, 2 * D_EXPERT)
    w_dn = w_down.astype(BF16).reshape(N_GROUPS, e * D_EXPERT, D_MODEL)
    y = _moe_groups(block_group, n_used, xs, w_gu, w_dn)
    return _moe_combine(alpha, dest, h2aug, _pad_rows([ln3_g, ln3_b]), y)


def kernel(x, mem, positions, w_in, attn_sinks, conv_w, conv_b, b_igate, b_fgate, ml_norm_g, w_att_branch, w_ml_branch, w_mix_out, ln1_g, ln1_b, w_xq, w_xkv, w_xo, ln2_g, ln2_b, w_router_group, b_router_group, w_router_expert, b_router_expert, w_gate, w_up, w_down, ln3_g, ln3_b):
    batch, seq, d = x.shape
    mem_len = mem.shape[1]
    depth = w_in.shape[0]
    alpha = (2 * depth) ** 0.25
    h = x.reshape(batch * seq, d)
    mem2 = mem.reshape(batch * mem_len, d)
    pos2 = positions.reshape(batch * seq, 1)
    stacked = (w_in, attn_sinks, conv_w, conv_b, b_igate, b_fgate, ml_norm_g, w_att_branch, w_ml_branch,
               w_mix_out, ln1_g, ln1_b, w_xq, w_xkv, w_xo, ln2_g, ln2_b, w_router_group, b_router_group,
               w_router_expert, b_router_expert, w_gate, w_up, w_down, ln3_g, ln3_b)
    for l in range(depth):
        h = _layer(alpha, h, mem2, pos2, batch, seq, mem_len, *(w[l] for w in stacked))
    return h.reshape(batch, seq, d)
```

```python
import functools

import jax
import jax.numpy as jnp
from jax import lax
from jax.experimental import pallas as pl
from jax.experimental.pallas import tpu as pltpu

F32 = jnp.float32
BF16 = jnp.bfloat16
I32 = jnp.int32

D_MODEL = 1024
ATT_HEADS = 16
ATT_KV_HEADS = 2
ATT_HEAD_DIM = 64
ATT_BLOCK = 128
ROPE_THETA = 10000.0
ML_HEADS = 4
ML_DQK = 128
ML_DV = 256
ML_CHUNK = 128
CONV_WIDTH = 4
X_HEADS = 4
X_HEAD_DIM = D_MODEL // X_HEADS
N_GROUPS = 8
EXPERTS_PER_GROUP = 8
N_EXPERTS = N_GROUPS * EXPERTS_PER_GROUP
TOP_K = 2
D_EXPERT = 256
MOE_BLOCK = 128
LN_EPS = 1e-5
RMS_EPS = 1e-6

ATT_Q_W = ATT_HEADS * ATT_HEAD_DIM
ATT_KV_W = ATT_KV_HEADS * ATT_HEAD_DIM
ML_QK_W = ML_HEADS * ML_DQK
ML_V_W = ML_HEADS * ML_DV

LANES = 128
SUBLANES = 8
VMEM_LIMIT_BYTES = 56 * 1024 * 1024

ROUTER_EXPERT_LANE0 = N_GROUPS
REC_GROUP_LANE = 0
REC_RANK_LANE = 1
REC_WEIGHT_LANE0 = 8
ROW_W = D_MODEL + LANES
GROUP_BLOCK = 512
LOG2_E = 1.4426950408889634


def _params(*semantics):
    return pltpu.CompilerParams(dimension_semantics=semantics, vmem_limit_bytes=VMEM_LIMIT_BYTES)


def _const_spec(shape):
    zeros = (0,) * len(shape)
    return pl.BlockSpec(shape, lambda *_: zeros, pipeline_mode=pl.Buffered(1))


def _layer_norm(z, g, b):
    mu = jnp.mean(z, axis=-1, keepdims=True)
    zc = z - mu
    var = jnp.mean(zc * zc, axis=-1, keepdims=True)
    return zc * lax.rsqrt(var + LN_EPS) * g + b


def _inproj_kernel(sink_ref, x_ref, pos_ref, inv_ref, wrope_ref, wv_ref, wmqk_ref, wmv_ref, wg_ref,
                   convw_ref, convb_ref, gbias_ref,
                   att_out, mq_out, mk_out, mv_out, g_out, prev_ref, kvprev_ref):
    t = x_ref.shape[0]
    first = pl.program_id(1) == 0
    xb = x_ref[...].astype(BF16)

    half = ATT_HEAD_DIM // 2
    ang = pos_ref[...].astype(F32) * inv_ref[0:1, :]
    sin_cos = jnp.sin(ang + inv_ref[1:2, :])
    lane = lax.broadcasted_iota(I32, (t, LANES), 1)
    first_half = (lane % ATT_HEAD_DIM) < half
    cos = jnp.where(first_half, pltpu.roll(sin_cos, LANES - half, 1), sin_cos)
    sin_signed = jnp.where(first_half, -sin_cos, pltpu.roll(sin_cos, half, 1))

    def rope(z):
        partner = jnp.where(first_half, pltpu.roll(z, LANES - ATT_HEAD_DIM // 2, 1),
                            pltpu.roll(z, ATT_HEAD_DIM // 2, 1))
        return z * cos + partner * sin_signed

    qk = jnp.dot(xb, wrope_ref[...], preferred_element_type=F32)
    q_scale = ATT_HEAD_DIM ** -0.5 * LOG2_E
    q_groups = [(rope(qk[:, g * LANES:(g + 1) * LANES]) * q_scale).astype(BF16) for g in range(ATT_Q_W // LANES)]
    k_groups = [rope(qk[:, ATT_Q_W + g * LANES:ATT_Q_W + (g + 1) * LANES]).astype(BF16)
                for g in range(ATT_KV_HEADS)]
    v_all = jnp.dot(xb, wv_ref[...], preferred_element_type=F32).astype(BF16)
    v_groups = [v_all[:, g * LANES:(g + 1) * LANES] for g in range(ATT_KV_HEADS)]

    nk = 2 * ATT_BLOCK
    kc = lax.broadcasted_iota(I32, (nk, ATT_BLOCK), 0)
    qi = lax.broadcasted_iota(I32, (nk, ATT_BLOCK), 1)
    band = (kc > qi) & (kc <= qi + ATT_BLOCK)
    lo = lax.broadcasted_iota(I32, (nk, LANES), 1) < ATT_HEAD_DIM
    dim_lo = lax.broadcasted_iota(I32, (LANES, ATT_BLOCK), 0) < ATT_HEAD_DIM
    pairs = ATT_HEADS // ATT_KV_HEADS // 2

    @pl.when(first)
    def _():
        kvprev_ref[...] = jnp.zeros_like(kvprev_ref)

    for j in range(t // ATT_BLOCK):
        cur = slice(j * ATT_BLOCK, (j + 1) * ATT_BLOCK)
        valid = band & (kc >= jnp.where(first, ATT_BLOCK, 0)) if j == 0 else band
        for kvh in range(ATT_KV_HEADS):
            if j == 0:
                k_prev = kvprev_ref[:, kvh * LANES:(kvh + 1) * LANES]
                v_prev = kvprev_ref[:, (ATT_KV_HEADS + kvh) * LANES:(ATT_KV_HEADS + kvh + 1) * LANES]
            else:
                prev = slice((j - 1) * ATT_BLOCK, j * ATT_BLOCK)
                k_prev, v_prev = k_groups[kvh][prev], v_groups[kvh][prev]
            kk = jnp.concatenate([k_prev, k_groups[kvh][cur]], axis=0)
            vv = jnp.concatenate([v_prev, v_groups[kvh][cur]], axis=0).astype(F32)
            zero = jnp.zeros_like(kk)
            k2 = jnp.concatenate([jnp.where(lo, kk, zero), jnp.where(lo, zero, kk)], axis=0)
            v2t = jnp.concatenate([jnp.where(lo, vv, 0.0), jnp.where(lo, 0.0, vv)], axis=0).T.astype(BF16)
            for pp in range(pairs):
                p = kvh * pairs + pp
                qp = q_groups[p][cur]
                st = lax.dot_general(k2, qp, (((1,), (1,)), ((), ())), preferred_element_type=F32)
                e_heads, recip = [], []
                for hh in range(2):
                    sink = sink_ref[2 * p + hh]
                    s_h = jnp.where(valid, st[hh * nk:(hh + 1) * nk], -jnp.inf)
                    m = jnp.maximum(jnp.max(s_h, axis=0, keepdims=True), sink)
                    e = jnp.exp2(s_h - m)
                    den = jnp.sum(e, axis=0, keepdims=True) + jnp.exp2(sink - m)
                    e_heads.append(e.astype(BF16))
                    recip.append(1.0 / den)
                out_t = jnp.dot(v2t, jnp.concatenate(e_heads, axis=0), preferred_element_type=F32)
                out_t = out_t * jnp.where(dim_lo, recip[0], recip[1])
                att_out[cur, p * LANES:(p + 1) * LANES] = out_t.T.astype(BF16)
    last = slice(t - ATT_BLOCK, t)
    kvprev_ref[...] = jnp.concatenate([g[last] for g in k_groups] + [g[last] for g in v_groups], axis=1)

    @pl.when(pl.program_id(1) == 0)
    def _():
        prev_ref[...] = jnp.zeros_like(prev_ref)

    pre = jnp.dot(xb, wmqk_ref[...], preferred_element_type=F32)
    prev8 = prev_ref[...]
    row8 = lax.broadcasted_iota(I32, (SUBLANES, 2 * ML_QK_W), 0)
    w_now = convw_ref[CONV_WIDTH - 1:CONV_WIDTH, :]
    acc = pre * w_now + convb_ref[...]
    top = pre[0:SUBLANES] * w_now + convb_ref[...]
    for d in range(1, CONV_WIDTH):
        w_d = convw_ref[CONV_WIDTH - 1 - d:CONV_WIDTH - d, :]
        shifted = pltpu.roll(pre, d, 0)
        acc = acc + shifted * w_d
        top = top + jnp.where(row8 < d, pltpu.roll(prev8, d, 0), shifted[0:SUBLANES]) * w_d
    prev_ref[...] = pre[t - SUBLANES:t]
    conv = jnp.concatenate([top, acc[SUBLANES:]], axis=0)
    act = conv * jax.nn.sigmoid(conv)
    mq_out[...] = act[:, :ML_QK_W].astype(BF16)
    mk_out[...] = (act[:, ML_QK_W:] * (ML_DQK ** -0.5)).astype(BF16)

    mv_out[...] = jnp.dot(xb, wmv_ref[...], preferred_element_type=F32).astype(BF16)

    gates = jnp.dot(xb, wg_ref[...], preferred_element_type=F32) + gbias_ref[...]
    g_out[...] = jnp.where(lane < ML_HEADS, gates, jax.nn.log_sigmoid(gates))


def _in_proj(x2, pos2, w_in, sinks, conv_w, conv_b, b_igate, b_fgate, batch, seq):
    n = x2.shape[0]
    t = min(512, seq)
    s_tiles = seq // t

    def dup(w):
        h = ATT_HEAD_DIM
        return jnp.concatenate([w[:, :h], w[:, :h], w[:, h:], w[:, h:]], axis=1)

    o_k = ATT_Q_W
    o_v = o_k + ATT_KV_W
    o_mq = o_v + ATT_KV_W
    o_mv = o_mq + 2 * ML_QK_W
    o_mo = o_mv + ML_V_W
    o_mi = o_mo + ML_V_W
    w_rope = jnp.concatenate([w_in[:, :o_k], dup(w_in[:, o_k:o_v])], axis=1).astype(BF16)
    w_v = dup(w_in[:, o_v:o_mq]).astype(BF16)
    w_mqk = w_in[:, o_mq:o_mv].astype(BF16)
    w_mv = w_in[:, o_mv:o_mo].astype(BF16)
    w_g = jnp.pad(w_in[:, o_mi:o_mi + 2 * ML_HEADS], ((0, 0), (0, LANES - 2 * ML_HEADS))).astype(BF16)
    gbias = jnp.pad(jnp.concatenate([b_igate, b_fgate]).astype(F32), (0, LANES - 2 * ML_HEADS))[None, :]
    half = ATT_HEAD_DIM // 2
    inv = ROPE_THETA ** (-jnp.arange(half, dtype=F32) / half)
    quarter_turn = jnp.where((jnp.arange(LANES) % ATT_HEAD_DIM) < half, 0.0, jnp.pi / 2).astype(F32)
    inv = jnp.stack([jnp.tile(inv, LANES // half), quarter_turn])

    row = lambda w: pl.BlockSpec((t, w), lambda b, s, *_: (b * s_tiles + s, 0))
    grid_spec = pltpu.PrefetchScalarGridSpec(
        num_scalar_prefetch=1,
        grid=(batch, s_tiles),
        in_specs=[row(D_MODEL), row(1), _const_spec((2, LANES)),
                  _const_spec(w_rope.shape), _const_spec(w_v.shape), _const_spec(w_mqk.shape),
                  _const_spec(w_mv.shape), _const_spec(w_g.shape),
                  _const_spec((CONV_WIDTH, 2 * ML_QK_W)), _const_spec((1, 2 * ML_QK_W)),
                  _const_spec((1, LANES))],
        out_specs=[row(ATT_Q_W), row(ML_QK_W), row(ML_QK_W), row(ML_V_W), row(LANES)],
        scratch_shapes=[pltpu.VMEM((SUBLANES, 2 * ML_QK_W), F32),
                        pltpu.VMEM((ATT_BLOCK, 4 * ATT_KV_W), BF16)],
    )
    return pl.pallas_call(
        _inproj_kernel,
        grid_spec=grid_spec,
        out_shape=[jax.ShapeDtypeStruct((n, ATT_Q_W), BF16),
                   jax.ShapeDtypeStruct((n, ML_QK_W), BF16),
                   jax.ShapeDtypeStruct((n, ML_QK_W), BF16),
                   jax.ShapeDtypeStruct((n, ML_V_W), BF16),
                   jax.ShapeDtypeStruct((n, LANES), F32)],
        compiler_params=_params("arbitrary", "arbitrary"),
        name="in_proj",
    )(sinks.astype(F32) * LOG2_E, x2, pos2, inv, w_rope, w_v, w_mqk, w_mv, w_g, conv_w.astype(F32),
      conv_b.astype(F32)[None, :], gbias)


def _mlstm_reset(c_ref, n_ref, m_ref):
    c_ref[...] = jnp.zeros_like(c_ref)
    n_ref[...] = jnp.zeros_like(n_ref)
    m_ref[...] = jnp.zeros_like(m_ref)


def _mlstm_chunk(rows, q_ref, k_ref, v_ref, g_ref, o_ref, c_ref, n_ref, m_ref):
    nb = q_ref.shape[0]
    L = ML_CHUNK
    causal = lax.broadcasted_iota(I32, (L, L), 1) <= lax.broadcasted_iota(I32, (L, L), 0)
    rowi = lax.broadcasted_iota(I32, (L, LANES), 0)

    def scan_rows(x, op, identity):
        d = 1
        while d < L:
            x = op(x, jnp.where(rowi >= d, pltpu.roll(x, d, 0), identity))
            d *= 2
        return x

    def chunk():
        heads = [(bb, h) for bb in range(nb) for h in range(ML_HEADS)]

        gate = []
        for bb in range(nb):
            g = g_ref[bb, rows, :]
            b = pltpu.roll(scan_rows(g, jnp.add, 0.0), LANES - ML_HEADS, 1)
            a = g - b
            a_max = scan_rows(a, jnp.maximum, -jnp.inf)
            m_prev = m_ref[bb, 0:1, :]
            m_t = b + jnp.maximum(m_prev, a_max)
            u = b - m_t
            b_end = b[L - 1:L, :]
            m_new = b_end + jnp.maximum(m_prev, a_max[L - 1:L, :])
            m_ref[bb, 0:1, :] = m_new
            gate.append(dict(u=u, w_inter=jnp.exp(u + m_prev), floor=jnp.exp(-m_t),
                             w_state=jnp.exp(b_end + m_prev - m_new), w_k=jnp.exp(b_end + a - m_new),
                             a_t=a.T))

        nt = (((1,), (1,)), ((), ()))
        for bb, h in heads:
            gt = gate[bb]
            idx = bb * ML_HEADS + h
            col = slice(h, h + 1)
            qh = q_ref[bb, rows, h * ML_DQK:(h + 1) * ML_DQK]
            kh = k_ref[bb, rows, h * ML_DQK:(h + 1) * ML_DQK]
            vh = v_ref[bb, rows, h * ML_DV:(h + 1) * ML_DV]
            c_prev = c_ref[idx]
            n_prev = n_ref[idx, 0:1, :]

            w_intra = jnp.exp(jnp.where(causal, gt["u"][:, col] + gt["a_t"][col, :], -jnp.inf))
            s = lax.dot_general(qh, kh, nt, preferred_element_type=F32) * w_intra
            w_inter = gt["w_inter"][:, col]
            num = (w_inter * jnp.dot(qh, c_prev.astype(BF16), preferred_element_type=F32)
                   + jnp.dot(s.astype(BF16), vh, preferred_element_type=F32))
            den = (w_inter * jnp.sum(qh.astype(F32) * n_prev, axis=1, keepdims=True)
                   + jnp.sum(s, axis=1, keepdims=True))
            hval = num / jnp.maximum(jnp.abs(den), gt["floor"][:, col])
            hval = hval * lax.rsqrt(jnp.mean(hval * hval, axis=1, keepdims=True) + RMS_EPS)
            o_ref[bb, rows, h * ML_DV:(h + 1) * ML_DV] = hval.astype(BF16)

            w_state = gt["w_state"][:, col]
            kw = kh.astype(F32) * gt["w_k"][:, col]
            c_ref[idx] = w_state * c_prev + jnp.dot(kw.T.astype(BF16), vh, preferred_element_type=F32)
            n_ref[idx, 0:1, :] = w_state * n_prev + jnp.sum(kw, axis=0, keepdims=True)

    chunk()


def _memkv_kernel(mem_ref, w_ref, k_out, v_out):
    kv = jnp.dot(mem_ref[...].astype(BF16), w_ref[...], preferred_element_type=F32)
    k_out[...] = kv[:, :D_MODEL].astype(BF16)
    v_out[...] = kv[:, D_MODEL:].astype(BF16)


def _mem_kv(mem2, w_xkv, batch, mem_len):
    blk = pl.BlockSpec((mem_len, D_MODEL), lambda b: (b, 0))
    return pl.pallas_call(
        _memkv_kernel,
        grid=(batch,),
        in_specs=[blk, _const_spec((D_MODEL, 2 * D_MODEL))],
        out_specs=[blk, blk],
        out_shape=[jax.ShapeDtypeStruct((batch * mem_len, D_MODEL), BF16)] * 2,
        compiler_params=_params("arbitrary"),
        name="mem_kv",
    )(mem2, w_xkv.astype(BF16))


def _post_kernel(alpha, x_ref, att_ref, mq_ref, mk_ref, mv_ref, g_ref, kx_ref, vx_ref, wgate_ref, watt_ref,
                 wml_ref, wmix_ref, wxq_ref, wxo_ref, wr_hi_ref, wr_lo_ref, vec_ref, rbias_ref,
                 h_out, count_out, base_ref, c_ref, n_ref, m_ref, hm_ref):
    t = x_ref.shape[0]
    first = (pl.program_id(0) == 0) & (pl.program_id(1) == 0)

    @pl.when(first)
    def _():
        base_ref[...] = jnp.zeros_like(base_ref)

    @pl.when(pl.program_id(1) == 0)
    def _():
        _mlstm_reset(c_ref, n_ref, m_ref)

    for c in range(t // ML_CHUNK):
        _mlstm_chunk(slice(c * ML_CHUNK, (c + 1) * ML_CHUNK), mq_ref, mk_ref, mv_ref, g_ref, hm_ref,
                     c_ref, n_ref, m_ref)

    x = x_ref[...]
    xb = x.astype(BF16)
    norm_g, ln1_g, ln1_b, ln2_g, ln2_b = (vec_ref[i:i + 1, :] for i in range(5))

    mo = jnp.dot(xb, wgate_ref[:, 0:D_MODEL], preferred_element_type=F32)
    hm = (hm_ref[0].astype(F32) * norm_g * jax.nn.sigmoid(mo)).astype(BF16)
    m_out = jnp.dot(hm, wml_ref[...], preferred_element_type=F32)
    gm = jnp.dot(xb, wgate_ref[:, 2 * D_MODEL:3 * D_MODEL], preferred_element_type=F32)
    y = jax.nn.sigmoid(gm) * m_out
    a_out = jnp.dot(att_ref[...], watt_ref[...], preferred_element_type=F32)
    ga = jnp.dot(xb, wgate_ref[:, D_MODEL:2 * D_MODEL], preferred_element_type=F32)
    y = y + jax.nn.sigmoid(ga) * a_out
    mix = jnp.dot(y.astype(BF16), wmix_ref[...], preferred_element_type=F32)
    h1 = _layer_norm(alpha * x + mix, ln1_g, ln1_b)

    q = (jnp.dot(h1.astype(BF16), wxq_ref[...], preferred_element_type=F32) * (X_HEAD_DIM ** -0.5)).astype(BF16)
    heads = []
    for h in range(X_HEADS):
        sl = slice(h * X_HEAD_DIM, (h + 1) * X_HEAD_DIM)
        sc = lax.dot_general(q[:, sl], kx_ref[:, sl], (((1,), (1,)), ((), ())), preferred_element_type=F32)
        e = jnp.exp(sc - jnp.max(sc, axis=1, keepdims=True))
        probs = (e * (1.0 / jnp.sum(e, axis=1, keepdims=True))).astype(BF16)
        heads.append(jnp.dot(probs, vx_ref[:, sl], preferred_element_type=F32).astype(BF16))
    xa = jnp.dot(jnp.concatenate(heads, axis=1), wxo_ref[...], preferred_element_type=F32)
    h2 = _layer_norm(alpha * h1 + xa, ln2_g, ln2_b)
    h_out[:, :D_MODEL] = h2

    h_hi = h2.astype(BF16)
    h_lo = (h2 - h_hi.astype(F32)).astype(BF16)
    logits = (jnp.dot(h_hi, wr_hi_ref[...], preferred_element_type=F32)
              + jnp.dot(h_lo, wr_hi_ref[...], preferred_element_type=F32)
              + jnp.dot(h_hi, wr_lo_ref[...], preferred_element_type=F32)) + rbias_ref[...]

    lane = lax.broadcasted_iota(I32, (t, LANES), 1)
    neg = -jnp.inf
    is_group = lane < N_GROUPS
    gl = jnp.where(is_group, logits, neg)
    g_max = jnp.max(gl, axis=1, keepdims=True)
    lane_f = lane.astype(F32)
    no_lane = float(LANES)
    g_sel = jnp.min(jnp.where(gl == g_max, lane_f, no_lane), axis=1, keepdims=True)
    p_group = 1.0 / jnp.sum(jnp.where(is_group, jnp.exp(logits - g_max), 0.0), axis=1, keepdims=True)
    lane_group = ((lane - ROUTER_EXPERT_LANE0) >> 3).astype(F32)
    in_group = (lane >= ROUTER_EXPERT_LANE0) & (lane_group == g_sel)
    el = jnp.where(in_group, logits, neg)
    v1 = jnp.max(el, axis=1, keepdims=True)
    i1 = jnp.min(jnp.where(el == v1, lane_f, no_lane), axis=1, keepdims=True)
    el2 = jnp.where(lane_f == i1, neg, el)
    v2 = jnp.max(el2, axis=1, keepdims=True)
    i2 = jnp.min(jnp.where(el2 == v2, lane_f, no_lane), axis=1, keepdims=True)
    e2 = jnp.exp(v2 - v1)
    w1 = p_group * (1.0 / (1.0 + e2))
    w2 = p_group * (e2 / (1.0 + e2))

    onehot = jnp.where(lane_f == g_sel, 1.0, 0.0)
    tri = jnp.where(lax.broadcasted_iota(I32, (t, t), 1) < lax.broadcasted_iota(I32, (t, t), 0), 1.0, 0.0).astype(BF16)
    before = jnp.dot(tri, onehot.astype(BF16), preferred_element_type=F32) + base_ref[0:1, :]
    rank = jnp.sum(jnp.where(lane_f == g_sel, before, 0.0), axis=1, keepdims=True)
    total = base_ref[0:1, :] + jnp.sum(onehot, axis=0, keepdims=True)
    base_ref[...] = jnp.broadcast_to(total, base_ref.shape)
    count_out[...] = jnp.broadcast_to(total, count_out.shape)

    logit_lane = lane_f + (ROUTER_EXPERT_LANE0 - REC_WEIGHT_LANE0) + EXPERTS_PER_GROUP * g_sel
    rec = jnp.where(logit_lane == i1, w1, jnp.where(logit_lane == i2, w2, 0.0))
    rec = jnp.where((lane >= REC_WEIGHT_LANE0) & (lane < REC_WEIGHT_LANE0 + EXPERTS_PER_GROUP), rec, 0.0)
    rec = jnp.where(lane == REC_GROUP_LANE, g_sel, rec)
    rec = jnp.where(lane == REC_RANK_LANE, rank, rec)
    h_out[:, D_MODEL:] = rec


def _post(alpha, x2, att, mq, mk, mv, gates, kx, vx, w_gates, w_att, w_ml, w_mix, w_xq, w_xo, w_router, b_router,
          vecs, batch, seq, mem_len):
    n = x2.shape[0]
    t = min(512, seq)
    s_tiles = seq // t
    row = lambda w: pl.BlockSpec((t, w), lambda b, s: (b * s_tiles + s, 0))
    memblk = pl.BlockSpec((mem_len, D_MODEL), lambda b, s: (b, 0))
    seq_blk = lambda w: pl.BlockSpec((1, t, w), lambda b, s: (b, s, 0))
    seqs = lambda a: a.reshape(batch, seq, a.shape[-1])
    w_hi = w_router.astype(BF16)
    w_lo = (w_router - w_hi.astype(F32)).astype(BF16)
    sq = _const_spec((D_MODEL, D_MODEL))
    return pl.pallas_call(
        functools.partial(_post_kernel, alpha),
        grid=(batch, s_tiles),
        in_specs=[row(D_MODEL), row(ATT_Q_W), seq_blk(ML_QK_W), seq_blk(ML_QK_W), seq_blk(ML_V_W), seq_blk(LANES),
                  memblk, memblk,
                  _const_spec((D_MODEL, 3 * D_MODEL)), sq, sq, sq, sq, sq,
                  _const_spec((D_MODEL, LANES)), _const_spec((D_MODEL, LANES)),
                  _const_spec((SUBLANES, D_MODEL)), _const_spec((1, LANES))],
        out_specs=[row(ROW_W), _const_spec((SUBLANES, LANES))],
        out_shape=[jax.ShapeDtypeStruct((n, ROW_W), F32),
                   jax.ShapeDtypeStruct((SUBLANES, LANES), F32)],
        scratch_shapes=[pltpu.VMEM((SUBLANES, LANES), F32),
                        pltpu.VMEM((ML_HEADS, ML_DQK, ML_DV), F32),
                        pltpu.VMEM((ML_HEADS, SUBLANES, ML_DQK), F32),
                        pltpu.VMEM((1, SUBLANES, LANES), F32),
                        pltpu.VMEM((1, t, ML_V_W), BF16)],
        compiler_params=_params("arbitrary", "arbitrary"),
        name="post",
    )(x2, att, seqs(mq), seqs(mk), seqs(mv), seqs(gates), kx, vx, w_gates, w_att, w_ml, w_mix, w_xq, w_xo, w_hi, w_lo, vecs, b_router)


DMA_UNROLL = 8


def _scatter_kernel(dest_ref, padend_ref, h_ref, xs_out, zero_buf, sem, zero_sem):
    i = pl.program_id(0)
    ts = h_ref.shape[0]

    @pl.when(i == 0)
    def _():
        zero_buf[...] = jnp.zeros_like(zero_buf)

        def clear(start):
            start = pl.multiple_of(start, GROUP_BLOCK)
            return pltpu.make_async_copy(zero_buf, xs_out.at[pl.ds(start, GROUP_BLOCK), :], zero_sem)

        def nonempty(g):
            return padend_ref[g] > (padend_ref[g - 1] if g else 0)

        used = padend_ref[N_GROUPS - 1]

        def unused(b):
            return used + b * GROUP_BLOCK < xs_out.shape[0]

        for phase in ("start", "wait"):
            for g in range(N_GROUPS):
                @pl.when(nonempty(g))
                def _():
                    getattr(clear(padend_ref[g] - GROUP_BLOCK), phase)()

                @pl.when(unused(g))
                def _():
                    getattr(clear(used + g * GROUP_BLOCK), phase)()

    def issue(j, carry):
        for u in range(DMA_UNROLL):
            t = j * DMA_UNROLL + u
            pltpu.make_async_copy(h_ref.at[pl.ds(t, 1), :], xs_out.at[pl.ds(dest_ref[i * ts + t], 1), :], sem).start()
        return carry

    lax.fori_loop(0, ts // DMA_UNROLL, issue, 0)

    pltpu.make_async_copy(h_ref, xs_out.at[pl.ds(0, ts), :], sem).wait()


def _moe_scatter(dest, pad_end, h2aug, rows):
    n = h2aug.shape[0]
    ts = min(1024, n)
    grid_spec = pltpu.PrefetchScalarGridSpec(
        num_scalar_prefetch=2,
        grid=(n // ts,),
        in_specs=[pl.BlockSpec((ts, ROW_W), lambda i, *_: (i, 0))],
        out_specs=pl.BlockSpec(memory_space=pl.ANY),
        scratch_shapes=[pltpu.VMEM((GROUP_BLOCK, ROW_W), F32),
                        pltpu.SemaphoreType.DMA(()), pltpu.SemaphoreType.DMA(())],
    )
    return pl.pallas_call(
        _scatter_kernel,
        grid_spec=grid_spec,
        out_shape=jax.ShapeDtypeStruct((rows, ROW_W), F32),
        compiler_params=_params("arbitrary"),
        name="moe_scatter",
    )(dest, pad_end, h2aug)


def _group_kernel(bgrp_ref, nused_ref, x_ref, wgu_ref, wd_ref, y_ref):
    del bgrp_ref
    i = pl.program_id(0)

    @pl.when(i < nused_ref[0])
    def _():
        xb = x_ref[:, :D_MODEL].astype(BF16)
        hidden = []
        for e in range(EXPERTS_PER_GROUP):
            gu = jnp.dot(xb, wgu_ref[0, e], preferred_element_type=F32)
            gate = gu[:, :D_EXPERT]
            lane = D_MODEL + REC_WEIGHT_LANE0 + e
            w = x_ref[:, lane:lane + 1]
            hw = jnp.where(w != 0.0, w * (gate * jax.nn.sigmoid(gate) * gu[:, D_EXPERT:]), 0.0)
            hidden.append(hw.astype(BF16))
        y_ref[...] = jnp.dot(jnp.concatenate(hidden, axis=1), wd_ref[0], preferred_element_type=F32)

    @pl.when(i >= nused_ref[0])
    def _():
        y_ref[...] = jnp.zeros_like(y_ref)


def _moe_groups(block_group, n_used, xs, w_gu, w_down):
    rows = xs.shape[0]
    e = EXPERTS_PER_GROUP
    grid_spec = pltpu.PrefetchScalarGridSpec(
        num_scalar_prefetch=2,
        grid=(rows // GROUP_BLOCK,),
        in_specs=[pl.BlockSpec((GROUP_BLOCK, ROW_W), lambda i, *_: (i, 0)),
                  pl.BlockSpec((1, e, D_MODEL, 2 * D_EXPERT), lambda i, bg, nu: (bg[i], 0, 0, 0)),
                  pl.BlockSpec((1, e * D_EXPERT, D_MODEL), lambda i, bg, nu: (bg[i], 0, 0))],
        out_specs=pl.BlockSpec((GROUP_BLOCK, D_MODEL), lambda i, *_: (i, 0)),
    )
    return pl.pallas_call(
        _group_kernel,
        grid_spec=grid_spec,
        out_shape=jax.ShapeDtypeStruct((rows, D_MODEL), F32),
        compiler_params=_params("arbitrary"),
        name="moe_groups",
    )(block_group, n_used, xs, w_gu, w_down)


def _combine_kernel(alpha, dest_ref, h_ref, vec_ref, y_hbm, o_ref, ybuf, sems):
    tc = h_ref.shape[0]
    i = pl.program_id(0)
    steps = pl.num_programs(0)

    def gather(tile, slot):
        def issue(j, carry):
            for u in range(DMA_UNROLL):
                t = j * DMA_UNROLL + u
                pltpu.make_async_copy(y_hbm.at[pl.ds(dest_ref[tile * tc + t], 1), :],
                                      ybuf.at[slot, pl.ds(t, 1), :], sems.at[slot]).start()
            return carry
        lax.fori_loop(0, tc // DMA_UNROLL, issue, 0)

    @pl.when(i == 0)
    def _():
        gather(0, 0)

    @pl.when(i + 1 < steps)
    def _():
        gather(i + 1, (i + 1) % 2)

    slot = i % 2
    pltpu.make_async_copy(y_hbm.at[pl.ds(0, tc), :], ybuf.at[slot], sems.at[slot]).wait()
    o_ref[...] = _layer_norm(alpha * h_ref[...] + ybuf[slot], vec_ref[0:1, :], vec_ref[1:2, :])


def _moe_combine(alpha, dest, h2aug, vecs, y):
    n = h2aug.shape[0]
    tc = min(512, n)
    grid_spec = pltpu.PrefetchScalarGridSpec(
        num_scalar_prefetch=1,
        grid=(n // tc,),
        in_specs=[pl.BlockSpec((tc, D_MODEL), lambda i, *_: (i, 0)),
                  pl.BlockSpec((SUBLANES, D_MODEL), lambda i, *_: (0, 0)),
                  pl.BlockSpec(memory_space=pl.ANY)],
        out_specs=pl.BlockSpec((tc, D_MODEL), lambda i, *_: (i, 0)),
        scratch_shapes=[pltpu.VMEM((2, tc, D_MODEL), F32), pltpu.SemaphoreType.DMA((2,))],
    )
    return pl.pallas_call(
        functools.partial(_combine_kernel, alpha),
        grid_spec=grid_spec,
        out_shape=jax.ShapeDtypeStruct((n, D_MODEL), F32),
        compiler_params=_params("arbitrary"),
        name="moe_combine",
    )(dest, h2aug, vecs, y)


def _pad_rows(vectors):
    rows = [v.astype(F32)[None, :] for v in vectors]
    rows.append(jnp.zeros((SUBLANES - len(rows), vectors[0].shape[0]), F32))
    return jnp.concatenate(rows, axis=0)


def _layer(alpha, h, mem2, pos2, batch, seq, mem_len, w_in, attn_sinks, conv_w, conv_b, b_igate, b_fgate,
           ml_norm_g, w_att_branch, w_ml_branch, w_mix_out, ln1_g, ln1_b, w_xq, w_xkv, w_xo, ln2_g, ln2_b,
           w_router_group, b_router_group, w_router_expert, b_router_expert, w_gate, w_up, w_down,
           ln3_g, ln3_b):
    n = h.shape[0]
    att, mq, mk, mv, gates = _in_proj(h, pos2, w_in, attn_sinks, conv_w, conv_b, b_igate, b_fgate, batch, seq)
    kx, vx = _mem_kv(mem2, w_xkv, batch, mem_len)

    o_mo = ATT_Q_W + 2 * ATT_KV_W + 2 * ML_QK_W + ML_V_W
    o_ga = o_mo + ML_V_W + 2 * ML_HEADS
    w_gates = jnp.concatenate([w_in[:, o_mo:o_mo + ML_V_W], w_in[:, o_ga:]], axis=1).astype(BF16)
    pad = LANES - N_GROUPS - N_EXPERTS
    w_router = jnp.pad(jnp.concatenate([w_router_group, w_router_expert], axis=1).astype(F32), ((0, 0), (0, pad)))
    b_router = jnp.pad(jnp.concatenate([b_router_group, b_router_expert]).astype(F32), (0, pad))[None, :]
    vecs = _pad_rows([ml_norm_g, ln1_g, ln1_b, ln2_g, ln2_b])
    h2aug, counts = _post(alpha, h, att, mq, mk, mv, gates, kx, vx, w_gates, w_att_branch.astype(BF16),
                          w_ml_branch.astype(BF16), w_mix_out.astype(BF16), w_xq.astype(BF16),
                          w_xo.astype(BF16), w_router, b_router, vecs, batch, seq, mem_len)

    counts = counts[0, :N_GROUPS].astype(I32)
    padded = ((counts + GROUP_BLOCK - 1) // GROUP_BLOCK) * GROUP_BLOCK
    pad_end = jnp.cumsum(padded).astype(I32)
    pad_start = pad_end - padded
    n_blocks = n // GROUP_BLOCK + N_GROUPS
    block_start = jnp.arange(n_blocks, dtype=I32) * GROUP_BLOCK
    block_group = jnp.sum((pad_end[None, :] <= block_start[:, None]).astype(I32), axis=1)
    block_group = jnp.minimum(block_group, N_GROUPS - 1)
    n_used = pad_end[-1:] // GROUP_BLOCK
    group = h2aug[:, D_MODEL + REC_GROUP_LANE].astype(I32)
    rank = h2aug[:, D_MODEL + REC_RANK_LANE].astype(I32)
    dest = pad_start[group] + rank

    xs = _moe_scatter(dest, pad_end, h2aug, n_blocks * GROUP_BLOCK)
    e = EXPERTS_PER_GROUP
    w_gu = jnp.concatenate([w_gate, w_up], axis=2).astype(BF16).reshape(N_GROUPS, e, D_MODEL, 2 * D_EXPERT)
    w_dn = w_down.astype(BF16).reshape(N_GROUPS, e * D_EXPERT, D_MODEL)
    y = _moe_groups(block_group, n_used, xs, w_gu, w_dn)
    return _moe_combine(alpha, dest, h2aug, _pad_rows([ln3_g, ln3_b]), y)


def kernel(x, mem, positions, w_in, attn_sinks, conv_w, conv_b, b_igate, b_fgate, ml_norm_g, w_att_branch, w_ml_branch, w_mix_out, ln1_g, ln1_b, w_xq, w_xkv, w_xo, ln2_g, ln2_b, w_router_group, b_router_group, w_router_expert, b_router_expert, w_gate, w_up, w_down, ln3_g, ln3_b):
    batch, seq, d = x.shape
    mem_len = mem.shape[1]
    depth = w_in.shape[0]
    alpha = (2 * depth) ** 0.25
    h = x.reshape(batch * seq, d)
    mem2 = mem.reshape(batch * mem_len, d)
    pos2 = positions.reshape(batch * seq, 1)
    stacked = (w_in, attn_sinks, conv_w, conv_b, b_igate, b_fgate, ml_norm_g, w_att_branch, w_ml_branch,
               w_mix_out, ln1_g, ln1_b, w_xq, w_xkv, w_xo, ln2_g, ln2_b, w_router_group, b_router_group,
               w_router_expert, b_router_expert, w_gate, w_up, w_down, ln3_g, ln3_b)
    for l in range(depth):
        h = _layer(alpha, h, mem2, pos2, batch, seq, mem_len, *(w[l] for w in stacked))
    return h.reshape(batch, seq, d)
```

```python
import functools

import jax
import jax.numpy as jnp
from jax import lax
from jax.experimental import pallas as pl
from jax.experimental.pallas import tpu as pltpu

F32 = jnp.float32
BF16 = jnp.bfloat16
I32 = jnp.int32

D_MODEL = 1024
ATT_HEADS = 16
ATT_KV_HEADS = 2
ATT_HEAD_DIM = 64
ATT_BLOCK = 128
ROPE_THETA = 10000.0
ML_HEADS = 4
ML_DQK = 128
ML_DV = 256
ML_CHUNK = 128
CONV_WIDTH = 4
X_HEADS = 4
X_HEAD_DIM = D_MODEL // X_HEADS
N_GROUPS = 8
EXPERTS_PER_GROUP = 8
N_EXPERTS = N_GROUPS * EXPERTS_PER_GROUP
TOP_K = 2
D_EXPERT = 256
MOE_BLOCK = 128
LN_EPS = 1e-5
RMS_EPS = 1e-6

ATT_Q_W = ATT_HEADS * ATT_HEAD_DIM
ATT_KV_W = ATT_KV_HEADS * ATT_HEAD_DIM
ML_QK_W = ML_HEADS * ML_DQK
ML_V_W = ML_HEADS * ML_DV

LANES = 128
SUBLANES = 8
VMEM_LIMIT_BYTES = 56 * 1024 * 1024

ROUTER_EXPERT_LANE0 = N_GROUPS
REC_GROUP_LANE = 0
REC_RANK_LANE = 1
REC_WEIGHT_LANE0 = 8
ROW_W = D_MODEL + LANES
GROUP_BLOCK = 512
LOG2_E = 1.4426950408889634
POST_COLS = 512


def _params(*semantics):
    return pltpu.CompilerParams(dimension_semantics=semantics, vmem_limit_bytes=VMEM_LIMIT_BYTES)


def _const_spec(shape):
    zeros = (0,) * len(shape)
    return pl.BlockSpec(shape, lambda *_: zeros, pipeline_mode=pl.Buffered(1))


def _layer_norm(z, g, b):
    mu = jnp.mean(z, axis=-1, keepdims=True)
    zc = z - mu
    var = jnp.mean(zc * zc, axis=-1, keepdims=True)
    return zc * lax.rsqrt(var + LN_EPS) * g + b


def _inproj_kernel(sink_ref, x_ref, pos_ref, inv_ref, wrope_ref, wv_ref, wmqk_ref, wmv_ref, wg_ref,
                   convw_ref, convb_ref, gbias_ref,
                   att_out, mq_out, mk_out, mv_out, g_out, prev_ref, kvprev_ref):
    t = x_ref.shape[0]
    first = pl.program_id(1) == 0
    xb = x_ref[...].astype(BF16)

    half = ATT_HEAD_DIM // 2
    ang = pos_ref[...].astype(F32) * inv_ref[0:1, :]
    sin_cos = jnp.sin(ang + inv_ref[1:2, :])
    lane = lax.broadcasted_iota(I32, (t, LANES), 1)
    first_half = (lane % ATT_HEAD_DIM) < half
    cos = jnp.where(first_half, pltpu.roll(sin_cos, LANES - half, 1), sin_cos)
    sin_signed = jnp.where(first_half, -sin_cos, pltpu.roll(sin_cos, half, 1))

    def rope(z):
        partner = jnp.where(first_half, pltpu.roll(z, LANES - ATT_HEAD_DIM // 2, 1),
                            pltpu.roll(z, ATT_HEAD_DIM // 2, 1))
        return z * cos + partner * sin_signed

    qk = jnp.dot(xb, wrope_ref[...], preferred_element_type=F32)
    q_scale = ATT_HEAD_DIM ** -0.5 * LOG2_E
    q_groups = [(rope(qk[:, g * LANES:(g + 1) * LANES]) * q_scale).astype(BF16) for g in range(ATT_Q_W // LANES)]
    k_groups = [rope(qk[:, ATT_Q_W + g * LANES:ATT_Q_W + (g + 1) * LANES]).astype(BF16)
                for g in range(ATT_KV_HEADS)]
    v_all = jnp.dot(xb, wv_ref[...], preferred_element_type=F32).astype(BF16)
    v_groups = [v_all[:, g * LANES:(g + 1) * LANES] for g in range(ATT_KV_HEADS)]

    nk = 2 * ATT_BLOCK
    kc = lax.broadcasted_iota(I32, (nk, ATT_BLOCK), 0)
    qi = lax.broadcasted_iota(I32, (nk, ATT_BLOCK), 1)
    band = (kc > qi) & (kc <= qi + ATT_BLOCK)
    lo = lax.broadcasted_iota(I32, (nk, LANES), 1) < ATT_HEAD_DIM
    dim_lo = lax.broadcasted_iota(I32, (LANES, ATT_BLOCK), 0) < ATT_HEAD_DIM
    pairs = ATT_HEADS // ATT_KV_HEADS // 2

    @pl.when(first)
    def _():
        kvprev_ref[...] = jnp.zeros_like(kvprev_ref)

    for j in range(t // ATT_BLOCK):
        cur = slice(j * ATT_BLOCK, (j + 1) * ATT_BLOCK)
        valid = band & (kc >= jnp.where(first, ATT_BLOCK, 0)) if j == 0 else band
        for kvh in range(ATT_KV_HEADS):
            if j == 0:
                k_prev = kvprev_ref[:, kvh * LANES:(kvh + 1) * LANES]
                v_prev = kvprev_ref[:, (ATT_KV_HEADS + kvh) * LANES:(ATT_KV_HEADS + kvh + 1) * LANES]
            else:
                prev = slice((j - 1) * ATT_BLOCK, j * ATT_BLOCK)
                k_prev, v_prev = k_groups[kvh][prev], v_groups[kvh][prev]
            kk = jnp.concatenate([k_prev, k_groups[kvh][cur]], axis=0)
            vv = jnp.concatenate([v_prev, v_groups[kvh][cur]], axis=0).astype(F32)
            zero = jnp.zeros_like(kk)
            k2 = jnp.concatenate([jnp.where(lo, kk, zero), jnp.where(lo, zero, kk)], axis=0)
            v2t = jnp.concatenate([jnp.where(lo, vv, 0.0), jnp.where(lo, 0.0, vv)], axis=0).T.astype(BF16)
            for pp in range(pairs):
                p = kvh * pairs + pp
                qp = q_groups[p][cur]
                st = lax.dot_general(k2, qp, (((1,), (1,)), ((), ())), preferred_element_type=F32)
                e_heads, recip = [], []
                for hh in range(2):
                    sink = sink_ref[2 * p + hh]
                    s_h = jnp.where(valid, st[hh * nk:(hh + 1) * nk], -jnp.inf)
                    m = jnp.maximum(jnp.max(s_h, axis=0, keepdims=True), sink)
                    e = jnp.exp2(s_h - m)
                    den = jnp.sum(e, axis=0, keepdims=True) + jnp.exp2(sink - m)
                    e_heads.append(e.astype(BF16))
                    recip.append(1.0 / den)
                out_t = jnp.dot(v2t, jnp.concatenate(e_heads, axis=0), preferred_element_type=F32)
                out_t = out_t * jnp.where(dim_lo, recip[0], recip[1])
                att_out[cur, p * LANES:(p + 1) * LANES] = out_t.T.astype(BF16)
    last = slice(t - ATT_BLOCK, t)
    kvprev_ref[...] = jnp.concatenate([g[last] for g in k_groups] + [g[last] for g in v_groups], axis=1)

    @pl.when(pl.program_id(1) == 0)
    def _():
        prev_ref[...] = jnp.zeros_like(prev_ref)

    row8 = lax.broadcasted_iota(I32, (SUBLANES, ML_QK_W), 0)
    for out_ref, scale, cs in ((mq_out, 1.0, slice(0, ML_QK_W)), (mk_out, ML_DQK ** -0.5, slice(ML_QK_W, 2 * ML_QK_W))):
        pre = jnp.dot(xb, wmqk_ref[:, cs], preferred_element_type=F32)
        prev8 = prev_ref[:, cs]
        w_now = convw_ref[CONV_WIDTH - 1:CONV_WIDTH, cs]
        acc = pre * w_now + convb_ref[:, cs]
        top = pre[0:SUBLANES] * w_now + convb_ref[:, cs]
        for d in range(1, CONV_WIDTH):
            w_d = convw_ref[CONV_WIDTH - 1 - d:CONV_WIDTH - d, cs]
            shifted = pltpu.roll(pre, d, 0)
            acc = acc + shifted * w_d
            top = top + jnp.where(row8 < d, pltpu.roll(prev8, d, 0), shifted[0:SUBLANES]) * w_d
        prev_ref[:, cs] = pre[t - SUBLANES:t]
        conv = jnp.concatenate([top, acc[SUBLANES:]], axis=0)
        out_ref[...] = (conv * jax.nn.sigmoid(conv) * scale).astype(BF16)

    mv_out[...] = jnp.dot(xb, wmv_ref[...], preferred_element_type=F32).astype(BF16)

    gates = jnp.dot(xb, wg_ref[...], preferred_element_type=F32) + gbias_ref[...]
    g_out[...] = jnp.where(lane < ML_HEADS, gates, jax.nn.log_sigmoid(gates))


def _in_proj(x2, pos2, w_in, sinks, conv_w, conv_b, b_igate, b_fgate, batch, seq):
    n = x2.shape[0]
    t = min(1024, seq)
    s_tiles = seq // t

    def dup(w):
        h = ATT_HEAD_DIM
        return jnp.concatenate([w[:, :h], w[:, :h], w[:, h:], w[:, h:]], axis=1)

    o_k = ATT_Q_W
    o_v = o_k + ATT_KV_W
    o_mq = o_v + ATT_KV_W
    o_mv = o_mq + 2 * ML_QK_W
    o_mo = o_mv + ML_V_W
    o_mi = o_mo + ML_V_W
    w_rope = jnp.concatenate([w_in[:, :o_k], dup(w_in[:, o_k:o_v])], axis=1).astype(BF16)
    w_v = dup(w_in[:, o_v:o_mq]).astype(BF16)
    w_mqk = w_in[:, o_mq:o_mv].astype(BF16)
    w_mv = w_in[:, o_mv:o_mo].astype(BF16)
    w_g = jnp.pad(w_in[:, o_mi:o_mi + 2 * ML_HEADS], ((0, 0), (0, LANES - 2 * ML_HEADS))).astype(BF16)
    gbias = jnp.pad(jnp.concatenate([b_igate, b_fgate]).astype(F32), (0, LANES - 2 * ML_HEADS))[None, :]
    half = ATT_HEAD_DIM // 2
    inv = ROPE_THETA ** (-jnp.arange(half, dtype=F32) / half)
    quarter_turn = jnp.where((jnp.arange(LANES) % ATT_HEAD_DIM) < half, 0.0, jnp.pi / 2).astype(F32)
    inv = jnp.stack([jnp.tile(inv, LANES // half), quarter_turn])

    row = lambda w: pl.BlockSpec((t, w), lambda b, s, *_: (b * s_tiles + s, 0))
    grid_spec = pltpu.PrefetchScalarGridSpec(
        num_scalar_prefetch=1,
        grid=(batch, s_tiles),
        in_specs=[row(D_MODEL), row(1), _const_spec((2, LANES)),
                  _const_spec(w_rope.shape), _const_spec(w_v.shape), _const_spec(w_mqk.shape),
                  _const_spec(w_mv.shape), _const_spec(w_g.shape),
                  _const_spec((CONV_WIDTH, 2 * ML_QK_W)), _const_spec((1, 2 * ML_QK_W)),
                  _const_spec((1, LANES))],
        out_specs=[row(ATT_Q_W), row(ML_QK_W), row(ML_QK_W), row(ML_V_W), row(LANES)],
        scratch_shapes=[pltpu.VMEM((SUBLANES, 2 * ML_QK_W), F32),
                        pltpu.VMEM((ATT_BLOCK, 4 * ATT_KV_W), BF16)],
    )
    return pl.pallas_call(
        _inproj_kernel,
        grid_spec=grid_spec,
        out_shape=[jax.ShapeDtypeStruct((n, ATT_Q_W), BF16),
                   jax.ShapeDtypeStruct((n, ML_QK_W), BF16),
                   jax.ShapeDtypeStruct((n, ML_QK_W), BF16),
                   jax.ShapeDtypeStruct((n, ML_V_W), BF16),
                   jax.ShapeDtypeStruct((n, LANES), F32)],
        compiler_params=_params("arbitrary", "arbitrary"),
        name="in_proj",
    )(sinks.astype(F32) * LOG2_E, x2, pos2, inv, w_rope, w_v, w_mqk, w_mv, w_g, conv_w.astype(F32),
      conv_b.astype(F32)[None, :], gbias)


def _mlstm_reset(c_ref, n_ref, m_ref):
    c_ref[...] = jnp.zeros_like(c_ref)
    n_ref[...] = jnp.zeros_like(n_ref)
    m_ref[...] = jnp.zeros_like(m_ref)


def _mlstm_chunk(rows, q_ref, k_ref, v_ref, g_ref, o_ref, c_ref, n_ref, m_ref):
    nb = q_ref.shape[0]
    L = ML_CHUNK
    causal = lax.broadcasted_iota(I32, (L, L), 1) <= lax.broadcasted_iota(I32, (L, L), 0)
    rowi = lax.broadcasted_iota(I32, (L, LANES), 0)

    def scan_rows(x, op, identity):
        d = 1
        while d < L:
            x = op(x, jnp.where(rowi >= d, pltpu.roll(x, d, 0), identity))
            d *= 2
        return x

    def chunk():
        heads = [(bb, h) for bb in range(nb) for h in range(ML_HEADS)]

        gate = []
        for bb in range(nb):
            g = g_ref[bb, rows, :]
            b = pltpu.roll(scan_rows(g, jnp.add, 0.0), LANES - ML_HEADS, 1)
            a = g - b
            a_max = scan_rows(a, jnp.maximum, -jnp.inf)
            m_prev = m_ref[bb, 0:1, :]
            m_t = b + jnp.maximum(m_prev, a_max)
            u = b - m_t
            b_end = b[L - 1:L, :]
            m_new = b_end + jnp.maximum(m_prev, a_max[L - 1:L, :])
            m_ref[bb, 0:1, :] = m_new
            gate.append(dict(u=u, w_inter=jnp.exp(u + m_prev), floor=jnp.exp(-m_t),
                             w_state=jnp.exp(b_end + m_prev - m_new), w_k=jnp.exp(b_end + a - m_new),
                             a_t=a.T))

        nt = (((1,), (1,)), ((), ()))
        for bb, h in heads:
            gt = gate[bb]
            idx = bb * ML_HEADS + h
            col = slice(h, h + 1)
            qh = q_ref[bb, rows, h * ML_DQK:(h + 1) * ML_DQK]
            kh = k_ref[bb, rows, h * ML_DQK:(h + 1) * ML_DQK]
            vh = v_ref[bb, rows, h * ML_DV:(h + 1) * ML_DV]
            c_prev = c_ref[idx]
            n_prev = n_ref[idx, 0:1, :]

            w_intra = jnp.exp(jnp.where(causal, gt["u"][:, col] + gt["a_t"][col, :], -jnp.inf))
            s = lax.dot_general(qh, kh, nt, preferred_element_type=F32) * w_intra
            w_inter = gt["w_inter"][:, col]
            num = (w_inter * jnp.dot(qh, c_prev.astype(BF16), preferred_element_type=F32)
                   + jnp.dot(s.astype(BF16), vh, preferred_element_type=F32))
            den = (w_inter * jnp.sum(qh.astype(F32) * n_prev, axis=1, keepdims=True)
                   + jnp.sum(s, axis=1, keepdims=True))
            hval = num / jnp.maximum(jnp.abs(den), gt["floor"][:, col])
            hval = hval * lax.rsqrt(jnp.mean(hval * hval, axis=1, keepdims=True) + RMS_EPS)
            o_ref[bb, rows, h * ML_DV:(h + 1) * ML_DV] = hval.astype(BF16)

            w_state = gt["w_state"][:, col]
            kw = kh.astype(F32) * gt["w_k"][:, col]
            c_ref[idx] = w_state * c_prev + jnp.dot(kw.T.astype(BF16), vh, preferred_element_type=F32)
            n_ref[idx, 0:1, :] = w_state * n_prev + jnp.sum(kw, axis=0, keepdims=True)

    chunk()


def _memkv_kernel(mem_ref, w_ref, k_out, v_out):
    kv = jnp.dot(mem_ref[...].astype(BF16), w_ref[...], preferred_element_type=F32)
    k_out[...] = kv[:, :D_MODEL].astype(BF16)
    v_out[...] = kv[:, D_MODEL:].astype(BF16)


def _mem_kv(mem2, w_xkv, batch, mem_len):
    blk = pl.BlockSpec((mem_len, D_MODEL), lambda b: (b, 0))
    return pl.pallas_call(
        _memkv_kernel,
        grid=(batch,),
        in_specs=[blk, _const_spec((D_MODEL, 2 * D_MODEL))],
        out_specs=[blk, blk],
        out_shape=[jax.ShapeDtypeStruct((batch * mem_len, D_MODEL), BF16)] * 2,
        compiler_params=_params("arbitrary"),
        name="mem_kv",
    )(mem2, w_xkv.astype(BF16))


def _post_kernel(alpha, x_ref, att_ref, mq_ref, mk_ref, mv_ref, g_ref, kx_ref, vx_ref, wgate_ref, watt_ref,
                 wml_ref, wmix_ref, wxq_ref, wxo_ref, wr_hi_ref, wr_lo_ref, vec_ref, rbias_ref,
                 h_out, count_out, base_ref, c_ref, n_ref, m_ref, hm_ref):
    t = x_ref.shape[0]
    first = (pl.program_id(0) == 0) & (pl.program_id(1) == 0)

    @pl.when(first)
    def _():
        base_ref[...] = jnp.zeros_like(base_ref)

    @pl.when(pl.program_id(1) == 0)
    def _():
        _mlstm_reset(c_ref, n_ref, m_ref)

    for c in range(t // ML_CHUNK):
        _mlstm_chunk(slice(c * ML_CHUNK, (c + 1) * ML_CHUNK), mq_ref, mk_ref, mv_ref, g_ref, hm_ref,
                     c_ref, n_ref, m_ref)

    x = x_ref[...]
    xb = x.astype(BF16)
    norm_g, ln1_g, ln1_b, ln2_g, ln2_b = (vec_ref[i:i + 1, :] for i in range(5))

    chunks = [slice(c, c + POST_COLS) for c in range(0, D_MODEL, POST_COLS)]
    hm = jnp.concatenate(
        [(hm_ref[0, :, cs].astype(F32) * norm_g[:, cs]
          * jax.nn.sigmoid(jnp.dot(xb, wgate_ref[:, cs], preferred_element_type=F32))).astype(BF16)
         for cs in chunks], axis=1)
    att = att_ref[...]
    y = []
    for cs in chunks:
        gm = jnp.dot(xb, wgate_ref[:, 2 * D_MODEL + cs.start:2 * D_MODEL + cs.stop], preferred_element_type=F32)
        ga = jnp.dot(xb, wgate_ref[:, D_MODEL + cs.start:D_MODEL + cs.stop], preferred_element_type=F32)
        m_out = jnp.dot(hm, wml_ref[:, cs], preferred_element_type=F32)
        a_out = jnp.dot(att, watt_ref[:, cs], preferred_element_type=F32)
        y.append((jax.nn.sigmoid(gm) * m_out + jax.nn.sigmoid(ga) * a_out).astype(BF16))
    mix = jnp.dot(jnp.concatenate(y, axis=1), wmix_ref[...], preferred_element_type=F32)
    h1 = _layer_norm(alpha * x + mix, ln1_g, ln1_b)

    q = (jnp.dot(h1.astype(BF16), wxq_ref[...], preferred_element_type=F32) * (X_HEAD_DIM ** -0.5)).astype(BF16)
    heads = []
    for h in range(X_HEADS):
        sl = slice(h * X_HEAD_DIM, (h + 1) * X_HEAD_DIM)
        sc = lax.dot_general(q[:, sl], kx_ref[:, sl], (((1,), (1,)), ((), ())), preferred_element_type=F32)
        e = jnp.exp(sc - jnp.max(sc, axis=1, keepdims=True))
        probs = (e * (1.0 / jnp.sum(e, axis=1, keepdims=True))).astype(BF16)
        heads.append(jnp.dot(probs, vx_ref[:, sl], preferred_element_type=F32).astype(BF16))
    xa = jnp.dot(jnp.concatenate(heads, axis=1), wxo_ref[...], preferred_element_type=F32)
    h2 = _layer_norm(alpha * h1 + xa, ln2_g, ln2_b)
    h_out[:, :D_MODEL] = h2

    h_hi = h2.astype(BF16)
    h_lo = (h2 - h_hi.astype(F32)).astype(BF16)
    logits = (jnp.dot(h_hi, wr_hi_ref[...], preferred_element_type=F32)
              + jnp.dot(h_lo, wr_hi_ref[...], preferred_element_type=F32)
              + jnp.dot(h_hi, wr_lo_ref[...], preferred_element_type=F32)) + rbias_ref[...]

    lane = lax.broadcasted_iota(I32, (t, LANES), 1)
    neg = -jnp.inf
    is_group = lane < N_GROUPS
    gl = jnp.where(is_group, logits, neg)
    g_max = jnp.max(gl, axis=1, keepdims=True)
    lane_f = lane.astype(F32)
    no_lane = float(LANES)
    g_sel = jnp.min(jnp.where(gl == g_max, lane_f, no_lane), axis=1, keepdims=True)
    p_group = 1.0 / jnp.sum(jnp.where(is_group, jnp.exp(logits - g_max), 0.0), axis=1, keepdims=True)
    lane_group = ((lane - ROUTER_EXPERT_LANE0) >> 3).astype(F32)
    in_group = (lane >= ROUTER_EXPERT_LANE0) & (lane_group == g_sel)
    el = jnp.where(in_group, logits, neg)
    v1 = jnp.max(el, axis=1, keepdims=True)
    i1 = jnp.min(jnp.where(el == v1, lane_f, no_lane), axis=1, keepdims=True)
    el2 = jnp.where(lane_f == i1, neg, el)
    v2 = jnp.max(el2, axis=1, keepdims=True)
    i2 = jnp.min(jnp.where(el2 == v2, lane_f, no_lane), axis=1, keepdims=True)
    e2 = jnp.exp(v2 - v1)
    w1 = p_group * (1.0 / (1.0 + e2))
    w2 = p_group * (e2 / (1.0 + e2))

    onehot = jnp.where(lane_f == g_sel, 1.0, 0.0)
    tri = jnp.where(lax.broadcasted_iota(I32, (t, t), 1) < lax.broadcasted_iota(I32, (t, t), 0), 1.0, 0.0).astype(BF16)
    before = jnp.dot(tri, onehot.astype(BF16), preferred_element_type=F32) + base_ref[0:1, :]
    rank = jnp.sum(jnp.where(lane_f == g_sel, before, 0.0), axis=1, keepdims=True)
    total = base_ref[0:1, :] + jnp.sum(onehot, axis=0, keepdims=True)
    base_ref[...] = jnp.broadcast_to(total, base_ref.shape)
    count_out[...] = jnp.broadcast_to(total, count_out.shape)

    logit_lane = lane_f + (ROUTER_EXPERT_LANE0 - REC_WEIGHT_LANE0) + EXPERTS_PER_GROUP * g_sel
    rec = jnp.where(logit_lane == i1, w1, jnp.where(logit_lane == i2, w2, 0.0))
    rec = jnp.where((lane >= REC_WEIGHT_LANE0) & (lane < REC_WEIGHT_LANE0 + EXPERTS_PER_GROUP), rec, 0.0)
    rec = jnp.where(lane == REC_GROUP_LANE, g_sel, rec)
    rec = jnp.where(lane == REC_RANK_LANE, rank, rec)
    h_out[:, D_MODEL:] = rec


def _post(alpha, x2, att, mq, mk, mv, gates, kx, vx, w_gates, w_att, w_ml, w_mix, w_xq, w_xo, w_router, b_router,
          vecs, batch, seq, mem_len):
    n = x2.shape[0]
    t = min(512, seq)
    s_tiles = seq // t
    row = lambda w: pl.BlockSpec((t, w), lambda b, s: (b * s_tiles + s, 0))
    memblk = pl.BlockSpec((mem_len, D_MODEL), lambda b, s: (b, 0))
    seq_blk = lambda w: pl.BlockSpec((1, t, w), lambda b, s: (b, s, 0))
    seqs = lambda a: a.reshape(batch, seq, a.shape[-1])
    w_hi = w_router.astype(BF16)
    w_lo = (w_router - w_hi.astype(F32)).astype(BF16)
    sq = _const_spec((D_MODEL, D_MODEL))
    return pl.pallas_call(
        functools.partial(_post_kernel, alpha),
        grid=(batch, s_tiles),
        in_specs=[row(D_MODEL), row(ATT_Q_W), seq_blk(ML_QK_W), seq_blk(ML_QK_W), seq_blk(ML_V_W), seq_blk(LANES),
                  memblk, memblk,
                  _const_spec((D_MODEL, 3 * D_MODEL)), sq, sq, sq, sq, sq,
                  _const_spec((D_MODEL, LANES)), _const_spec((D_MODEL, LANES)),
                  _const_spec((SUBLANES, D_MODEL)), _const_spec((1, LANES))],
        out_specs=[row(ROW_W), _const_spec((SUBLANES, LANES))],
        out_shape=[jax.ShapeDtypeStruct((n, ROW_W), F32),
                   jax.ShapeDtypeStruct((SUBLANES, LANES), F32)],
        scratch_shapes=[pltpu.VMEM((SUBLANES, LANES), F32),
                        pltpu.VMEM((ML_HEADS, ML_DQK, ML_DV), F32),
                        pltpu.VMEM((ML_HEADS, SUBLANES, ML_DQK), F32),
                        pltpu.VMEM((1, SUBLANES, LANES), F32),
                        pltpu.VMEM((1, t, ML_V_W), BF16)],
        compiler_params=_params("arbitrary", "arbitrary"),
        name="post",
    )(x2, att, seqs(mq), seqs(mk), seqs(mv), seqs(gates), kx, vx, w_gates, w_att, w_ml, w_mix, w_xq, w_xo, w_hi, w_lo, vecs, b_router)


DMA_UNROLL = 32


def _scatter_kernel(dest_ref, padend_ref, h_ref, xs_out, zero_buf, sem, zero_sem):
    i = pl.program_id(0)
    ts = h_ref.shape[0]

    @pl.when(i == 0)
    def _():
        zero_buf[...] = jnp.zeros_like(zero_buf)

        def clear(start):
            start = pl.multiple_of(start, GROUP_BLOCK)
            return pltpu.make_async_copy(zero_buf, xs_out.at[pl.ds(start, GROUP_BLOCK), :], zero_sem)

        def nonempty(g):
            return padend_ref[g] > (padend_ref[g - 1] if g else 0)

        used = padend_ref[N_GROUPS - 1]

        def unused(b):
            return used + b * GROUP_BLOCK < xs_out.shape[0]

        for phase in ("start", "wait"):
            for g in range(N_GROUPS):
                @pl.when(nonempty(g))
                def _():
                    getattr(clear(padend_ref[g] - GROUP_BLOCK), phase)()

                @pl.when(unused(g))
                def _():
                    getattr(clear(used + g * GROUP_BLOCK), phase)()

    def issue(j, carry):
        for u in range(DMA_UNROLL):
            t = j * DMA_UNROLL + u
            pltpu.make_async_copy(h_ref.at[pl.ds(t, 1), :], xs_out.at[pl.ds(dest_ref[i * ts + t], 1), :], sem).start()
        return carry

    lax.fori_loop(0, ts // DMA_UNROLL, issue, 0)

    pltpu.make_async_copy(h_ref, xs_out.at[pl.ds(0, ts), :], sem).wait()


def _moe_scatter(dest, pad_end, h2aug, rows):
    n = h2aug.shape[0]
    ts = min(1024, n)
    grid_spec = pltpu.PrefetchScalarGridSpec(
        num_scalar_prefetch=2,
        grid=(n // ts,),
        in_specs=[pl.BlockSpec((ts, ROW_W), lambda i, *_: (i, 0))],
        out_specs=pl.BlockSpec(memory_space=pl.ANY),
        scratch_shapes=[pltpu.VMEM((GROUP_BLOCK, ROW_W), F32),
                        pltpu.SemaphoreType.DMA(()), pltpu.SemaphoreType.DMA(())],
    )
    return pl.pallas_call(
        _scatter_kernel,
        grid_spec=grid_spec,
        out_shape=jax.ShapeDtypeStruct((rows, ROW_W), F32),
        compiler_params=_params("arbitrary"),
        name="moe_scatter",
    )(dest, pad_end, h2aug)


def _group_kernel(bgrp_ref, nused_ref, x_ref, wgu_ref, wd_ref, y_ref):
    del bgrp_ref
    i = pl.program_id(0)
    n_used = nused_ref[0]

    @pl.when(i < n_used)
    def _():
        xb = x_ref[:, :D_MODEL].astype(BF16)
        hidden = []
        for e in range(EXPERTS_PER_GROUP):
            gu = jnp.dot(xb, wgu_ref[0, e], preferred_element_type=F32)
            gate = gu[:, :D_EXPERT]
            lane = D_MODEL + REC_WEIGHT_LANE0 + e
            w = x_ref[:, lane:lane + 1]
            hw = jnp.where(w != 0.0, w * (gate * jax.nn.sigmoid(gate) * gu[:, D_EXPERT:]), 0.0)
            hidden.append(hw.astype(BF16))
        y_ref[...] = jnp.dot(jnp.concatenate(hidden, axis=1), wd_ref[0], preferred_element_type=F32)

    @pl.when(i >= n_used)
    def _():
        y_ref[...] = jnp.zeros_like(y_ref)


def _moe_groups(block_group, n_used, xs, w_gu, w_down):
    rows = xs.shape[0]
    e = EXPERTS_PER_GROUP
    grid_spec = pltpu.PrefetchScalarGridSpec(
        num_scalar_prefetch=2,
        grid=(rows // GROUP_BLOCK,),
        in_specs=[pl.BlockSpec((GROUP_BLOCK, ROW_W), lambda i, *_: (i, 0)),
                  pl.BlockSpec((1, e, D_MODEL, 2 * D_EXPERT), lambda i, bg, nu: (bg[i], 0, 0, 0)),
                  pl.BlockSpec((1, e * D_EXPERT, D_MODEL), lambda i, bg, nu: (bg[i], 0, 0))],
        out_specs=pl.BlockSpec((GROUP_BLOCK, D_MODEL), lambda i, *_: (i, 0)),
    )
    return pl.pallas_call(
        _group_kernel,
        grid_spec=grid_spec,
        out_shape=jax.ShapeDtypeStruct((rows, D_MODEL), F32),
        compiler_params=_params("arbitrary"),
        name="moe_groups",
    )(block_group, n_used, xs, w_gu, w_down)


def _combine_kernel(alpha, dest_ref, h_ref, vec_ref, y_hbm, o_ref, ybuf, sems):
    tc = h_ref.shape[0]
    i = pl.program_id(0)
    steps = pl.num_programs(0)

    def gather(tile, slot):
        def issue(j, carry):
            for u in range(DMA_UNROLL):
                t = j * DMA_UNROLL + u
                pltpu.make_async_copy(y_hbm.at[pl.ds(dest_ref[tile * tc + t], 1), :],
                                      ybuf.at[slot, pl.ds(t, 1), :], sems.at[slot]).start()
            return carry
        lax.fori_loop(0, tc // DMA_UNROLL, issue, 0)

    @pl.when(i == 0)
    def _():
        gather(0, 0)

    @pl.when(i + 1 < steps)
    def _():
        gather(i + 1, (i + 1) % 2)

    slot = i % 2
    pltpu.make_async_copy(y_hbm.at[pl.ds(0, tc), :], ybuf.at[slot], sems.at[slot]).wait()
    o_ref[...] = _layer_norm(alpha * h_ref[...] + ybuf[slot], vec_ref[0:1, :], vec_ref[1:2, :])


def _moe_combine(alpha, dest, h2aug, vecs, y):
    n = h2aug.shape[0]
    tc = min(512, n)
    grid_spec = pltpu.PrefetchScalarGridSpec(
        num_scalar_prefetch=1,
        grid=(n // tc,),
        in_specs=[pl.BlockSpec((tc, D_MODEL), lambda i, *_: (i, 0)),
                  pl.BlockSpec((SUBLANES, D_MODEL), lambda i, *_: (0, 0)),
                  pl.BlockSpec(memory_space=pl.ANY)],
        out_specs=pl.BlockSpec((tc, D_MODEL), lambda i, *_: (i, 0)),
        scratch_shapes=[pltpu.VMEM((2, tc, D_MODEL), F32), pltpu.SemaphoreType.DMA((2,))],
    )
    return pl.pallas_call(
        functools.partial(_combine_kernel, alpha),
        grid_spec=grid_spec,
        out_shape=jax.ShapeDtypeStruct((n, D_MODEL), F32),
        compiler_params=_params("arbitrary"),
        name="moe_combine",
    )(dest, h2aug, vecs, y)


def _pad_rows(vectors):
    rows = [v.astype(F32)[None, :] for v in vectors]
    rows.append(jnp.zeros((SUBLANES - len(rows), vectors[0].shape[0]), F32))
    return jnp.concatenate(rows, axis=0)


def _layer(alpha, h, mem2, pos2, batch, seq, mem_len, w_in, attn_sinks, conv_w, conv_b, b_igate, b_fgate,
           ml_norm_g, w_att_branch, w_ml_branch, w_mix_out, ln1_g, ln1_b, w_xq, w_xkv, w_xo, ln2_g, ln2_b,
           w_router_group, b_router_group, w_router_expert, b_router_expert, w_gate, w_up, w_down,
           ln3_g, ln3_b):
    n = h.shape[0]
    att, mq, mk, mv, gates = _in_proj(h, pos2, w_in, attn_sinks, conv_w, conv_b, b_igate, b_fgate, batch, seq)
    kx, vx = _mem_kv(mem2, w_xkv, batch, mem_len)

    o_mo = ATT_Q_W + 2 * ATT_KV_W + 2 * ML_QK_W + ML_V_W
    o_ga = o_mo + ML_V_W + 2 * ML_HEADS
    w_gates = jnp.concatenate([w_in[:, o_mo:o_mo + ML_V_W], w_in[:, o_ga:]], axis=1).astype(BF16)
    pad = LANES - N_GROUPS - N_EXPERTS
    w_router = jnp.pad(jnp.concatenate([w_router_group, w_router_expert], axis=1).astype(F32), ((0, 0), (0, pad)))
    b_router = jnp.pad(jnp.concatenate([b_router_group, b_router_expert]).astype(F32), (0, pad))[None, :]
    vecs = _pad_rows([ml_norm_g, ln1_g, ln1_b, ln2_g, ln2_b])
    h2aug, counts = _post(alpha, h, att, mq, mk, mv, gates, kx, vx, w_gates, w_att_branch.astype(BF16),
                          w_ml_branch.astype(BF16), w_mix_out.astype(BF16), w_xq.astype(BF16),
                          w_xo.astype(BF16), w_router, b_router, vecs, batch, seq, mem_len)

    counts = counts[0, :N_GROUPS].astype(I32)
    padded = ((counts + GROUP_BLOCK - 1) // GROUP_BLOCK) * GROUP_BLOCK
    pad_end = jnp.cumsum(padded).astype(I32)
    pad_start = pad_end - padded
    n_blocks = n // GROUP_BLOCK + N_GROUPS
    block_start = jnp.arange(n_blocks, dtype=I32) * GROUP_BLOCK
    block_group = jnp.sum((pad_end[None, :] <= block_start[:, None]).astype(I32), axis=1)
    block_group = jnp.minimum(block_group, N_GROUPS - 1)
    n_used = pad_end[-1:] // GROUP_BLOCK
    group = h2aug[:, D_MODEL + REC_GROUP_LANE].astype(I32)
    rank = h2aug[:, D_MODEL + REC_RANK_LANE].astype(I32)
    dest = pad_start[group] + rank

    xs = _moe_scatter(dest, pad_end, h2aug, n_blocks * GROUP_BLOCK)
    e = EXPERTS_PER_GROUP
    w_gu = jnp.concatenate([w_gate, w_up], axis=2).astype(BF16).reshape(N_GROUPS, e, D_MODEL, 2 * D_EXPERT)
    w_dn = w_down.astype(BF16).reshape(N_GROUPS, e * D_EXPERT, D_MODEL)
    y = _moe_groups(block_group, n_used, xs, w_gu, w_dn)
    return _moe_combine(alpha, dest, h2aug, _pad_rows([ln3_g, ln3_b]), y)


def kernel(x, mem, positions, w_in, attn_sinks, conv_w, conv_b, b_igate, b_fgate, ml_norm_g, w_att_branch, w_ml_branch, w_mix_out, ln1_g, ln1_b, w_xq, w_xkv, w_xo, ln2_g, ln2_b, w_router_group, b_router_group, w_router_expert, b_router_expert, w_gate, w_up, w_down, ln3_g, ln3_b):
    batch, seq, d = x.shape
    mem_len = mem.shape[1]
    depth = w_in.shape[0]
    alpha = (2 * depth) ** 0.25
    h = x.reshape(batch * seq, d)
    mem2 = mem.reshape(batch * mem_len, d)
    pos2 = positions.reshape(batch * seq, 1)
    stacked = (w_in, attn_sinks, conv_w, conv_b, b_igate, b_fgate, ml_norm_g, w_att_branch, w_ml_branch,
               w_mix_out, ln1_g, ln1_b, w_xq, w_xkv, w_xo, ln2_g, ln2_b, w_router_group, b_router_group,
               w_router_expert, b_router_expert, w_gate, w_up, w_down, ln3_g, ln3_b)
    for l in range(depth):
        h = _layer(alpha, h, mem2, pos2, batch, seq, mem_len, *(w[l] for w in stacked))
    return h.reshape(batch, seq, d)
```

```python
import functools

import jax
import jax.numpy as jnp
from jax import lax
from jax.experimental import pallas as pl
from jax.experimental.pallas import tpu as pltpu

F32 = jnp.float32
BF16 = jnp.bfloat16
I32 = jnp.int32

D_MODEL = 1024
ATT_HEADS = 16
ATT_KV_HEADS = 2
ATT_HEAD_DIM = 64
ATT_BLOCK = 128
ROPE_THETA = 10000.0
ML_HEADS = 4
ML_DQK = 128
ML_DV = 256
ML_CHUNK = 128
CONV_WIDTH = 4
X_HEADS = 4
X_HEAD_DIM = D_MODEL // X_HEADS
N_GROUPS = 8
EXPERTS_PER_GROUP = 8
N_EXPERTS = N_GROUPS * EXPERTS_PER_GROUP
TOP_K = 2
D_EXPERT = 256
MOE_BLOCK = 128
LN_EPS = 1e-5
RMS_EPS = 1e-6

ATT_Q_W = ATT_HEADS * ATT_HEAD_DIM
ATT_KV_W = ATT_KV_HEADS * ATT_HEAD_DIM
ML_QK_W = ML_HEADS * ML_DQK
ML_V_W = ML_HEADS * ML_DV

LANES = 128
SUBLANES = 8
VMEM_LIMIT_BYTES = 56 * 1024 * 1024

ROUTER_EXPERT_LANE0 = N_GROUPS
PAIRS_PER_GROUP = EXPERTS_PER_GROUP * (EXPERTS_PER_GROUP - 1) // 2
N_CLASSES = N_GROUPS * PAIRS_PER_GROUP
CLASS_LANES = 256
REC_CLASS_LANE = 0
REC_RANK_LANE = 1
REC_WEIGHT_LANE0 = 8
ROW_W = D_MODEL + LANES
GROUP_BLOCK = 512
LOG2_E = 1.4426950408889634
POST_COLS = 512


def _params(*semantics):
    return pltpu.CompilerParams(dimension_semantics=semantics, vmem_limit_bytes=VMEM_LIMIT_BYTES)


def _const_spec(shape):
    zeros = (0,) * len(shape)
    return pl.BlockSpec(shape, lambda *_: zeros, pipeline_mode=pl.Buffered(1))


def _layer_norm(z, g, b):
    mu = jnp.mean(z, axis=-1, keepdims=True)
    zc = z - mu
    var = jnp.mean(zc * zc, axis=-1, keepdims=True)
    return zc * lax.rsqrt(var + LN_EPS) * g + b


def _inproj_kernel(sink_ref, x_ref, pos_ref, inv_ref, wrope_ref, wv_ref, wmqk_ref, wmv_ref, wg_ref,
                   convw_ref, convb_ref, gbias_ref,
                   att_out, mq_out, mk_out, mv_out, g_out, prev_ref, kvprev_ref):
    t = x_ref.shape[0]
    first = pl.program_id(1) == 0
    xb = x_ref[...].astype(BF16)

    half = ATT_HEAD_DIM // 2
    ang = pos_ref[...].astype(F32) * inv_ref[0:1, :]
    sin_cos = jnp.sin(ang + inv_ref[1:2, :])
    lane = lax.broadcasted_iota(I32, (t, LANES), 1)
    first_half = (lane % ATT_HEAD_DIM) < half
    cos = jnp.where(first_half, pltpu.roll(sin_cos, LANES - half, 1), sin_cos)
    sin_signed = jnp.where(first_half, -sin_cos, pltpu.roll(sin_cos, half, 1))

    def rope(z):
        partner = jnp.where(first_half, pltpu.roll(z, LANES - ATT_HEAD_DIM // 2, 1),
                            pltpu.roll(z, ATT_HEAD_DIM // 2, 1))
        return z * cos + partner * sin_signed

    qk = jnp.dot(xb, wrope_ref[...], preferred_element_type=F32)
    q_scale = ATT_HEAD_DIM ** -0.5 * LOG2_E
    q_groups = [(rope(qk[:, g * LANES:(g + 1) * LANES]) * q_scale).astype(BF16) for g in range(ATT_Q_W // LANES)]
    k_groups = [rope(qk[:, ATT_Q_W + g * LANES:ATT_Q_W + (g + 1) * LANES]).astype(BF16)
                for g in range(ATT_KV_HEADS)]
    v_all = jnp.dot(xb, wv_ref[...], preferred_element_type=F32).astype(BF16)
    v_groups = [v_all[:, g * LANES:(g + 1) * LANES] for g in range(ATT_KV_HEADS)]

    nk = 2 * ATT_BLOCK
    kc = lax.broadcasted_iota(I32, (nk, ATT_BLOCK), 0)
    qi = lax.broadcasted_iota(I32, (nk, ATT_BLOCK), 1)
    band = (kc > qi) & (kc <= qi + ATT_BLOCK)
    lo = lax.broadcasted_iota(I32, (nk, LANES), 1) < ATT_HEAD_DIM
    dim_lo = lax.broadcasted_iota(I32, (LANES, ATT_BLOCK), 0) < ATT_HEAD_DIM
    pairs = ATT_HEADS // ATT_KV_HEADS // 2

    @pl.when(first)
    def _():
        kvprev_ref[...] = jnp.zeros_like(kvprev_ref)

    for j in range(t // ATT_BLOCK):
        cur = slice(j * ATT_BLOCK, (j + 1) * ATT_BLOCK)
        valid = band & (kc >= jnp.where(first, ATT_BLOCK, 0)) if j == 0 else band
        for kvh in range(ATT_KV_HEADS):
            if j == 0:
                k_prev = kvprev_ref[:, kvh * LANES:(kvh + 1) * LANES]
                v_prev = kvprev_ref[:, (ATT_KV_HEADS + kvh) * LANES:(ATT_KV_HEADS + kvh + 1) * LANES]
            else:
                prev = slice((j - 1) * ATT_BLOCK, j * ATT_BLOCK)
                k_prev, v_prev = k_groups[kvh][prev], v_groups[kvh][prev]
            kk = jnp.concatenate([k_prev, k_groups[kvh][cur]], axis=0)
            vv = jnp.concatenate([v_prev, v_groups[kvh][cur]], axis=0).astype(F32)
            zero = jnp.zeros_like(kk)
            k2 = jnp.concatenate([jnp.where(lo, kk, zero), jnp.where(lo, zero, kk)], axis=0)
            v2t = jnp.concatenate([jnp.where(lo, vv, 0.0), jnp.where(lo, 0.0, vv)], axis=0).T.astype(BF16)
            for pp in range(pairs):
                p = kvh * pairs + pp
                qp = q_groups[p][cur]
                st = lax.dot_general(k2, qp, (((1,), (1,)), ((), ())), preferred_element_type=F32)
                e_heads, recip = [], []
                for hh in range(2):
                    sink = sink_ref[2 * p + hh]
                    s_h = jnp.where(valid, st[hh * nk:(hh + 1) * nk], -jnp.inf)
                    m = jnp.maximum(jnp.max(s_h, axis=0, keepdims=True), sink)
                    e = jnp.exp2(s_h - m)
                    den = jnp.sum(e, axis=0, keepdims=True) + jnp.exp2(sink - m)
                    e_heads.append(e.astype(BF16))
                    recip.append(1.0 / den)
                out_t = jnp.dot(v2t, jnp.concatenate(e_heads, axis=0), preferred_element_type=F32)
                out_t = out_t * jnp.where(dim_lo, recip[0], recip[1])
                att_out[cur, p * LANES:(p + 1) * LANES] = out_t.T.astype(BF16)
    last = slice(t - ATT_BLOCK, t)
    kvprev_ref[...] = jnp.concatenate([g[last] for g in k_groups] + [g[last] for g in v_groups], axis=1)

    @pl.when(pl.program_id(1) == 0)
    def _():
        prev_ref[...] = jnp.zeros_like(prev_ref)

    row8 = lax.broadcasted_iota(I32, (SUBLANES, ML_QK_W), 0)
    for out_ref, scale, cs in ((mq_out, 1.0, slice(0, ML_QK_W)), (mk_out, ML_DQK ** -0.5, slice(ML_QK_W, 2 * ML_QK_W))):
        pre = jnp.dot(xb, wmqk_ref[:, cs], preferred_element_type=F32)
        prev8 = prev_ref[:, cs]
        w_now = convw_ref[CONV_WIDTH - 1:CONV_WIDTH, cs]
        acc = pre * w_now + convb_ref[:, cs]
        top = pre[0:SUBLANES] * w_now + convb_ref[:, cs]
        for d in range(1, CONV_WIDTH):
            w_d = convw_ref[CONV_WIDTH - 1 - d:CONV_WIDTH - d, cs]
            shifted = pltpu.roll(pre, d, 0)
            acc = acc + shifted * w_d
            top = top + jnp.where(row8 < d, pltpu.roll(prev8, d, 0), shifted[0:SUBLANES]) * w_d
        prev_ref[:, cs] = pre[t - SUBLANES:t]
        conv = jnp.concatenate([top, acc[SUBLANES:]], axis=0)
        out_ref[...] = (conv * jax.nn.sigmoid(conv) * scale).astype(BF16)

    mv_out[...] = jnp.dot(xb, wmv_ref[...], preferred_element_type=F32).astype(BF16)

    gates = jnp.dot(xb, wg_ref[...], preferred_element_type=F32) + gbias_ref[...]
    g_out[...] = jnp.where(lane < ML_HEADS, gates, jax.nn.log_sigmoid(gates))


def _in_proj(x2, pos2, w_in, sinks, conv_w, conv_b, b_igate, b_fgate, batch, seq):
    n = x2.shape[0]
    t = min(1024, seq)
    s_tiles = seq // t

    def dup(w):
        h = ATT_HEAD_DIM
        return jnp.concatenate([w[:, :h], w[:, :h], w[:, h:], w[:, h:]], axis=1)

    o_k = ATT_Q_W
    o_v = o_k + ATT_KV_W
    o_mq = o_v + ATT_KV_W
    o_mv = o_mq + 2 * ML_QK_W
    o_mo = o_mv + ML_V_W
    o_mi = o_mo + ML_V_W
    w_rope = jnp.concatenate([w_in[:, :o_k], dup(w_in[:, o_k:o_v])], axis=1).astype(BF16)
    w_v = dup(w_in[:, o_v:o_mq]).astype(BF16)
    w_mqk = w_in[:, o_mq:o_mv].astype(BF16)
    w_mv = w_in[:, o_mv:o_mo].astype(BF16)
    w_g = jnp.pad(w_in[:, o_mi:o_mi + 2 * ML_HEADS], ((0, 0), (0, LANES - 2 * ML_HEADS))).astype(BF16)
    gbias = jnp.pad(jnp.concatenate([b_igate, b_fgate]).astype(F32), (0, LANES - 2 * ML_HEADS))[None, :]
    half = ATT_HEAD_DIM // 2
    inv = ROPE_THETA ** (-jnp.arange(half, dtype=F32) / half)
    quarter_turn = jnp.where((jnp.arange(LANES) % ATT_HEAD_DIM) < half, 0.0, jnp.pi / 2).astype(F32)
    inv = jnp.stack([jnp.tile(inv, LANES // half), quarter_turn])

    row = lambda w: pl.BlockSpec((t, w), lambda b, s, *_: (b * s_tiles + s, 0))
    grid_spec = pltpu.PrefetchScalarGridSpec(
        num_scalar_prefetch=1,
        grid=(batch, s_tiles),
        in_specs=[row(D_MODEL), row(1), _const_spec((2, LANES)),
                  _const_spec(w_rope.shape), _const_spec(w_v.shape), _const_spec(w_mqk.shape),
                  _const_spec(w_mv.shape), _const_spec(w_g.shape),
                  _const_spec((CONV_WIDTH, 2 * ML_QK_W)), _const_spec((1, 2 * ML_QK_W)),
                  _const_spec((1, LANES))],
        out_specs=[row(ATT_Q_W), row(ML_QK_W), row(ML_QK_W), row(ML_V_W), row(LANES)],
        scratch_shapes=[pltpu.VMEM((SUBLANES, 2 * ML_QK_W), F32),
                        pltpu.VMEM((ATT_BLOCK, 4 * ATT_KV_W), BF16)],
    )
    return pl.pallas_call(
        _inproj_kernel,
        grid_spec=grid_spec,
        out_shape=[jax.ShapeDtypeStruct((n, ATT_Q_W), BF16),
                   jax.ShapeDtypeStruct((n, ML_QK_W), BF16),
                   jax.ShapeDtypeStruct((n, ML_QK_W), BF16),
                   jax.ShapeDtypeStruct((n, ML_V_W), BF16),
                   jax.ShapeDtypeStruct((n, LANES), F32)],
        compiler_params=_params("arbitrary", "arbitrary"),
        name="in_proj",
    )(sinks.astype(F32) * LOG2_E, x2, pos2, inv, w_rope, w_v, w_mqk, w_mv, w_g, conv_w.astype(F32),
      conv_b.astype(F32)[None, :], gbias)


def _mlstm_reset(c_ref, n_ref, m_ref):
    c_ref[...] = jnp.zeros_like(c_ref)
    n_ref[...] = jnp.zeros_like(n_ref)
    m_ref[...] = jnp.zeros_like(m_ref)


def _mlstm_chunk(rows, q_ref, k_ref, v_ref, g_ref, o_ref, c_ref, n_ref, m_ref):
    nb = q_ref.shape[0]
    L = ML_CHUNK
    causal = lax.broadcasted_iota(I32, (L, L), 1) <= lax.broadcasted_iota(I32, (L, L), 0)
    rowi = lax.broadcasted_iota(I32, (L, LANES), 0)

    def scan_rows(x, op, identity):
        d = 1
        while d < L:
            x = op(x, jnp.where(rowi >= d, pltpu.roll(x, d, 0), identity))
            d *= 2
        return x

    def chunk():
        heads = [(bb, h) for bb in range(nb) for h in range(ML_HEADS)]

        gate = []
        for bb in range(nb):
            g = g_ref[bb, rows, :]
            b = pltpu.roll(scan_rows(g, jnp.add, 0.0), LANES - ML_HEADS, 1)
            a = g - b
            a_max = scan_rows(a, jnp.maximum, -jnp.inf)
            m_prev = m_ref[bb, 0:1, :]
            m_t = b + jnp.maximum(m_prev, a_max)
            u = b - m_t
            b_end = b[L - 1:L, :]
            m_new = b_end + jnp.maximum(m_prev, a_max[L - 1:L, :])
            m_ref[bb, 0:1, :] = m_new
            gate.append(dict(u=u, w_inter=jnp.exp(u + m_prev), floor=jnp.exp(-m_t),
                             w_state=jnp.exp(b_end + m_prev - m_new), w_k=jnp.exp(b_end + a - m_new),
                             a_t=a.T))

        nt = (((1,), (1,)), ((), ()))
        for bb, h in heads:
            gt = gate[bb]
            idx = bb * ML_HEADS + h
            col = slice(h, h + 1)
            qh = q_ref[bb, rows, h * ML_DQK:(h + 1) * ML_DQK]
            kh = k_ref[bb, rows, h * ML_DQK:(h + 1) * ML_DQK]
            vh = v_ref[bb, rows, h * ML_DV:(h + 1) * ML_DV]
            c_prev = c_ref[idx]
            n_prev = n_ref[idx, 0:1, :]

            w_intra = jnp.exp(jnp.where(causal, gt["u"][:, col] + gt["a_t"][col, :], -jnp.inf))
            s = lax.dot_general(qh, kh, nt, preferred_element_type=F32) * w_intra
            w_inter = gt["w_inter"][:, col]
            num = (w_inter * jnp.dot(qh, c_prev.astype(BF16), preferred_element_type=F32)
                   + jnp.dot(s.astype(BF16), vh, preferred_element_type=F32))
            den = (w_inter * jnp.sum(qh.astype(F32) * n_prev, axis=1, keepdims=True)
                   + jnp.sum(s, axis=1, keepdims=True))
            hval = num / jnp.maximum(jnp.abs(den), gt["floor"][:, col])
            hval = hval * lax.rsqrt(jnp.mean(hval * hval, axis=1, keepdims=True) + RMS_EPS)
            o_ref[bb, rows, h * ML_DV:(h + 1) * ML_DV] = hval.astype(BF16)

            w_state = gt["w_state"][:, col]
            kw = kh.astype(F32) * gt["w_k"][:, col]
            c_ref[idx] = w_state * c_prev + jnp.dot(kw.T.astype(BF16), vh, preferred_element_type=F32)
            n_ref[idx, 0:1, :] = w_state * n_prev + jnp.sum(kw, axis=0, keepdims=True)

    chunk()


def _memkv_kernel(mem_ref, w_ref, k_out, v_out):
    kv = jnp.dot(mem_ref[...].astype(BF16), w_ref[...], preferred_element_type=F32)
    k_out[...] = kv[:, :D_MODEL].astype(BF16)
    v_out[...] = kv[:, D_MODEL:].astype(BF16)


def _mem_kv(mem2, w_xkv, batch, mem_len):
    blk = pl.BlockSpec((mem_len, D_MODEL), lambda b: (b, 0))
    return pl.pallas_call(
        _memkv_kernel,
        grid=(batch,),
        in_specs=[blk, _const_spec((D_MODEL, 2 * D_MODEL))],
        out_specs=[blk, blk],
        out_shape=[jax.ShapeDtypeStruct((batch * mem_len, D_MODEL), BF16)] * 2,
        compiler_params=_params("arbitrary"),
        name="mem_kv",
    )(mem2, w_xkv.astype(BF16))


def _post_kernel(alpha, x_ref, att_ref, mq_ref, mk_ref, mv_ref, g_ref, kx_ref, vx_ref, wgate_ref, watt_ref,
                 wml_ref, wmix_ref, wxq_ref, wxo_ref, wr_hi_ref, wr_lo_ref, vec_ref, rbias_ref,
                 h_out, count_out, base_ref, c_ref, n_ref, m_ref, hm_ref):
    t = x_ref.shape[0]
    first = (pl.program_id(0) == 0) & (pl.program_id(1) == 0)

    @pl.when(first)
    def _():
        base_ref[...] = jnp.zeros_like(base_ref)

    @pl.when(pl.program_id(1) == 0)
    def _():
        _mlstm_reset(c_ref, n_ref, m_ref)

    for c in range(t // ML_CHUNK):
        _mlstm_chunk(slice(c * ML_CHUNK, (c + 1) * ML_CHUNK), mq_ref, mk_ref, mv_ref, g_ref, hm_ref,
                     c_ref, n_ref, m_ref)

    x = x_ref[...]
    xb = x.astype(BF16)
    norm_g, ln1_g, ln1_b, ln2_g, ln2_b = (vec_ref[i:i + 1, :] for i in range(5))

    chunks = [slice(c, c + POST_COLS) for c in range(0, D_MODEL, POST_COLS)]
    hm = jnp.concatenate(
        [(hm_ref[0, :, cs].astype(F32) * norm_g[:, cs]
          * jax.nn.sigmoid(jnp.dot(xb, wgate_ref[:, cs], preferred_element_type=F32))).astype(BF16)
         for cs in chunks], axis=1)
    att = att_ref[...]
    y = []
    for cs in chunks:
        gm = jnp.dot(xb, wgate_ref[:, 2 * D_MODEL + cs.start:2 * D_MODEL + cs.stop], preferred_element_type=F32)
        ga = jnp.dot(xb, wgate_ref[:, D_MODEL + cs.start:D_MODEL + cs.stop], preferred_element_type=F32)
        m_out = jnp.dot(hm, wml_ref[:, cs], preferred_element_type=F32)
        a_out = jnp.dot(att, watt_ref[:, cs], preferred_element_type=F32)
        y.append((jax.nn.sigmoid(gm) * m_out + jax.nn.sigmoid(ga) * a_out).astype(BF16))
    mix = jnp.dot(jnp.concatenate(y, axis=1), wmix_ref[...], preferred_element_type=F32)
    h1 = _layer_norm(alpha * x + mix, ln1_g, ln1_b)

    q = (jnp.dot(h1.astype(BF16), wxq_ref[...], preferred_element_type=F32) * (X_HEAD_DIM ** -0.5)).astype(BF16)
    heads = []
    for h in range(X_HEADS):
        sl = slice(h * X_HEAD_DIM, (h + 1) * X_HEAD_DIM)
        sc = lax.dot_general(q[:, sl], kx_ref[:, sl], (((1,), (1,)), ((), ())), preferred_element_type=F32)
        e = jnp.exp(sc - jnp.max(sc, axis=1, keepdims=True))
        probs = (e * (1.0 / jnp.sum(e, axis=1, keepdims=True))).astype(BF16)
        heads.append(jnp.dot(probs, vx_ref[:, sl], preferred_element_type=F32).astype(BF16))
    xa = jnp.dot(jnp.concatenate(heads, axis=1), wxo_ref[...], preferred_element_type=F32)
    h2 = _layer_norm(alpha * h1 + xa, ln2_g, ln2_b)
    h_out[:, :D_MODEL] = h2

    h_hi = h2.astype(BF16)
    h_lo = (h2 - h_hi.astype(F32)).astype(BF16)
    logits = (jnp.dot(h_hi, wr_hi_ref[...], preferred_element_type=F32)
              + jnp.dot(h_lo, wr_hi_ref[...], preferred_element_type=F32)
              + jnp.dot(h_hi, wr_lo_ref[...], preferred_element_type=F32)) + rbias_ref[...]

    lane = lax.broadcasted_iota(I32, (t, LANES), 1)
    neg = -jnp.inf
    is_group = lane < N_GROUPS
    gl = jnp.where(is_group, logits, neg)
    g_max = jnp.max(gl, axis=1, keepdims=True)
    lane_f = lane.astype(F32)
    no_lane = float(LANES)
    g_sel = jnp.min(jnp.where(gl == g_max, lane_f, no_lane), axis=1, keepdims=True)
    p_group = 1.0 / jnp.sum(jnp.where(is_group, jnp.exp(logits - g_max), 0.0), axis=1, keepdims=True)
    lane_group = ((lane - ROUTER_EXPERT_LANE0) >> 3).astype(F32)
    in_group = (lane >= ROUTER_EXPERT_LANE0) & (lane_group == g_sel)
    el = jnp.where(in_group, logits, neg)
    v1 = jnp.max(el, axis=1, keepdims=True)
    i1 = jnp.min(jnp.where(el == v1, lane_f, no_lane), axis=1, keepdims=True)
    el2 = jnp.where(lane_f == i1, neg, el)
    v2 = jnp.max(el2, axis=1, keepdims=True)
    i2 = jnp.min(jnp.where(el2 == v2, lane_f, no_lane), axis=1, keepdims=True)
    e2 = jnp.exp(v2 - v1)
    w1 = p_group * (1.0 / (1.0 + e2))
    w2 = p_group * (e2 / (1.0 + e2))

    first_logit = ROUTER_EXPERT_LANE0 + EXPERTS_PER_GROUP * g_sel
    lo = jnp.minimum(i1, i2) - first_logit
    hi = jnp.maximum(i1, i2) - first_logit
    cls = PAIRS_PER_GROUP * g_sel + lo * (2 * EXPERTS_PER_GROUP - 3 - lo) * 0.5 + hi - 1.0
    class_lane = lax.broadcasted_iota(I32, (t, CLASS_LANES), 1).astype(F32)
    onehot = jnp.where(class_lane == cls, 1.0, 0.0)
    tri = jnp.where(lax.broadcasted_iota(I32, (t, t), 1) < lax.broadcasted_iota(I32, (t, t), 0), 1.0, 0.0).astype(BF16)
    before = jnp.dot(tri, onehot.astype(BF16), preferred_element_type=F32) + base_ref[0:1, :]
    rank = jnp.sum(jnp.where(class_lane == cls, before, 0.0), axis=1, keepdims=True)
    total = base_ref[0:1, :] + jnp.sum(onehot, axis=0, keepdims=True)
    base_ref[...] = jnp.broadcast_to(total, base_ref.shape)
    count_out[...] = jnp.broadcast_to(total, count_out.shape)

    logit_lane = lane_f + (ROUTER_EXPERT_LANE0 - REC_WEIGHT_LANE0) + EXPERTS_PER_GROUP * g_sel
    rec = jnp.where(logit_lane == i1, w1, jnp.where(logit_lane == i2, w2, 0.0))
    rec = jnp.where((lane >= REC_WEIGHT_LANE0) & (lane < REC_WEIGHT_LANE0 + EXPERTS_PER_GROUP), rec, 0.0)
    rec = jnp.where(lane == REC_CLASS_LANE, cls, rec)
    rec = jnp.where(lane == REC_RANK_LANE, rank, rec)
    h_out[:, D_MODEL:] = rec


def _post(alpha, x2, att, mq, mk, mv, gates, kx, vx, w_gates, w_att, w_ml, w_mix, w_xq, w_xo, w_router, b_router,
          vecs, batch, seq, mem_len):
    n = x2.shape[0]
    t = min(512, seq)
    s_tiles = seq // t
    row = lambda w: pl.BlockSpec((t, w), lambda b, s: (b * s_tiles + s, 0))
    memblk = pl.BlockSpec((mem_len, D_MODEL), lambda b, s: (b, 0))
    seq_blk = lambda w: pl.BlockSpec((1, t, w), lambda b, s: (b, s, 0))
    seqs = lambda a: a.reshape(batch, seq, a.shape[-1])
    w_hi = w_router.astype(BF16)
    w_lo = (w_router - w_hi.astype(F32)).astype(BF16)
    sq = _const_spec((D_MODEL, D_MODEL))
    return pl.pallas_call(
        functools.partial(_post_kernel, alpha),
        grid=(batch, s_tiles),
        in_specs=[row(D_MODEL), row(ATT_Q_W), seq_blk(ML_QK_W), seq_blk(ML_QK_W), seq_blk(ML_V_W), seq_blk(LANES),
                  memblk, memblk,
                  _const_spec((D_MODEL, 3 * D_MODEL)), sq, sq, sq, sq, sq,
                  _const_spec((D_MODEL, LANES)), _const_spec((D_MODEL, LANES)),
                  _const_spec((SUBLANES, D_MODEL)), _const_spec((1, LANES))],
        out_specs=[row(ROW_W), _const_spec((SUBLANES, CLASS_LANES))],
        out_shape=[jax.ShapeDtypeStruct((n, ROW_W), F32),
                   jax.ShapeDtypeStruct((SUBLANES, CLASS_LANES), F32)],
        scratch_shapes=[pltpu.VMEM((SUBLANES, CLASS_LANES), F32),
                        pltpu.VMEM((ML_HEADS, ML_DQK, ML_DV), F32),
                        pltpu.VMEM((ML_HEADS, SUBLANES, ML_DQK), F32),
                        pltpu.VMEM((1, SUBLANES, LANES), F32),
                        pltpu.VMEM((1, t, ML_V_W), BF16)],
        compiler_params=_params("arbitrary", "arbitrary"),
        name="post",
    )(x2, att, seqs(mq), seqs(mk), seqs(mv), seqs(gates), kx, vx, w_gates, w_att, w_ml, w_mix, w_xq, w_xo, w_hi, w_lo, vecs, b_router)


DMA_UNROLL = 32


def _scatter_kernel(dest_ref, padend_ref, h_ref, xs_out, zero_buf, sem, zero_sem):
    i = pl.program_id(0)
    ts = h_ref.shape[0]

    @pl.when(i == 0)
    def _():
        zero_buf[...] = jnp.zeros_like(zero_buf)

        def clear(start):
            start = pl.multiple_of(start, GROUP_BLOCK)
            return pltpu.make_async_copy(zero_buf, xs_out.at[pl.ds(start, GROUP_BLOCK), :], zero_sem)

        def nonempty(g):
            return padend_ref[g] > (padend_ref[g - 1] if g else 0)

        used = padend_ref[N_GROUPS - 1]

        def unused(b):
            return used + b * GROUP_BLOCK < xs_out.shape[0]

        for phase in ("start", "wait"):
            for g in range(N_GROUPS):
                @pl.when(nonempty(g))
                def _():
                    getattr(clear(padend_ref[g] - GROUP_BLOCK), phase)()

                @pl.when(unused(g))
                def _():
                    getattr(clear(used + g * GROUP_BLOCK), phase)()

    def issue(j, carry):
        for u in range(DMA_UNROLL):
            t = j * DMA_UNROLL + u
            pltpu.make_async_copy(h_ref.at[pl.ds(t, 1), :], xs_out.at[pl.ds(dest_ref[i * ts + t], 1), :], sem).start()
        return carry

    lax.fori_loop(0, ts // DMA_UNROLL, issue, 0)

    pltpu.make_async_copy(h_ref, xs_out.at[pl.ds(0, ts), :], sem).wait()


def _moe_scatter(dest, pad_end, h2aug, rows):
    n = h2aug.shape[0]
    ts = min(1024, n)
    grid_spec = pltpu.PrefetchScalarGridSpec(
        num_scalar_prefetch=2,
        grid=(n // ts,),
        in_specs=[pl.BlockSpec((ts, ROW_W), lambda i, *_: (i, 0))],
        out_specs=pl.BlockSpec(memory_space=pl.ANY),
        scratch_shapes=[pltpu.VMEM((GROUP_BLOCK, ROW_W), F32),
                        pltpu.SemaphoreType.DMA(()), pltpu.SemaphoreType.DMA(())],
    )
    return pl.pallas_call(
        _scatter_kernel,
        grid_spec=grid_spec,
        out_shape=jax.ShapeDtypeStruct((rows, ROW_W), F32),
        compiler_params=_params("arbitrary"),
        name="moe_scatter",
    )(dest, pad_end, h2aug)


def _group_kernel(bgrp_ref, nused_ref, bmask_ref, x_ref, wgu_ref, wd_ref, y_ref, xb_ref):
    del bgrp_ref
    i = pl.program_id(0)
    y_ref[...] = jnp.zeros_like(y_ref)

    @pl.when(i < nused_ref[0])
    def _():
        xb_ref[...] = x_ref[:, :D_MODEL].astype(BF16)
        mask = bmask_ref[i]
        for e in range(EXPERTS_PER_GROUP):
            @pl.when(((mask >> e) & 1) == 1)
            def _():
                gu = jnp.dot(xb_ref[...], wgu_ref[0, e], preferred_element_type=F32)
                gate = gu[:, :D_EXPERT]
                lane = D_MODEL + REC_WEIGHT_LANE0 + e
                w = x_ref[:, lane:lane + 1]
                hw = jnp.where(w != 0.0, w * (gate * jax.nn.sigmoid(gate) * gu[:, D_EXPERT:]), 0.0)
                y_ref[...] += jnp.dot(hw.astype(BF16), wd_ref[0, e * D_EXPERT:(e + 1) * D_EXPERT, :],
                                      preferred_element_type=F32)


def _moe_groups(block_group, n_used, block_mask, xs, w_gu, w_down):
    rows = xs.shape[0]
    e = EXPERTS_PER_GROUP
    grid_spec = pltpu.PrefetchScalarGridSpec(
        num_scalar_prefetch=3,
        grid=(rows // GROUP_BLOCK,),
        in_specs=[pl.BlockSpec((GROUP_BLOCK, ROW_W), lambda i, *_: (i, 0)),
                  pl.BlockSpec((1, e, D_MODEL, 2 * D_EXPERT), lambda i, bg, *_: (bg[i], 0, 0, 0)),
                  pl.BlockSpec((1, e * D_EXPERT, D_MODEL), lambda i, bg, *_: (bg[i], 0, 0))],
        out_specs=pl.BlockSpec((GROUP_BLOCK, D_MODEL), lambda i, *_: (i, 0)),
        scratch_shapes=[pltpu.VMEM((GROUP_BLOCK, D_MODEL), BF16)],
    )
    return pl.pallas_call(
        _group_kernel,
        grid_spec=grid_spec,
        out_shape=jax.ShapeDtypeStruct((rows, D_MODEL), F32),
        compiler_params=_params("arbitrary"),
        name="moe_groups",
    )(block_group, n_used, block_mask, xs, w_gu, w_down)


def _combine_kernel(alpha, dest_ref, h_ref, vec_ref, y_hbm, o_ref, ybuf, sems):
    tc = h_ref.shape[0]
    i = pl.program_id(0)
    steps = pl.num_programs(0)

    def gather(tile, slot):
        def issue(j, carry):
            for u in range(DMA_UNROLL):
                t = j * DMA_UNROLL + u
                pltpu.make_async_copy(y_hbm.at[pl.ds(dest_ref[tile * tc + t], 1), :],
                                      ybuf.at[slot, pl.ds(t, 1), :], sems.at[slot]).start()
            return carry
        lax.fori_loop(0, tc // DMA_UNROLL, issue, 0)

    @pl.when(i == 0)
    def _():
        gather(0, 0)

    @pl.when(i + 1 < steps)
    def _():
        gather(i + 1, (i + 1) % 2)

    slot = i % 2
    pltpu.make_async_copy(y_hbm.at[pl.ds(0, tc), :], ybuf.at[slot], sems.at[slot]).wait()
    o_ref[...] = _layer_norm(alpha * h_ref[...] + ybuf[slot], vec_ref[0:1, :], vec_ref[1:2, :])


def _moe_combine(alpha, dest, h2aug, vecs, y):
    n = h2aug.shape[0]
    tc = min(512, n)
    grid_spec = pltpu.PrefetchScalarGridSpec(
        num_scalar_prefetch=1,
        grid=(n // tc,),
        in_specs=[pl.BlockSpec((tc, D_MODEL), lambda i, *_: (i, 0)),
                  pl.BlockSpec((SUBLANES, D_MODEL), lambda i, *_: (0, 0)),
                  pl.BlockSpec(memory_space=pl.ANY)],
        out_specs=pl.BlockSpec((tc, D_MODEL), lambda i, *_: (i, 0)),
        scratch_shapes=[pltpu.VMEM((2, tc, D_MODEL), F32), pltpu.SemaphoreType.DMA((2,))],
    )
    return pl.pallas_call(
        functools.partial(_combine_kernel, alpha),
        grid_spec=grid_spec,
        out_shape=jax.ShapeDtypeStruct((n, D_MODEL), F32),
        compiler_params=_params("arbitrary"),
        name="moe_combine",
    )(dest, h2aug, vecs, y)


def _pad_rows(vectors):
    rows = [v.astype(F32)[None, :] for v in vectors]
    rows.append(jnp.zeros((SUBLANES - len(rows), vectors[0].shape[0]), F32))
    return jnp.concatenate(rows, axis=0)


def _layer(alpha, h, mem2, pos2, batch, seq, mem_len, w_in, attn_sinks, conv_w, conv_b, b_igate, b_fgate,
           ml_norm_g, w_att_branch, w_ml_branch, w_mix_out, ln1_g, ln1_b, w_xq, w_xkv, w_xo, ln2_g, ln2_b,
           w_router_group, b_router_group, w_router_expert, b_router_expert, w_gate, w_up, w_down,
           ln3_g, ln3_b):
    n = h.shape[0]
    att, mq, mk, mv, gates = _in_proj(h, pos2, w_in, attn_sinks, conv_w, conv_b, b_igate, b_fgate, batch, seq)
    kx, vx = _mem_kv(mem2, w_xkv, batch, mem_len)

    o_mo = ATT_Q_W + 2 * ATT_KV_W + 2 * ML_QK_W + ML_V_W
    o_ga = o_mo + ML_V_W + 2 * ML_HEADS
    w_gates = jnp.concatenate([w_in[:, o_mo:o_mo + ML_V_W], w_in[:, o_ga:]], axis=1).astype(BF16)
    pad = LANES - N_GROUPS - N_EXPERTS
    w_router = jnp.pad(jnp.concatenate([w_router_group, w_router_expert], axis=1).astype(F32), ((0, 0), (0, pad)))
    b_router = jnp.pad(jnp.concatenate([b_router_group, b_router_expert]).astype(F32), (0, pad))[None, :]
    vecs = _pad_rows([ml_norm_g, ln1_g, ln1_b, ln2_g, ln2_b])
    h2aug, counts = _post(alpha, h, att, mq, mk, mv, gates, kx, vx, w_gates, w_att_branch.astype(BF16),
                          w_ml_branch.astype(BF16), w_mix_out.astype(BF16), w_xq.astype(BF16),
                          w_xo.astype(BF16), w_router, b_router, vecs, batch, seq, mem_len)

    e = EXPERTS_PER_GROUP
    class_count = counts[0, :N_CLASSES].astype(I32)
    group_count = jnp.sum(class_count.reshape(N_GROUPS, PAIRS_PER_GROUP), axis=1)
    padded = ((group_count + GROUP_BLOCK - 1) // GROUP_BLOCK) * GROUP_BLOCK
    pad_end = jnp.cumsum(padded).astype(I32)
    pad_start = pad_end - padded
    in_group = jnp.cumsum(class_count.reshape(N_GROUPS, PAIRS_PER_GROUP), axis=1) - class_count.reshape(N_GROUPS, -1)
    class_start = (pad_start[:, None] + in_group).reshape(N_CLASSES)
    class_end = class_start + class_count
    n_blocks = n // GROUP_BLOCK + N_GROUPS
    block_start = jnp.arange(n_blocks, dtype=I32) * GROUP_BLOCK
    block_group = jnp.sum((pad_end[None, :] <= block_start[:, None]).astype(I32), axis=1)
    block_group = jnp.minimum(block_group, N_GROUPS - 1)
    n_used = pad_end[-1:] // GROUP_BLOCK
    pairs = [(lo, hi) for lo in range(e) for hi in range(lo + 1, e)]
    uses = jnp.array([[int(x in p) for x in range(e)] for p in pairs] * N_GROUPS, I32)
    overlap = ((class_start[None, :] < block_start[:, None] + GROUP_BLOCK) & (class_end[None, :] > block_start[:, None])
               & (class_count[None, :] > 0)).astype(I32)
    block_uses = jnp.sum(overlap[:, :, None] * uses[None, :, :], axis=1)
    block_mask = jnp.sum((block_uses > 0).astype(I32) << jnp.arange(e, dtype=I32)[None, :], axis=1)
    cls = h2aug[:, D_MODEL + REC_CLASS_LANE].astype(I32)
    rank = h2aug[:, D_MODEL + REC_RANK_LANE].astype(I32)
    dest = class_start[cls] + rank

    xs = _moe_scatter(dest, pad_end, h2aug, n_blocks * GROUP_BLOCK)
    w_gu = jnp.concatenate([w_gate, w_up], axis=2).astype(BF16).reshape(N_GROUPS, e, D_MODEL, 2 * D_EXPERT)
    w_dn = w_down.astype(BF16).reshape(N_GROUPS, e * D_EXPERT, D_MODEL)
    y = _moe_groups(block_group, n_used, block_mask, xs, w_gu, w_dn)
    return _moe_combine(alpha, dest, h2aug, _pad_rows([ln3_g, ln3_b]), y)


def kernel(x, mem, positions, w_in, attn_sinks, conv_w, conv_b, b_igate, b_fgate, ml_norm_g, w_att_branch, w_ml_branch, w_mix_out, ln1_g, ln1_b, w_xq, w_xkv, w_xo, ln2_g, ln2_b, w_router_group, b_router_group, w_router_expert, b_router_expert, w_gate, w_up, w_down, ln3_g, ln3_b):
    batch, seq, d = x.shape
    mem_len = mem.shape[1]
    depth = w_in.shape[0]
    alpha = (2 * depth) ** 0.25
    h = x.reshape(batch * seq, d)
    mem2 = mem.reshape(batch * mem_len, d)
    pos2 = positions.reshape(batch * seq, 1)
    stacked = (w_in, attn_sinks, conv_w, conv_b, b_igate, b_fgate, ml_norm_g, w_att_branch, w_ml_branch,
               w_mix_out, ln1_g, ln1_b, w_xq, w_xkv, w_xo, ln2_g, ln2_b, w_router_group, b_router_group,
               w_router_expert, b_router_expert, w_gate, w_up, w_down, ln3_g, ln3_b)
    for l in range(depth):
        h = _layer(alpha, h, mem2, pos2, batch, seq, mem_len, *(w[l] for w in stacked))
    return h.reshape(batch, seq, d)
```

```python
import functools

import jax
import jax.numpy as jnp
from jax import lax
from jax.experimental import pallas as pl
from jax.experimental.pallas import tpu as pltpu

F32 = jnp.float32
BF16 = jnp.bfloat16
I32 = jnp.int32

D_MODEL = 1024
ATT_HEADS = 16
ATT_KV_HEADS = 2
ATT_HEAD_DIM = 64
ATT_BLOCK = 128
ROPE_THETA = 10000.0
ML_HEADS = 4
ML_DQK = 128
ML_DV = 256
ML_CHUNK = 128
CONV_WIDTH = 4
X_HEADS = 4
X_HEAD_DIM = D_MODEL // X_HEADS
N_GROUPS = 8
EXPERTS_PER_GROUP = 8
N_EXPERTS = N_GROUPS * EXPERTS_PER_GROUP
TOP_K = 2
D_EXPERT = 256
MOE_BLOCK = 128
LN_EPS = 1e-5
RMS_EPS = 1e-6

ATT_Q_W = ATT_HEADS * ATT_HEAD_DIM
ATT_KV_W = ATT_KV_HEADS * ATT_HEAD_DIM
ML_QK_W = ML_HEADS * ML_DQK
ML_V_W = ML_HEADS * ML_DV

LANES = 128
SUBLANES = 8
VMEM_LIMIT_BYTES = 56 * 1024 * 1024

ROUTER_EXPERT_LANE0 = N_GROUPS
PAIRS_PER_GROUP = EXPERTS_PER_GROUP * (EXPERTS_PER_GROUP - 1) // 2
N_CLASSES = N_GROUPS * PAIRS_PER_GROUP
CLASS_LANES = 256
REC_CLASS_LANE = 0
REC_RANK_LANE = 1
REC_WEIGHT_LANE0 = 8
ROW_W = D_MODEL + LANES
GROUP_BLOCK = 512
LOG2_E = 1.4426950408889634
POST_COLS = 512


def _params(*semantics):
    return pltpu.CompilerParams(dimension_semantics=semantics, vmem_limit_bytes=VMEM_LIMIT_BYTES)


def _const_spec(shape):
    zeros = (0,) * len(shape)
    return pl.BlockSpec(shape, lambda *_: zeros, pipeline_mode=pl.Buffered(1))


def _layer_norm(z, g, b):
    mu = jnp.mean(z, axis=-1, keepdims=True)
    zc = z - mu
    var = jnp.mean(zc * zc, axis=-1, keepdims=True)
    return zc * lax.rsqrt(var + LN_EPS) * g + b


def _inproj_kernel(sink_ref, x_ref, pos_ref, inv_ref, wrope_ref, wv_ref, wmqk_ref, wmv_ref, wg_ref,
                   convw_ref, convb_ref, gbias_ref,
                   att_out, mq_out, mk_out, mv_out, g_out, prev_ref, kvprev_ref):
    t = x_ref.shape[0]
    first = pl.program_id(1) == 0
    xb = x_ref[...].astype(BF16)

    half = ATT_HEAD_DIM // 2
    ang = pos_ref[...].astype(F32) * inv_ref[0:1, :]
    sin_cos = jnp.sin(ang + inv_ref[1:2, :])
    lane = lax.broadcasted_iota(I32, (t, LANES), 1)
    first_half = (lane % ATT_HEAD_DIM) < half
    cos = jnp.where(first_half, pltpu.roll(sin_cos, LANES - half, 1), sin_cos)
    sin_signed = jnp.where(first_half, -sin_cos, pltpu.roll(sin_cos, half, 1))

    def rope(z):
        partner = jnp.where(first_half, pltpu.roll(z, LANES - ATT_HEAD_DIM // 2, 1),
                            pltpu.roll(z, ATT_HEAD_DIM // 2, 1))
        return z * cos + partner * sin_signed

    qk = jnp.dot(xb, wrope_ref[...], preferred_element_type=F32)
    q_scale = ATT_HEAD_DIM ** -0.5 * LOG2_E
    q_groups = [(rope(qk[:, g * LANES:(g + 1) * LANES]) * q_scale).astype(BF16) for g in range(ATT_Q_W // LANES)]
    k_groups = [rope(qk[:, ATT_Q_W + g * LANES:ATT_Q_W + (g + 1) * LANES]).astype(BF16)
                for g in range(ATT_KV_HEADS)]
    v_all = jnp.dot(xb, wv_ref[...], preferred_element_type=F32).astype(BF16)
    v_groups = [v_all[:, g * LANES:(g + 1) * LANES] for g in range(ATT_KV_HEADS)]

    nk = 2 * ATT_BLOCK
    kc = lax.broadcasted_iota(I32, (nk, ATT_BLOCK), 0)
    qi = lax.broadcasted_iota(I32, (nk, ATT_BLOCK), 1)
    band = (kc > qi) & (kc <= qi + ATT_BLOCK)
    lo = lax.broadcasted_iota(I32, (nk, LANES), 1) < ATT_HEAD_DIM
    dim_lo = lax.broadcasted_iota(I32, (LANES, ATT_BLOCK), 0) < ATT_HEAD_DIM
    pairs = ATT_HEADS // ATT_KV_HEADS // 2

    @pl.when(first)
    def _():
        kvprev_ref[...] = jnp.zeros_like(kvprev_ref)

    for j in range(t // ATT_BLOCK):
        cur = slice(j * ATT_BLOCK, (j + 1) * ATT_BLOCK)
        valid = band & (kc >= jnp.where(first, ATT_BLOCK, 0)) if j == 0 else band
        for kvh in range(ATT_KV_HEADS):
            if j == 0:
                k_prev = kvprev_ref[:, kvh * LANES:(kvh + 1) * LANES]
                v_prev = kvprev_ref[:, (ATT_KV_HEADS + kvh) * LANES:(ATT_KV_HEADS + kvh + 1) * LANES]
            else:
                prev = slice((j - 1) * ATT_BLOCK, j * ATT_BLOCK)
                k_prev, v_prev = k_groups[kvh][prev], v_groups[kvh][prev]
            kk = jnp.concatenate([k_prev, k_groups[kvh][cur]], axis=0)
            vv = jnp.concatenate([v_prev, v_groups[kvh][cur]], axis=0).astype(F32)
            zero = jnp.zeros_like(kk)
            k2 = jnp.concatenate([jnp.where(lo, kk, zero), jnp.where(lo, zero, kk)], axis=0)
            v2t = jnp.concatenate([jnp.where(lo, vv, 0.0), jnp.where(lo, 0.0, vv)], axis=0).T.astype(BF16)
            for pp in range(pairs):
                p = kvh * pairs + pp
                qp = q_groups[p][cur]
                st = lax.dot_general(k2, qp, (((1,), (1,)), ((), ())), preferred_element_type=F32)
                e_heads, recip = [], []
                for hh in range(2):
                    sink = sink_ref[2 * p + hh]
                    s_h = jnp.where(valid, st[hh * nk:(hh + 1) * nk], -jnp.inf)
                    m = jnp.maximum(jnp.max(s_h, axis=0, keepdims=True), sink)
                    e = jnp.exp2(s_h - m)
                    den = jnp.sum(e, axis=0, keepdims=True) + jnp.exp2(sink - m)
                    e_heads.append(e.astype(BF16))
                    recip.append(1.0 / den)
                out_t = jnp.dot(v2t, jnp.concatenate(e_heads, axis=0), preferred_element_type=F32)
                out_t = out_t * jnp.where(dim_lo, recip[0], recip[1])
                att_out[cur, p * LANES:(p + 1) * LANES] = out_t.T.astype(BF16)
    last = slice(t - ATT_BLOCK, t)
    kvprev_ref[...] = jnp.concatenate([g[last] for g in k_groups] + [g[last] for g in v_groups], axis=1)

    @pl.when(pl.program_id(1) == 0)
    def _():
        prev_ref[...] = jnp.zeros_like(prev_ref)

    row8 = lax.broadcasted_iota(I32, (SUBLANES, ML_QK_W), 0)
    for out_ref, scale, cs in ((mq_out, 1.0, slice(0, ML_QK_W)), (mk_out, ML_DQK ** -0.5, slice(ML_QK_W, 2 * ML_QK_W))):
        pre = jnp.dot(xb, wmqk_ref[:, cs], preferred_element_type=F32)
        prev8 = prev_ref[:, cs]
        w_now = convw_ref[CONV_WIDTH - 1:CONV_WIDTH, cs]
        acc = pre * w_now + convb_ref[:, cs]
        top = pre[0:SUBLANES] * w_now + convb_ref[:, cs]
        for d in range(1, CONV_WIDTH):
            w_d = convw_ref[CONV_WIDTH - 1 - d:CONV_WIDTH - d, cs]
            shifted = pltpu.roll(pre, d, 0)
            acc = acc + shifted * w_d
            top = top + jnp.where(row8 < d, pltpu.roll(prev8, d, 0), shifted[0:SUBLANES]) * w_d
        prev_ref[:, cs] = pre[t - SUBLANES:t]
        conv = jnp.concatenate([top, acc[SUBLANES:]], axis=0)
        out_ref[...] = (conv * jax.nn.sigmoid(conv) * scale).astype(BF16)

    mv_out[...] = jnp.dot(xb, wmv_ref[...], preferred_element_type=F32).astype(BF16)

    gates = jnp.dot(xb, wg_ref[...], preferred_element_type=F32) + gbias_ref[...]
    g_out[...] = jnp.where(lane < ML_HEADS, gates, jax.nn.log_sigmoid(gates))


def _in_proj(x2, pos2, w_in, sinks, conv_w, conv_b, b_igate, b_fgate, batch, seq):
    n = x2.shape[0]
    t = min(1024, seq)
    s_tiles = seq // t

    def dup(w):
        h = ATT_HEAD_DIM
        return jnp.concatenate([w[:, :h], w[:, :h], w[:, h:], w[:, h:]], axis=1)

    o_k = ATT_Q_W
    o_v = o_k + ATT_KV_W
    o_mq = o_v + ATT_KV_W
    o_mv = o_mq + 2 * ML_QK_W
    o_mo = o_mv + ML_V_W
    o_mi = o_mo + ML_V_W
    w_rope = jnp.concatenate([w_in[:, :o_k], dup(w_in[:, o_k:o_v])], axis=1).astype(BF16)
    w_v = dup(w_in[:, o_v:o_mq]).astype(BF16)
    w_mqk = w_in[:, o_mq:o_mv].astype(BF16)
    w_mv = w_in[:, o_mv:o_mo].astype(BF16)
    w_g = jnp.pad(w_in[:, o_mi:o_mi + 2 * ML_HEADS], ((0, 0), (0, LANES - 2 * ML_HEADS))).astype(BF16)
    gbias = jnp.pad(jnp.concatenate([b_igate, b_fgate]).astype(F32), (0, LANES - 2 * ML_HEADS))[None, :]
    half = ATT_HEAD_DIM // 2
    inv = ROPE_THETA ** (-jnp.arange(half, dtype=F32) / half)
    quarter_turn = jnp.where((jnp.arange(LANES) % ATT_HEAD_DIM) < half, 0.0, jnp.pi / 2).astype(F32)
    inv = jnp.stack([jnp.tile(inv, LANES // half), quarter_turn])

    row = lambda w: pl.BlockSpec((t, w), lambda b, s, *_: (b * s_tiles + s, 0))
    grid_spec = pltpu.PrefetchScalarGridSpec(
        num_scalar_prefetch=1,
        grid=(batch, s_tiles),
        in_specs=[row(D_MODEL), row(1), _const_spec((2, LANES)),
                  _const_spec(w_rope.shape), _const_spec(w_v.shape), _const_spec(w_mqk.shape),
                  _const_spec(w_mv.shape), _const_spec(w_g.shape),
                  _const_spec((CONV_WIDTH, 2 * ML_QK_W)), _const_spec((1, 2 * ML_QK_W)),
                  _const_spec((1, LANES))],
        out_specs=[row(ATT_Q_W), row(ML_QK_W), row(ML_QK_W), row(ML_V_W), row(LANES)],
        scratch_shapes=[pltpu.VMEM((SUBLANES, 2 * ML_QK_W), F32),
                        pltpu.VMEM((ATT_BLOCK, 4 * ATT_KV_W), BF16)],
    )
    return pl.pallas_call(
        _inproj_kernel,
        grid_spec=grid_spec,
        out_shape=[jax.ShapeDtypeStruct((n, ATT_Q_W), BF16),
                   jax.ShapeDtypeStruct((n, ML_QK_W), BF16),
                   jax.ShapeDtypeStruct((n, ML_QK_W), BF16),
                   jax.ShapeDtypeStruct((n, ML_V_W), BF16),
                   jax.ShapeDtypeStruct((n, LANES), F32)],
        compiler_params=_params("arbitrary", "arbitrary"),
        name="in_proj",
    )(sinks.astype(F32) * LOG2_E, x2, pos2, inv, w_rope, w_v, w_mqk, w_mv, w_g, conv_w.astype(F32),
      conv_b.astype(F32)[None, :], gbias)


def _mlstm_reset(c_ref, n_ref, m_ref):
    c_ref[...] = jnp.zeros_like(c_ref)
    n_ref[...] = jnp.zeros_like(n_ref)
    m_ref[...] = jnp.zeros_like(m_ref)


def _mlstm_chunk(rows, q_ref, k_ref, v_ref, g_ref, o_ref, c_ref, n_ref, m_ref):
    nb = q_ref.shape[0]
    L = ML_CHUNK
    causal = lax.broadcasted_iota(I32, (L, L), 1) <= lax.broadcasted_iota(I32, (L, L), 0)
    rowi = lax.broadcasted_iota(I32, (L, LANES), 0)

    def scan_rows(x, op, identity):
        d = 1
        while d < L:
            x = op(x, jnp.where(rowi >= d, pltpu.roll(x, d, 0), identity))
            d *= 2
        return x

    def chunk():
        heads = [(bb, h) for bb in range(nb) for h in range(ML_HEADS)]

        gate = []
        for bb in range(nb):
            g = g_ref[bb, rows, :]
            b = pltpu.roll(scan_rows(g, jnp.add, 0.0), LANES - ML_HEADS, 1)
            a = g - b
            a_max = scan_rows(a, jnp.maximum, -jnp.inf)
            m_prev = m_ref[bb, 0:1, :]
            m_t = b + jnp.maximum(m_prev, a_max)
            u = b - m_t
            b_end = b[L - 1:L, :]
            m_new = b_end + jnp.maximum(m_prev, a_max[L - 1:L, :])
            m_ref[bb, 0:1, :] = m_new
            gate.append(dict(u=u, w_inter=jnp.exp(u + m_prev), floor=jnp.exp(-m_t),
                             w_state=jnp.exp(b_end + m_prev - m_new), w_k=jnp.exp(b_end + a - m_new),
                             a_t=a.T))

        nt = (((1,), (1,)), ((), ()))
        for bb, h in heads:
            gt = gate[bb]
            idx = bb * ML_HEADS + h
            col = slice(h, h + 1)
            qh = q_ref[bb, rows, h * ML_DQK:(h + 1) * ML_DQK]
            kh = k_ref[bb, rows, h * ML_DQK:(h + 1) * ML_DQK]
            vh = v_ref[bb, rows, h * ML_DV:(h + 1) * ML_DV]
            c_prev = c_ref[idx]
            n_prev = n_ref[idx, 0:1, :]

            w_intra = jnp.exp(jnp.where(causal, gt["u"][:, col] + gt["a_t"][col, :], -jnp.inf))
            s = lax.dot_general(qh, kh, nt, preferred_element_type=F32) * w_intra
            w_inter = gt["w_inter"][:, col]
            num = (w_inter * jnp.dot(qh, c_prev.astype(BF16), preferred_element_type=F32)
                   + jnp.dot(s.astype(BF16), vh, preferred_element_type=F32))
            den = (w_inter * jnp.sum(qh.astype(F32) * n_prev, axis=1, keepdims=True)
                   + jnp.sum(s, axis=1, keepdims=True))
            hval = num / jnp.maximum(jnp.abs(den), gt["floor"][:, col])
            hval = hval * lax.rsqrt(jnp.mean(hval * hval, axis=1, keepdims=True) + RMS_EPS)
            o_ref[bb, rows, h * ML_DV:(h + 1) * ML_DV] = hval.astype(BF16)

            w_state = gt["w_state"][:, col]
            kw = kh.astype(F32) * gt["w_k"][:, col]
            c_ref[idx] = w_state * c_prev + jnp.dot(kw.T.astype(BF16), vh, preferred_element_type=F32)
            n_ref[idx, 0:1, :] = w_state * n_prev + jnp.sum(kw, axis=0, keepdims=True)

    chunk()


def _memkv_kernel(mem_ref, w_ref, k_out, v_out):
    kv = jnp.dot(mem_ref[...].astype(BF16), w_ref[...], preferred_element_type=F32)
    k_out[...] = kv[:, :D_MODEL].astype(BF16)
    v_out[...] = kv[:, D_MODEL:].astype(BF16)


def _mem_kv(mem2, w_xkv, batch, mem_len):
    blk = pl.BlockSpec((mem_len, D_MODEL), lambda b: (b, 0))
    return pl.pallas_call(
        _memkv_kernel,
        grid=(batch,),
        in_specs=[blk, _const_spec((D_MODEL, 2 * D_MODEL))],
        out_specs=[blk, blk],
        out_shape=[jax.ShapeDtypeStruct((batch * mem_len, D_MODEL), BF16)] * 2,
        compiler_params=_params("arbitrary"),
        name="mem_kv",
    )(mem2, w_xkv.astype(BF16))


def _post_kernel(alpha, x_ref, att_ref, mq_ref, mk_ref, mv_ref, g_ref, kx_ref, vx_ref, wgate_ref, watt_ref,
                 wml_ref, wmix_ref, wxq_ref, wxo_ref, wr_hi_ref, wr_lo_ref, vec_ref, rbias_ref,
                 h_out, count_out, base_ref, c_ref, n_ref, m_ref, hm_ref):
    t = x_ref.shape[0]
    first = (pl.program_id(0) == 0) & (pl.program_id(1) == 0)

    @pl.when(first)
    def _():
        base_ref[...] = jnp.zeros_like(base_ref)

    @pl.when(pl.program_id(1) == 0)
    def _():
        _mlstm_reset(c_ref, n_ref, m_ref)

    for c in range(t // ML_CHUNK):
        _mlstm_chunk(slice(c * ML_CHUNK, (c + 1) * ML_CHUNK), mq_ref, mk_ref, mv_ref, g_ref, hm_ref,
                     c_ref, n_ref, m_ref)

    x = x_ref[...]
    xb = x.astype(BF16)
    norm_g, ln1_g, ln1_b, ln2_g, ln2_b = (vec_ref[i:i + 1, :] for i in range(5))

    chunks = [slice(c, c + POST_COLS) for c in range(0, D_MODEL, POST_COLS)]
    hm = jnp.concatenate(
        [(hm_ref[0, :, cs].astype(F32) * norm_g[:, cs]
          * jax.nn.sigmoid(jnp.dot(xb, wgate_ref[:, cs], preferred_element_type=F32))).astype(BF16)
         for cs in chunks], axis=1)
    att = att_ref[...]
    y = []
    for cs in chunks:
        gm = jnp.dot(xb, wgate_ref[:, 2 * D_MODEL + cs.start:2 * D_MODEL + cs.stop], preferred_element_type=F32)
        ga = jnp.dot(xb, wgate_ref[:, D_MODEL + cs.start:D_MODEL + cs.stop], preferred_element_type=F32)
        m_out = jnp.dot(hm, wml_ref[:, cs], preferred_element_type=F32)
        a_out = jnp.dot(att, watt_ref[:, cs], preferred_element_type=F32)
        y.append((jax.nn.sigmoid(gm) * m_out + jax.nn.sigmoid(ga) * a_out).astype(BF16))
    mix = jnp.dot(jnp.concatenate(y, axis=1), wmix_ref[...], preferred_element_type=F32)
    h1 = _layer_norm(alpha * x + mix, ln1_g, ln1_b)

    q = (jnp.dot(h1.astype(BF16), wxq_ref[...], preferred_element_type=F32) * (X_HEAD_DIM ** -0.5)).astype(BF16)
    heads = []
    for h in range(X_HEADS):
        sl = slice(h * X_HEAD_DIM, (h + 1) * X_HEAD_DIM)
        sc = lax.dot_general(q[:, sl], kx_ref[:, sl], (((1,), (1,)), ((), ())), preferred_element_type=F32)
        e = jnp.exp(sc - jnp.max(sc, axis=1, keepdims=True))
        probs = (e * (1.0 / jnp.sum(e, axis=1, keepdims=True))).astype(BF16)
        heads.append(jnp.dot(probs, vx_ref[:, sl], preferred_element_type=F32).astype(BF16))
    xa = jnp.dot(jnp.concatenate(heads, axis=1), wxo_ref[...], preferred_element_type=F32)
    h2 = _layer_norm(alpha * h1 + xa, ln2_g, ln2_b)
    h_out[:, :D_MODEL] = h2

    h_hi = h2.astype(BF16)
    h_lo = (h2 - h_hi.astype(F32)).astype(BF16)
    logits = (jnp.dot(h_hi, wr_hi_ref[...], preferred_element_type=F32)
              + jnp.dot(h_lo, wr_hi_ref[...], preferred_element_type=F32)
              + jnp.dot(h_hi, wr_lo_ref[...], preferred_element_type=F32)) + rbias_ref[...]

    lane = lax.broadcasted_iota(I32, (t, LANES), 1)
    neg = -jnp.inf
    is_group = lane < N_GROUPS
    gl = jnp.where(is_group, logits, neg)
    g_max = jnp.max(gl, axis=1, keepdims=True)
    lane_f = lane.astype(F32)
    no_lane = float(LANES)
    g_sel = jnp.min(jnp.where(gl == g_max, lane_f, no_lane), axis=1, keepdims=True)
    p_group = 1.0 / jnp.sum(jnp.where(is_group, jnp.exp(logits - g_max), 0.0), axis=1, keepdims=True)
    lane_group = ((lane - ROUTER_EXPERT_LANE0) >> 3).astype(F32)
    in_group = (lane >= ROUTER_EXPERT_LANE0) & (lane_group == g_sel)
    el = jnp.where(in_group, logits, neg)
    v1 = jnp.max(el, axis=1, keepdims=True)
    i1 = jnp.min(jnp.where(el == v1, lane_f, no_lane), axis=1, keepdims=True)
    el2 = jnp.where(lane_f == i1, neg, el)
    v2 = jnp.max(el2, axis=1, keepdims=True)
    i2 = jnp.min(jnp.where(el2 == v2, lane_f, no_lane), axis=1, keepdims=True)
    e2 = jnp.exp(v2 - v1)
    w1 = p_group * (1.0 / (1.0 + e2))
    w2 = p_group * (e2 / (1.0 + e2))

    first_logit = ROUTER_EXPERT_LANE0 + EXPERTS_PER_GROUP * g_sel
    lo = jnp.minimum(i1, i2) - first_logit
    hi = jnp.maximum(i1, i2) - first_logit
    cls = PAIRS_PER_GROUP * g_sel + lo * (2 * EXPERTS_PER_GROUP - 3 - lo) * 0.5 + hi - 1.0
    class_lane = lax.broadcasted_iota(I32, (t, CLASS_LANES), 1).astype(F32)
    onehot = jnp.where(class_lane == cls, 1.0, 0.0)
    tri = jnp.where(lax.broadcasted_iota(I32, (t, t), 1) < lax.broadcasted_iota(I32, (t, t), 0), 1.0, 0.0).astype(BF16)
    before = jnp.dot(tri, onehot.astype(BF16), preferred_element_type=F32) + base_ref[0:1, :]
    rank = jnp.sum(jnp.where(class_lane == cls, before, 0.0), axis=1, keepdims=True)
    total = base_ref[0:1, :] + jnp.sum(onehot, axis=0, keepdims=True)
    base_ref[...] = jnp.broadcast_to(total, base_ref.shape)
    count_out[...] = jnp.broadcast_to(total, count_out.shape)

    logit_lane = lane_f + (ROUTER_EXPERT_LANE0 - REC_WEIGHT_LANE0) + EXPERTS_PER_GROUP * g_sel
    rec = jnp.where(logit_lane == i1, w1, jnp.where(logit_lane == i2, w2, 0.0))
    rec = jnp.where((lane >= REC_WEIGHT_LANE0) & (lane < REC_WEIGHT_LANE0 + EXPERTS_PER_GROUP), rec, 0.0)
    rec = jnp.where(lane == REC_CLASS_LANE, cls, rec)
    rec = jnp.where(lane == REC_RANK_LANE, rank, rec)
    h_out[:, D_MODEL:] = rec


def _post(alpha, x2, att, mq, mk, mv, gates, kx, vx, w_gates, w_att, w_ml, w_mix, w_xq, w_xo, w_router, b_router,
          vecs, batch, seq, mem_len):
    n = x2.shape[0]
    t = min(512, seq)
    s_tiles = seq // t
    row = lambda w: pl.BlockSpec((t, w), lambda b, s: (b * s_tiles + s, 0))
    memblk = pl.BlockSpec((mem_len, D_MODEL), lambda b, s: (b, 0))
    seq_blk = lambda w: pl.BlockSpec((1, t, w), lambda b, s: (b, s, 0))
    seqs = lambda a: a.reshape(batch, seq, a.shape[-1])
    w_hi = w_router.astype(BF16)
    w_lo = (w_router - w_hi.astype(F32)).astype(BF16)
    sq = _const_spec((D_MODEL, D_MODEL))
    return pl.pallas_call(
        functools.partial(_post_kernel, alpha),
        grid=(batch, s_tiles),
        in_specs=[row(D_MODEL), row(ATT_Q_W), seq_blk(ML_QK_W), seq_blk(ML_QK_W), seq_blk(ML_V_W), seq_blk(LANES),
                  memblk, memblk,
                  _const_spec((D_MODEL, 3 * D_MODEL)), sq, sq, sq, sq, sq,
                  _const_spec((D_MODEL, LANES)), _const_spec((D_MODEL, LANES)),
                  _const_spec((SUBLANES, D_MODEL)), _const_spec((1, LANES))],
        out_specs=[row(ROW_W), _const_spec((SUBLANES, CLASS_LANES))],
        out_shape=[jax.ShapeDtypeStruct((n, ROW_W), F32),
                   jax.ShapeDtypeStruct((SUBLANES, CLASS_LANES), F32)],
        scratch_shapes=[pltpu.VMEM((SUBLANES, CLASS_LANES), F32),
                        pltpu.VMEM((ML_HEADS, ML_DQK, ML_DV), F32),
                        pltpu.VMEM((ML_HEADS, SUBLANES, ML_DQK), F32),
                        pltpu.VMEM((1, SUBLANES, LANES), F32),
                        pltpu.VMEM((1, t, ML_V_W), BF16)],
        compiler_params=_params("arbitrary", "arbitrary"),
        name="post",
    )(x2, att, seqs(mq), seqs(mk), seqs(mv), seqs(gates), kx, vx, w_gates, w_att, w_ml, w_mix, w_xq, w_xo, w_hi, w_lo, vecs, b_router)


DMA_UNROLL = 32


def _scatter_kernel(dest_ref, padend_ref, h_ref, xs_out, zero_buf, sem, zero_sem):
    i = pl.program_id(0)
    ts = h_ref.shape[0]

    @pl.when(i == 0)
    def _():
        zero_buf[...] = jnp.zeros_like(zero_buf)

        def clear(start):
            start = pl.multiple_of(start, GROUP_BLOCK)
            return pltpu.make_async_copy(zero_buf, xs_out.at[pl.ds(start, GROUP_BLOCK), :], zero_sem)

        def nonempty(g):
            return padend_ref[g] > (padend_ref[g - 1] if g else 0)

        used = padend_ref[N_GROUPS - 1]

        def unused(b):
            return used + b * GROUP_BLOCK < xs_out.shape[0]

        for phase in ("start", "wait"):
            for g in range(N_GROUPS):
                @pl.when(nonempty(g))
                def _():
                    getattr(clear(padend_ref[g] - GROUP_BLOCK), phase)()

                @pl.when(unused(g))
                def _():
                    getattr(clear(used + g * GROUP_BLOCK), phase)()

    def issue(j, carry):
        for u in range(DMA_UNROLL):
            t = j * DMA_UNROLL + u
            pltpu.make_async_copy(h_ref.at[pl.ds(t, 1), :], xs_out.at[pl.ds(dest_ref[i * ts + t], 1), :], sem).start()
        return carry

    lax.fori_loop(0, ts // DMA_UNROLL, issue, 0)

    pltpu.make_async_copy(h_ref, xs_out.at[pl.ds(0, ts), :], sem).wait()


def _moe_scatter(dest, pad_end, h2aug, rows):
    n = h2aug.shape[0]
    ts = min(1024, n)
    grid_spec = pltpu.PrefetchScalarGridSpec(
        num_scalar_prefetch=2,
        grid=(n // ts,),
        in_specs=[pl.BlockSpec((ts, ROW_W), lambda i, *_: (i, 0))],
        out_specs=pl.BlockSpec(memory_space=pl.ANY),
        scratch_shapes=[pltpu.VMEM((GROUP_BLOCK, ROW_W), F32),
                        pltpu.SemaphoreType.DMA(()), pltpu.SemaphoreType.DMA(())],
    )
    return pl.pallas_call(
        _scatter_kernel,
        grid_spec=grid_spec,
        out_shape=jax.ShapeDtypeStruct((rows, ROW_W), F32),
        compiler_params=_params("arbitrary"),
        name="moe_scatter",
    )(dest, pad_end, h2aug)


def _group_kernel(bgrp_ref, nused_ref, bmask_ref, x_ref, wgu_ref, wd_ref, y_ref, xb_ref):
    del bgrp_ref
    i = pl.program_id(0)
    y_ref[...] = jnp.zeros_like(y_ref)

    @pl.when(i < nused_ref[0])
    def _():
        xb_ref[...] = x_ref[:, :D_MODEL].astype(BF16)
        mask = bmask_ref[i]
        for e in range(EXPERTS_PER_GROUP):
            @pl.when(((mask >> e) & 1) == 1)
            def _():
                gu = jnp.dot(xb_ref[...], wgu_ref[0, e], preferred_element_type=F32)
                gate = gu[:, :D_EXPERT]
                lane = D_MODEL + REC_WEIGHT_LANE0 + e
                w = x_ref[:, lane:lane + 1]
                hw = jnp.where(w != 0.0, w * (gate * jax.nn.sigmoid(gate) * gu[:, D_EXPERT:]), 0.0)
                y_ref[...] += jnp.dot(hw.astype(BF16), wd_ref[0, e * D_EXPERT:(e + 1) * D_EXPERT, :],
                                      preferred_element_type=F32)


def _moe_groups(block_group, n_used, block_mask, xs, w_gu, w_down):
    rows = xs.shape[0]
    e = EXPERTS_PER_GROUP
    grid_spec = pltpu.PrefetchScalarGridSpec(
        num_scalar_prefetch=3,
        grid=(rows // GROUP_BLOCK,),
        in_specs=[pl.BlockSpec((GROUP_BLOCK, ROW_W), lambda i, *_: (i, 0)),
                  pl.BlockSpec((1, e, D_MODEL, 2 * D_EXPERT), lambda i, bg, *_: (bg[i], 0, 0, 0)),
                  pl.BlockSpec((1, e * D_EXPERT, D_MODEL), lambda i, bg, *_: (bg[i], 0, 0))],
        out_specs=pl.BlockSpec((GROUP_BLOCK, D_MODEL), lambda i, *_: (i, 0)),
        scratch_shapes=[pltpu.VMEM((GROUP_BLOCK, D_MODEL), BF16)],
    )
    return pl.pallas_call(
        _group_kernel,
        grid_spec=grid_spec,
        out_shape=jax.ShapeDtypeStruct((rows, D_MODEL), F32),
        compiler_params=_params("arbitrary"),
        name="moe_groups",
    )(block_group, n_used, block_mask, xs, w_gu, w_down)


def _combine_kernel(alpha, dest_ref, h_ref, vec_ref, y_hbm, o_ref, ybuf, sems):
    tc = h_ref.shape[0]
    i = pl.program_id(0)
    steps = pl.num_programs(0)

    def gather(tile, slot):
        def issue(j, carry):
            for u in range(DMA_UNROLL):
                t = j * DMA_UNROLL + u
                pltpu.make_async_copy(y_hbm.at[pl.ds(dest_ref[tile * tc + t], 1), :],
                                      ybuf.at[slot, pl.ds(t, 1), :], sems.at[slot]).start()
            return carry
        lax.fori_loop(0, tc // DMA_UNROLL, issue, 0)

    @pl.when(i == 0)
    def _():
        gather(0, 0)

    @pl.when(i + 1 < steps)
    def _():
        gather(i + 1, (i + 1) % 2)

    slot = i % 2
    pltpu.make_async_copy(y_hbm.at[pl.ds(0, tc), :], ybuf.at[slot], sems.at[slot]).wait()
    o_ref[...] = _layer_norm(alpha * h_ref[...] + ybuf[slot], vec_ref[0:1, :], vec_ref[1:2, :])


def _moe_combine(alpha, dest, h2aug, vecs, y):
    n = h2aug.shape[0]
    tc = min(512, n)
    grid_spec = pltpu.PrefetchScalarGridSpec(
        num_scalar_prefetch=1,
        grid=(n // tc,),
        in_specs=[pl.BlockSpec((tc, D_MODEL), lambda i, *_: (i, 0)),
                  pl.BlockSpec((SUBLANES, D_MODEL), lambda i, *_: (0, 0)),
                  pl.BlockSpec(memory_space=pl.ANY)],
        out_specs=pl.BlockSpec((tc, D_MODEL), lambda i, *_: (i, 0)),
        scratch_shapes=[pltpu.VMEM((2, tc, D_MODEL), F32), pltpu.SemaphoreType.DMA((2,))],
    )
    return pl.pallas_call(
        functools.partial(_combine_kernel, alpha),
        grid_spec=grid_spec,
        out_shape=jax.ShapeDtypeStruct((n, D_MODEL), F32),
        compiler_params=_params("arbitrary"),
        name="moe_combine",
    )(dest, h2aug, vecs, y)


def _pad_rows(vectors):
    rows = [v.astype(F32)[None, :] for v in vectors]
    rows.append(jnp.zeros((SUBLANES - len(rows), vectors[0].shape[0]), F32))
    return jnp.concatenate(rows, axis=0)


def _layer(alpha, h, mem2, pos2, batch, seq, mem_len, w_in, attn_sinks, conv_w, conv_b, b_igate, b_fgate,
           ml_norm_g, w_att_branch, w_ml_branch, w_mix_out, ln1_g, ln1_b, w_xq, w_xkv, w_xo, ln2_g, ln2_b,
           w_router_group, b_router_group, w_router_expert, b_router_expert, w_gate, w_up, w_down,
           ln3_g, ln3_b):
    n = h.shape[0]
    att, mq, mk, mv, gates = _in_proj(h, pos2, w_in, attn_sinks, conv_w, conv_b, b_igate, b_fgate, batch, seq)
    kx, vx = _mem_kv(mem2, w_xkv, batch, mem_len)

    o_mo = ATT_Q_W + 2 * ATT_KV_W + 2 * ML_QK_W + ML_V_W
    o_ga = o_mo + ML_V_W + 2 * ML_HEADS
    w_gates = jnp.concatenate([w_in[:, o_mo:o_mo + ML_V_W], w_in[:, o_ga:]], axis=1).astype(BF16)
    pad = LANES - N_GROUPS - N_EXPERTS
    w_router = jnp.pad(jnp.concatenate([w_router_group, w_router_expert], axis=1).astype(F32), ((0, 0), (0, pad)))
    b_router = jnp.pad(jnp.concatenate([b_router_group, b_router_expert]).astype(F32), (0, pad))[None, :]
    vecs = _pad_rows([ml_norm_g, ln1_g, ln1_b, ln2_g, ln2_b])
    h2aug, counts = _post(alpha, h, att, mq, mk, mv, gates, kx, vx, w_gates, w_att_branch.astype(BF16),
                          w_ml_branch.astype(BF16), w_mix_out.astype(BF16), w_xq.astype(BF16),
                          w_xo.astype(BF16), w_router, b_router, vecs, batch, seq, mem_len)

    e = EXPERTS_PER_GROUP
    class_count = counts[0, :N_CLASSES].astype(I32)
    group_count = jnp.sum(class_count.reshape(N_GROUPS, PAIRS_PER_GROUP), axis=1)
    padded = ((group_count + GROUP_BLOCK - 1) // GROUP_BLOCK) * GROUP_BLOCK
    pad_end = jnp.cumsum(padded).astype(I32)
    pad_start = pad_end - padded
    in_group = jnp.cumsum(class_count.reshape(N_GROUPS, PAIRS_PER_GROUP), axis=1) - class_count.reshape(N_GROUPS, -1)
    class_start = (pad_start[:, None] + in_group).reshape(N_CLASSES)
    class_end = class_start + class_count
    n_blocks = n // GROUP_BLOCK + N_GROUPS
    block_start = jnp.arange(n_blocks, dtype=I32) * GROUP_BLOCK
    block_group = jnp.sum((pad_end[None, :] <= block_start[:, None]).astype(I32), axis=1)
    block_group = jnp.minimum(block_group, N_GROUPS - 1)
    n_used = pad_end[-1:] // GROUP_BLOCK
    pairs = [(lo, hi) for lo in range(e) for hi in range(lo + 1, e)]
    uses = jnp.array([[int(x in p) for x in range(e)] for p in pairs] * N_GROUPS, I32)
    overlap = ((class_start[None, :] < block_start[:, None] + GROUP_BLOCK) & (class_end[None, :] > block_start[:, None])
               & (class_count[None, :] > 0)).astype(I32)
    block_uses = jnp.sum(overlap[:, :, None] * uses[None, :, :], axis=1)
    block_mask = jnp.sum((block_uses > 0).astype(I32) << jnp.arange(e, dtype=I32)[None, :], axis=1)
    cls = h2aug[:, D_MODEL + REC_CLASS_LANE].astype(I32)
    rank = h2aug[:, D_MODEL + REC_RANK_LANE].astype(I32)
    is_class = cls[:, None] == jnp.arange(N_CLASSES, dtype=I32)[None, :]
    dest = jnp.sum(jnp.where(is_class, class_start[None, :], 0), axis=1) + rank

    xs = _moe_scatter(dest, pad_end, h2aug, n_blocks * GROUP_BLOCK)
    w_gu = jnp.concatenate([w_gate, w_up], axis=2).astype(BF16).reshape(N_GROUPS, e, D_MODEL, 2 * D_EXPERT)
    w_dn = w_down.astype(BF16).reshape(N_GROUPS, e * D_EXPERT, D_MODEL)
    y = _moe_groups(block_group, n_used, block_mask, xs, w_gu, w_dn)
    return _moe_combine(alpha, dest, h2aug, _pad_rows([ln3_g, ln3_b]), y)


def kernel(x, mem, positions, w_in, attn_sinks, conv_w, conv_b, b_igate, b_fgate, ml_norm_g, w_att_branch, w_ml_branch, w_mix_out, ln1_g, ln1_b, w_xq, w_xkv, w_xo, ln2_g, ln2_b, w_router_group, b_router_group, w_router_expert, b_router_expert, w_gate, w_up, w_down, ln3_g, ln3_b):
    batch, seq, d = x.shape
    mem_len = mem.shape[1]
    depth = w_in.shape[0]
    alpha = (2 * depth) ** 0.25
    h = x.reshape(batch * seq, d)
    mem2 = mem.reshape(batch * mem_len, d)
    pos2 = positions.reshape(batch * seq, 1)
    stacked = (w_in, attn_sinks, conv_w, conv_b, b_igate, b_fgate, ml_norm_g, w_att_branch, w_ml_branch,
               w_mix_out, ln1_g, ln1_b, w_xq, w_xkv, w_xo, ln2_g, ln2_b, w_router_group, b_router_group,
               w_router_expert, b_router_expert, w_gate, w_up, w_down, ln3_g, ln3_b)
    for l in range(depth):
        h = _layer(alpha, h, mem2, pos2, batch, seq, mem_len, *(w[l] for w in stacked))
    return h.reshape(batch, seq, d)
```

```python
import functools

import jax
import jax.numpy as jnp
from jax import lax
from jax.experimental import pallas as pl
from jax.experimental.pallas import tpu as pltpu

F32 = jnp.float32
BF16 = jnp.bfloat16
I32 = jnp.int32

D_MODEL = 1024
ATT_HEADS = 16
ATT_KV_HEADS = 2
ATT_HEAD_DIM = 64
ATT_BLOCK = 128
ROPE_THETA = 10000.0
ML_HEADS = 4
ML_DQK = 128
ML_DV = 256
ML_CHUNK = 128
CONV_WIDTH = 4
X_HEADS = 4
X_HEAD_DIM = D_MODEL // X_HEADS
N_GROUPS = 8
EXPERTS_PER_GROUP = 8
N_EXPERTS = N_GROUPS * EXPERTS_PER_GROUP
D_EXPERT = 256
LN_EPS = 1e-5
RMS_EPS = 1e-6

ATT_Q_W = ATT_HEADS * ATT_HEAD_DIM
ATT_KV_W = ATT_KV_HEADS * ATT_HEAD_DIM
ML_QK_W = ML_HEADS * ML_DQK
ML_V_W = ML_HEADS * ML_DV

LANES = 128
SUBLANES = 8
VMEM_LIMIT_BYTES = 56 * 1024 * 1024

ROUTER_EXPERT_LANE0 = N_GROUPS
PAIRS_PER_GROUP = EXPERTS_PER_GROUP * (EXPERTS_PER_GROUP - 1) // 2
N_CLASSES = N_GROUPS * PAIRS_PER_GROUP
CLASS_LANES = 256
REC_CLASS_LANE = 0
REC_RANK_LANE = 1
REC_WEIGHT_LANE0 = 8
ROW_W = D_MODEL + LANES
GROUP_BLOCK = 512
IN_PROJ_TILE = 1024
POST_TILE = 512
SCATTER_TILE = 1024
COMBINE_TILE = 512
LOG2_E = 1.4426950408889634
POST_COLS = 512


def _params(*semantics):
    return pltpu.CompilerParams(dimension_semantics=semantics, vmem_limit_bytes=VMEM_LIMIT_BYTES)


def _const_spec(shape):
    zeros = (0,) * len(shape)
    return pl.BlockSpec(shape, lambda *_: zeros, pipeline_mode=pl.Buffered(1))


def _layer_norm(z, g, b):
    mu = jnp.mean(z, axis=-1, keepdims=True)
    zc = z - mu
    var = jnp.mean(zc * zc, axis=-1, keepdims=True)
    return zc * lax.rsqrt(var + LN_EPS) * g + b


def _inproj_kernel(sink_ref, x_ref, pos_ref, inv_ref, wrope_ref, wv_ref, wmqk_ref, wmv_ref, wg_ref,
                   convw_ref, convb_ref, gbias_ref,
                   att_out, mq_out, mk_out, mv_out, g_out, prev_ref, kvprev_ref):
    t = x_ref.shape[0]
    first = pl.program_id(1) == 0
    xb = x_ref[...].astype(BF16)

    half = ATT_HEAD_DIM // 2
    ang = pos_ref[...].astype(F32) * inv_ref[0:1, :]
    sin_cos = jnp.sin(ang + inv_ref[1:2, :])
    lane = lax.broadcasted_iota(I32, (t, LANES), 1)
    first_half = (lane % ATT_HEAD_DIM) < half
    cos = jnp.where(first_half, pltpu.roll(sin_cos, LANES - half, 1), sin_cos)
    sin_signed = jnp.where(first_half, -sin_cos, pltpu.roll(sin_cos, half, 1))

    def rope(z):
        partner = jnp.where(first_half, pltpu.roll(z, LANES - ATT_HEAD_DIM // 2, 1),
                            pltpu.roll(z, ATT_HEAD_DIM // 2, 1))
        return z * cos + partner * sin_signed

    qk = jnp.dot(xb, wrope_ref[...], preferred_element_type=F32)
    q_scale = ATT_HEAD_DIM ** -0.5 * LOG2_E
    q_groups = [(rope(qk[:, g * LANES:(g + 1) * LANES]) * q_scale).astype(BF16) for g in range(ATT_Q_W // LANES)]
    k_groups = [rope(qk[:, ATT_Q_W + g * LANES:ATT_Q_W + (g + 1) * LANES]).astype(BF16)
                for g in range(ATT_KV_HEADS)]
    v_all = jnp.dot(xb, wv_ref[...], preferred_element_type=F32).astype(BF16)
    v_groups = [v_all[:, g * LANES:(g + 1) * LANES] for g in range(ATT_KV_HEADS)]

    nk = 2 * ATT_BLOCK
    kc = lax.broadcasted_iota(I32, (nk, ATT_BLOCK), 0)
    qi = lax.broadcasted_iota(I32, (nk, ATT_BLOCK), 1)
    band = (kc > qi) & (kc <= qi + ATT_BLOCK)
    lo = lax.broadcasted_iota(I32, (nk, LANES), 1) < ATT_HEAD_DIM
    dim_lo = lax.broadcasted_iota(I32, (LANES, ATT_BLOCK), 0) < ATT_HEAD_DIM
    pairs = ATT_HEADS // ATT_KV_HEADS // 2

    @pl.when(first)
    def _():
        kvprev_ref[...] = jnp.zeros_like(kvprev_ref)

    for j in range(t // ATT_BLOCK):
        cur = slice(j * ATT_BLOCK, (j + 1) * ATT_BLOCK)
        valid = band & (kc >= jnp.where(first, ATT_BLOCK, 0)) if j == 0 else band
        for kvh in range(ATT_KV_HEADS):
            if j == 0:
                k_prev = kvprev_ref[:, kvh * LANES:(kvh + 1) * LANES]
                v_prev = kvprev_ref[:, (ATT_KV_HEADS + kvh) * LANES:(ATT_KV_HEADS + kvh + 1) * LANES]
            else:
                prev = slice((j - 1) * ATT_BLOCK, j * ATT_BLOCK)
                k_prev, v_prev = k_groups[kvh][prev], v_groups[kvh][prev]
            kk = jnp.concatenate([k_prev, k_groups[kvh][cur]], axis=0)
            vv = jnp.concatenate([v_prev, v_groups[kvh][cur]], axis=0).astype(F32)
            zero = jnp.zeros_like(kk)
            k2 = jnp.concatenate([jnp.where(lo, kk, zero), jnp.where(lo, zero, kk)], axis=0)
            v2t = jnp.concatenate([jnp.where(lo, vv, 0.0), jnp.where(lo, 0.0, vv)], axis=0).T.astype(BF16)
            for pp in range(pairs):
                p = kvh * pairs + pp
                qp = q_groups[p][cur]
                st = lax.dot_general(k2, qp, (((1,), (1,)), ((), ())), preferred_element_type=F32)
                e_heads, recip = [], []
                for hh in range(2):
                    sink = sink_ref[2 * p + hh]
                    s_h = jnp.where(valid, st[hh * nk:(hh + 1) * nk], -jnp.inf)
                    m = jnp.maximum(jnp.max(s_h, axis=0, keepdims=True), sink)
                    e = jnp.exp2(s_h - m)
                    den = jnp.sum(e, axis=0, keepdims=True) + jnp.exp2(sink - m)
                    e_heads.append(e.astype(BF16))
                    recip.append(1.0 / den)
                out_t = jnp.dot(v2t, jnp.concatenate(e_heads, axis=0), preferred_element_type=F32)
                out_t = out_t * jnp.where(dim_lo, recip[0], recip[1])
                att_out[cur, p * LANES:(p + 1) * LANES] = out_t.T.astype(BF16)
    last = slice(t - ATT_BLOCK, t)
    kvprev_ref[...] = jnp.concatenate([g[last] for g in k_groups] + [g[last] for g in v_groups], axis=1)

    @pl.when(pl.program_id(1) == 0)
    def _():
        prev_ref[...] = jnp.zeros_like(prev_ref)

    row8 = lax.broadcasted_iota(I32, (SUBLANES, ML_QK_W), 0)
    for out_ref, scale, cs in ((mq_out, 1.0, slice(0, ML_QK_W)), (mk_out, ML_DQK ** -0.5, slice(ML_QK_W, 2 * ML_QK_W))):
        pre = jnp.dot(xb, wmqk_ref[:, cs], preferred_element_type=F32)
        prev8 = prev_ref[:, cs]
        w_now = convw_ref[CONV_WIDTH - 1:CONV_WIDTH, cs]
        acc = pre * w_now + convb_ref[:, cs]
        top = pre[0:SUBLANES] * w_now + convb_ref[:, cs]
        for d in range(1, CONV_WIDTH):
            w_d = convw_ref[CONV_WIDTH - 1 - d:CONV_WIDTH - d, cs]
            shifted = pltpu.roll(pre, d, 0)
            acc = acc + shifted * w_d
            top = top + jnp.where(row8 < d, pltpu.roll(prev8, d, 0), shifted[0:SUBLANES]) * w_d
        prev_ref[:, cs] = pre[t - SUBLANES:t]
        conv = jnp.concatenate([top, acc[SUBLANES:]], axis=0)
        out_ref[...] = (conv * jax.nn.sigmoid(conv) * scale).astype(BF16)

    mv_out[...] = jnp.dot(xb, wmv_ref[...], preferred_element_type=F32).astype(BF16)

    gates = jnp.dot(xb, wg_ref[...], preferred_element_type=F32) + gbias_ref[...]
    g_out[...] = jnp.where(lane < ML_HEADS, gates, jax.nn.log_sigmoid(gates))


def _in_proj(x2, pos2, w_in, sinks, conv_w, conv_b, b_igate, b_fgate, batch, seq):
    n = x2.shape[0]
    t = min(IN_PROJ_TILE, seq)
    s_tiles = seq // t

    def dup(w):
        h = ATT_HEAD_DIM
        return jnp.concatenate([w[:, :h], w[:, :h], w[:, h:], w[:, h:]], axis=1)

    o_k = ATT_Q_W
    o_v = o_k + ATT_KV_W
    o_mq = o_v + ATT_KV_W
    o_mv = o_mq + 2 * ML_QK_W
    o_mo = o_mv + ML_V_W
    o_mi = o_mo + ML_V_W
    w_rope = jnp.concatenate([w_in[:, :o_k], dup(w_in[:, o_k:o_v])], axis=1).astype(BF16)
    w_v = dup(w_in[:, o_v:o_mq]).astype(BF16)
    w_mqk = w_in[:, o_mq:o_mv].astype(BF16)
    w_mv = w_in[:, o_mv:o_mo].astype(BF16)
    w_g = jnp.pad(w_in[:, o_mi:o_mi + 2 * ML_HEADS], ((0, 0), (0, LANES - 2 * ML_HEADS))).astype(BF16)
    gbias = jnp.pad(jnp.concatenate([b_igate, b_fgate]).astype(F32), (0, LANES - 2 * ML_HEADS))[None, :]
    half = ATT_HEAD_DIM // 2
    inv = ROPE_THETA ** (-jnp.arange(half, dtype=F32) / half)
    quarter_turn = jnp.where((jnp.arange(LANES) % ATT_HEAD_DIM) < half, 0.0, jnp.pi / 2).astype(F32)
    inv = jnp.stack([jnp.tile(inv, LANES // half), quarter_turn])

    row = lambda w: pl.BlockSpec((t, w), lambda b, s, *_: (b * s_tiles + s, 0))
    grid_spec = pltpu.PrefetchScalarGridSpec(
        num_scalar_prefetch=1,
        grid=(batch, s_tiles),
        in_specs=[row(D_MODEL), row(1), _const_spec((2, LANES)),
                  _const_spec(w_rope.shape), _const_spec(w_v.shape), _const_spec(w_mqk.shape),
                  _const_spec(w_mv.shape), _const_spec(w_g.shape),
                  _const_spec((CONV_WIDTH, 2 * ML_QK_W)), _const_spec((1, 2 * ML_QK_W)),
                  _const_spec((1, LANES))],
        out_specs=[row(ATT_Q_W), row(ML_QK_W), row(ML_QK_W), row(ML_V_W), row(LANES)],
        scratch_shapes=[pltpu.VMEM((SUBLANES, 2 * ML_QK_W), F32),
                        pltpu.VMEM((ATT_BLOCK, 4 * ATT_KV_W), BF16)],
    )
    return pl.pallas_call(
        _inproj_kernel,
        grid_spec=grid_spec,
        out_shape=[jax.ShapeDtypeStruct((n, ATT_Q_W), BF16),
                   jax.ShapeDtypeStruct((n, ML_QK_W), BF16),
                   jax.ShapeDtypeStruct((n, ML_QK_W), BF16),
                   jax.ShapeDtypeStruct((n, ML_V_W), BF16),
                   jax.ShapeDtypeStruct((n, LANES), F32)],
        compiler_params=_params("arbitrary", "arbitrary"),
        name="in_proj",
    )(sinks.astype(F32) * LOG2_E, x2, pos2, inv, w_rope, w_v, w_mqk, w_mv, w_g, conv_w.astype(F32),
      conv_b.astype(F32)[None, :], gbias)


def _mlstm_reset(c_ref, n_ref, m_ref):
    c_ref[...] = jnp.zeros_like(c_ref)
    n_ref[...] = jnp.zeros_like(n_ref)
    m_ref[...] = jnp.zeros_like(m_ref)


def _mlstm_chunk(rows, q_ref, k_ref, v_ref, g_ref, o_ref, c_ref, n_ref, m_ref):
    nb = q_ref.shape[0]
    L = ML_CHUNK
    causal = lax.broadcasted_iota(I32, (L, L), 1) <= lax.broadcasted_iota(I32, (L, L), 0)
    rowi = lax.broadcasted_iota(I32, (L, LANES), 0)

    def scan_rows(x, op, identity):
        d = 1
        while d < L:
            x = op(x, jnp.where(rowi >= d, pltpu.roll(x, d, 0), identity))
            d *= 2
        return x

    def chunk():
        heads = [(bb, h) for bb in range(nb) for h in range(ML_HEADS)]

        gate = []
        for bb in range(nb):
            g = g_ref[bb, rows, :]
            b = pltpu.roll(scan_rows(g, jnp.add, 0.0), LANES - ML_HEADS, 1)
            a = g - b
            a_max = scan_rows(a, jnp.maximum, -jnp.inf)
            m_prev = m_ref[bb, 0:1, :]
            m_t = b + jnp.maximum(m_prev, a_max)
            u = b - m_t
            b_end = b[L - 1:L, :]
            m_new = b_end + jnp.maximum(m_prev, a_max[L - 1:L, :])
            m_ref[bb, 0:1, :] = m_new
            gate.append(dict(u=u, w_inter=jnp.exp(u + m_prev), floor=jnp.exp(-m_t),
                             w_state=jnp.exp(b_end + m_prev - m_new), w_k=jnp.exp(b_end + a - m_new),
                             a_t=a.T))

        nt = (((1,), (1,)), ((), ()))
        for bb, h in heads:
            gt = gate[bb]
            idx = bb * ML_HEADS + h
            col = slice(h, h + 1)
            qh = q_ref[bb, rows, h * ML_DQK:(h + 1) * ML_DQK]
            kh = k_ref[bb, rows, h * ML_DQK:(h + 1) * ML_DQK]
            vh = v_ref[bb, rows, h * ML_DV:(h + 1) * ML_DV]
            c_prev = c_ref[idx]
            n_prev = n_ref[idx, 0:1, :]

            w_intra = jnp.exp(jnp.where(causal, gt["u"][:, col] + gt["a_t"][col, :], -jnp.inf))
            s = lax.dot_general(qh, kh, nt, preferred_element_type=F32) * w_intra
            w_inter = gt["w_inter"][:, col]
            num = (w_inter * jnp.dot(qh, c_prev.astype(BF16), preferred_element_type=F32)
                   + jnp.dot(s.astype(BF16), vh, preferred_element_type=F32))
            den = (w_inter * jnp.sum(qh.astype(F32) * n_prev, axis=1, keepdims=True)
                   + jnp.sum(s, axis=1, keepdims=True))
            hval = num / jnp.maximum(jnp.abs(den), gt["floor"][:, col])
            hval = hval * lax.rsqrt(jnp.mean(hval * hval, axis=1, keepdims=True) + RMS_EPS)
            o_ref[bb, rows, h * ML_DV:(h + 1) * ML_DV] = hval.astype(BF16)

            w_state = gt["w_state"][:, col]
            kw = kh.astype(F32) * gt["w_k"][:, col]
            c_ref[idx] = w_state * c_prev + jnp.dot(kw.T.astype(BF16), vh, preferred_element_type=F32)
            n_ref[idx, 0:1, :] = w_state * n_prev + jnp.sum(kw, axis=0, keepdims=True)

    chunk()


def _memkv_kernel(mem_ref, w_ref, k_out, v_out):
    kv = jnp.dot(mem_ref[...].astype(BF16), w_ref[...], preferred_element_type=F32)
    k_out[...] = kv[:, :D_MODEL].astype(BF16)
    v_out[...] = kv[:, D_MODEL:].astype(BF16)


def _mem_kv(mem2, w_xkv, batch, mem_len):
    blk = pl.BlockSpec((mem_len, D_MODEL), lambda b: (b, 0))
    return pl.pallas_call(
        _memkv_kernel,
        grid=(batch,),
        in_specs=[blk, _const_spec((D_MODEL, 2 * D_MODEL))],
        out_specs=[blk, blk],
        out_shape=[jax.ShapeDtypeStruct((batch * mem_len, D_MODEL), BF16)] * 2,
        compiler_params=_params("arbitrary"),
        name="mem_kv",
    )(mem2, w_xkv.astype(BF16))


def _post_kernel(alpha, x_ref, att_ref, mq_ref, mk_ref, mv_ref, g_ref, kx_ref, vx_ref, wgate_ref, watt_ref,
                 wml_ref, wmix_ref, wxq_ref, wxo_ref, wr_hi_ref, wr_lo_ref, vec_ref, rbias_ref,
                 h_out, count_out, base_ref, c_ref, n_ref, m_ref, hm_ref):
    t = x_ref.shape[0]
    first = (pl.program_id(0) == 0) & (pl.program_id(1) == 0)

    @pl.when(first)
    def _():
        base_ref[...] = jnp.zeros_like(base_ref)

    @pl.when(pl.program_id(1) == 0)
    def _():
        _mlstm_reset(c_ref, n_ref, m_ref)

    for c in range(t // ML_CHUNK):
        _mlstm_chunk(slice(c * ML_CHUNK, (c + 1) * ML_CHUNK), mq_ref, mk_ref, mv_ref, g_ref, hm_ref,
                     c_ref, n_ref, m_ref)

    x = x_ref[...]
    xb = x.astype(BF16)
    norm_g, ln1_g, ln1_b, ln2_g, ln2_b = (vec_ref[i:i + 1, :] for i in range(5))

    chunks = [slice(c, c + POST_COLS) for c in range(0, D_MODEL, POST_COLS)]
    hm = jnp.concatenate(
        [(hm_ref[0, :, cs].astype(F32) * norm_g[:, cs]
          * jax.nn.sigmoid(jnp.dot(xb, wgate_ref[:, cs], preferred_element_type=F32))).astype(BF16)
         for cs in chunks], axis=1)
    att = att_ref[...]
    y = []
    for cs in chunks:
        gm = jnp.dot(xb, wgate_ref[:, 2 * D_MODEL + cs.start:2 * D_MODEL + cs.stop], preferred_element_type=F32)
        ga = jnp.dot(xb, wgate_ref[:, D_MODEL + cs.start:D_MODEL + cs.stop], preferred_element_type=F32)
        m_out = jnp.dot(hm, wml_ref[:, cs], preferred_element_type=F32)
        a_out = jnp.dot(att, watt_ref[:, cs], preferred_element_type=F32)
        y.append((jax.nn.sigmoid(gm) * m_out + jax.nn.sigmoid(ga) * a_out).astype(BF16))
    mix = jnp.dot(jnp.concatenate(y, axis=1), wmix_ref[...], preferred_element_type=F32)
    h1 = _layer_norm(alpha * x + mix, ln1_g, ln1_b)

    q = (jnp.dot(h1.astype(BF16), wxq_ref[...], preferred_element_type=F32) * (X_HEAD_DIM ** -0.5)).astype(BF16)
    heads = []
    for h in range(X_HEADS):
        sl = slice(h * X_HEAD_DIM, (h + 1) * X_HEAD_DIM)
        sc = lax.dot_general(q[:, sl], kx_ref[:, sl], (((1,), (1,)), ((), ())), preferred_element_type=F32)
        e = jnp.exp(sc - jnp.max(sc, axis=1, keepdims=True))
        probs = (e * (1.0 / jnp.sum(e, axis=1, keepdims=True))).astype(BF16)
        heads.append(jnp.dot(probs, vx_ref[:, sl], preferred_element_type=F32).astype(BF16))
    xa = jnp.dot(jnp.concatenate(heads, axis=1), wxo_ref[...], preferred_element_type=F32)
    h2 = _layer_norm(alpha * h1 + xa, ln2_g, ln2_b)
    h_out[:, :D_MODEL] = h2

    h_hi = h2.astype(BF16)
    h_lo = (h2 - h_hi.astype(F32)).astype(BF16)
    logits = (jnp.dot(h_hi, wr_hi_ref[...], preferred_element_type=F32)
              + jnp.dot(h_lo, wr_hi_ref[...], preferred_element_type=F32)
              + jnp.dot(h_hi, wr_lo_ref[...], preferred_element_type=F32)) + rbias_ref[...]

    lane = lax.broadcasted_iota(I32, (t, LANES), 1)
    neg = -jnp.inf
    is_group = lane < N_GROUPS
    gl = jnp.where(is_group, logits, neg)
    g_max = jnp.max(gl, axis=1, keepdims=True)
    lane_f = lane.astype(F32)
    no_lane = float(LANES)
    g_sel = jnp.min(jnp.where(gl == g_max, lane_f, no_lane), axis=1, keepdims=True)
    p_group = 1.0 / jnp.sum(jnp.where(is_group, jnp.exp(logits - g_max), 0.0), axis=1, keepdims=True)
    lane_group = ((lane - ROUTER_EXPERT_LANE0) >> (EXPERTS_PER_GROUP.bit_length() - 1)).astype(F32)
    in_group = (lane >= ROUTER_EXPERT_LANE0) & (lane_group == g_sel)
    el = jnp.where(in_group, logits, neg)
    v1 = jnp.max(el, axis=1, keepdims=True)
    i1 = jnp.min(jnp.where(el == v1, lane_f, no_lane), axis=1, keepdims=True)
    el2 = jnp.where(lane_f == i1, neg, el)
    v2 = jnp.max(el2, axis=1, keepdims=True)
    i2 = jnp.min(jnp.where(el2 == v2, lane_f, no_lane), axis=1, keepdims=True)
    e2 = jnp.exp(v2 - v1)
    w1 = p_group * (1.0 / (1.0 + e2))
    w2 = p_group * (e2 / (1.0 + e2))

    first_logit = ROUTER_EXPERT_LANE0 + EXPERTS_PER_GROUP * g_sel
    lo = jnp.minimum(i1, i2) - first_logit
    hi = jnp.maximum(i1, i2) - first_logit
    cls = PAIRS_PER_GROUP * g_sel + lo * (2 * EXPERTS_PER_GROUP - 3 - lo) * 0.5 + hi - 1.0
    class_lane = lax.broadcasted_iota(I32, (t, CLASS_LANES), 1).astype(F32)
    onehot = jnp.where(class_lane == cls, 1.0, 0.0)
    tri = jnp.where(lax.broadcasted_iota(I32, (t, t), 1) < lax.broadcasted_iota(I32, (t, t), 0), 1.0, 0.0).astype(BF16)
    before = jnp.dot(tri, onehot.astype(BF16), preferred_element_type=F32) + base_ref[0:1, :]
    rank = jnp.sum(jnp.where(class_lane == cls, before, 0.0), axis=1, keepdims=True)
    total = base_ref[0:1, :] + jnp.sum(onehot, axis=0, keepdims=True)
    base_ref[...] = jnp.broadcast_to(total, base_ref.shape)
    count_out[...] = jnp.broadcast_to(total, count_out.shape)

    logit_lane = lane_f + (ROUTER_EXPERT_LANE0 - REC_WEIGHT_LANE0) + EXPERTS_PER_GROUP * g_sel
    rec = jnp.where(logit_lane == i1, w1, jnp.where(logit_lane == i2, w2, 0.0))
    rec = jnp.where((lane >= REC_WEIGHT_LANE0) & (lane < REC_WEIGHT_LANE0 + EXPERTS_PER_GROUP), rec, 0.0)
    rec = jnp.where(lane == REC_CLASS_LANE, cls, rec)
    rec = jnp.where(lane == REC_RANK_LANE, rank, rec)
    h_out[:, D_MODEL:] = rec


def _post(alpha, x2, att, mq, mk, mv, gates, kx, vx, w_gates, w_att, w_ml, w_mix, w_xq, w_xo, w_router, b_router,
          vecs, batch, seq, mem_len):
    n = x2.shape[0]
    t = min(POST_TILE, seq)
    s_tiles = seq // t
    row = lambda w: pl.BlockSpec((t, w), lambda b, s: (b * s_tiles + s, 0))
    memblk = pl.BlockSpec((mem_len, D_MODEL), lambda b, s: (b, 0))
    seq_blk = lambda w: pl.BlockSpec((1, t, w), lambda b, s: (b, s, 0))
    seqs = lambda a: a.reshape(batch, seq, a.shape[-1])
    w_hi = w_router.astype(BF16)
    w_lo = (w_router - w_hi.astype(F32)).astype(BF16)
    sq = _const_spec((D_MODEL, D_MODEL))
    return pl.pallas_call(
        functools.partial(_post_kernel, alpha),
        grid=(batch, s_tiles),
        in_specs=[row(D_MODEL), row(ATT_Q_W), seq_blk(ML_QK_W), seq_blk(ML_QK_W), seq_blk(ML_V_W), seq_blk(LANES),
                  memblk, memblk,
                  _const_spec((D_MODEL, 3 * D_MODEL)), sq, sq, sq, sq, sq,
                  _const_spec((D_MODEL, LANES)), _const_spec((D_MODEL, LANES)),
                  _const_spec((SUBLANES, D_MODEL)), _const_spec((1, LANES))],
        out_specs=[row(ROW_W), _const_spec((SUBLANES, CLASS_LANES))],
        out_shape=[jax.ShapeDtypeStruct((n, ROW_W), F32),
                   jax.ShapeDtypeStruct((SUBLANES, CLASS_LANES), F32)],
        scratch_shapes=[pltpu.VMEM((SUBLANES, CLASS_LANES), F32),
                        pltpu.VMEM((ML_HEADS, ML_DQK, ML_DV), F32),
                        pltpu.VMEM((ML_HEADS, SUBLANES, ML_DQK), F32),
                        pltpu.VMEM((1, SUBLANES, LANES), F32),
                        pltpu.VMEM((1, t, ML_V_W), BF16)],
        compiler_params=_params("arbitrary", "arbitrary"),
        name="post",
    )(x2, att, seqs(mq), seqs(mk), seqs(mv), seqs(gates), kx, vx, w_gates, w_att, w_ml, w_mix, w_xq, w_xo, w_hi, w_lo, vecs, b_router)


DMA_UNROLL = 32


def _scatter_kernel(dest_ref, padend_ref, h_ref, xs_out, zero_buf, sem, zero_sem):
    i = pl.program_id(0)
    ts = h_ref.shape[0]

    @pl.when(i == 0)
    def _():
        zero_buf[...] = jnp.zeros_like(zero_buf)

        def clear(start):
            start = pl.multiple_of(start, GROUP_BLOCK)
            return pltpu.make_async_copy(zero_buf, xs_out.at[pl.ds(start, GROUP_BLOCK), :], zero_sem)

        def nonempty(g):
            return padend_ref[g] > (padend_ref[g - 1] if g else 0)

        used = padend_ref[N_GROUPS - 1]

        def unused(b):
            return used + b * GROUP_BLOCK < xs_out.shape[0]

        for phase in ("start", "wait"):
            for g in range(N_GROUPS):
                @pl.when(nonempty(g))
                def _():
                    getattr(clear(padend_ref[g] - GROUP_BLOCK), phase)()

                @pl.when(unused(g))
                def _():
                    getattr(clear(used + g * GROUP_BLOCK), phase)()

    def issue(j, carry):
        for u in range(DMA_UNROLL):
            t = j * DMA_UNROLL + u
            pltpu.make_async_copy(h_ref.at[pl.ds(t, 1), :], xs_out.at[pl.ds(dest_ref[i * ts + t], 1), :], sem).start()
        return carry

    lax.fori_loop(0, ts // DMA_UNROLL, issue, 0)

    pltpu.make_async_copy(h_ref, xs_out.at[pl.ds(0, ts), :], sem).wait()


def _moe_scatter(dest, pad_end, h2aug, rows):
    n = h2aug.shape[0]
    ts = min(SCATTER_TILE, n)
    grid_spec = pltpu.PrefetchScalarGridSpec(
        num_scalar_prefetch=2,
        grid=(n // ts,),
        in_specs=[pl.BlockSpec((ts, ROW_W), lambda i, *_: (i, 0))],
        out_specs=pl.BlockSpec(memory_space=pl.ANY),
        scratch_shapes=[pltpu.VMEM((GROUP_BLOCK, ROW_W), F32),
                        pltpu.SemaphoreType.DMA(()), pltpu.SemaphoreType.DMA(())],
    )
    return pl.pallas_call(
        _scatter_kernel,
        grid_spec=grid_spec,
        out_shape=jax.ShapeDtypeStruct((rows, ROW_W), F32),
        compiler_params=_params("arbitrary"),
        name="moe_scatter",
    )(dest, pad_end, h2aug)


def _group_kernel(bgrp_ref, nused_ref, bmask_ref, x_ref, wgu_ref, wd_ref, y_ref, xb_ref):
    del bgrp_ref
    i = pl.program_id(0)
    y_ref[...] = jnp.zeros_like(y_ref)

    @pl.when(i < nused_ref[0])
    def _():
        xb_ref[...] = x_ref[:, :D_MODEL].astype(BF16)
        mask = bmask_ref[i]
        for e in range(EXPERTS_PER_GROUP):
            @pl.when(((mask >> e) & 1) == 1)
            def _():
                gu = jnp.dot(xb_ref[...], wgu_ref[0, e], preferred_element_type=F32)
                gate = gu[:, :D_EXPERT]
                lane = D_MODEL + REC_WEIGHT_LANE0 + e
                w = x_ref[:, lane:lane + 1]
                hw = jnp.where(w != 0.0, w * (gate * jax.nn.sigmoid(gate) * gu[:, D_EXPERT:]), 0.0)
                y_ref[...] += jnp.dot(hw.astype(BF16), wd_ref[0, e * D_EXPERT:(e + 1) * D_EXPERT, :],
                                      preferred_element_type=F32)


def _moe_groups(block_group, n_used, block_mask, xs, w_gu, w_down):
    rows = xs.shape[0]
    e = EXPERTS_PER_GROUP
    grid_spec = pltpu.PrefetchScalarGridSpec(
        num_scalar_prefetch=3,
        grid=(rows // GROUP_BLOCK,),
        in_specs=[pl.BlockSpec((GROUP_BLOCK, ROW_W), lambda i, *_: (i, 0)),
                  pl.BlockSpec((1, e, D_MODEL, 2 * D_EXPERT), lambda i, bg, *_: (bg[i], 0, 0, 0)),
                  pl.BlockSpec((1, e * D_EXPERT, D_MODEL), lambda i, bg, *_: (bg[i], 0, 0))],
        out_specs=pl.BlockSpec((GROUP_BLOCK, D_MODEL), lambda i, *_: (i, 0)),
        scratch_shapes=[pltpu.VMEM((GROUP_BLOCK, D_MODEL), BF16)],
    )
    return pl.pallas_call(
        _group_kernel,
        grid_spec=grid_spec,
        out_shape=jax.ShapeDtypeStruct((rows, D_MODEL), F32),
        compiler_params=_params("arbitrary"),
        name="moe_groups",
    )(block_group, n_used, block_mask, xs, w_gu, w_down)


def _combine_kernel(alpha, dest_ref, h_ref, vec_ref, y_hbm, o_ref, ybuf, sems):
    tc = h_ref.shape[0]
    i = pl.program_id(0)
    steps = pl.num_programs(0)

    def gather(tile, slot):
        def issue(j, carry):
            for u in range(DMA_UNROLL):
                t = j * DMA_UNROLL + u
                pltpu.make_async_copy(y_hbm.at[pl.ds(dest_ref[tile * tc + t], 1), :],
                                      ybuf.at[slot, pl.ds(t, 1), :], sems.at[slot]).start()
            return carry
        lax.fori_loop(0, tc // DMA_UNROLL, issue, 0)

    @pl.when(i == 0)
    def _():
        gather(0, 0)

    @pl.when(i + 1 < steps)
    def _():
        gather(i + 1, (i + 1) % 2)

    slot = i % 2
    pltpu.make_async_copy(y_hbm.at[pl.ds(0, tc), :], ybuf.at[slot], sems.at[slot]).wait()
    o_ref[...] = _layer_norm(alpha * h_ref[...] + ybuf[slot], vec_ref[0:1, :], vec_ref[1:2, :])


def _moe_combine(alpha, dest, h2aug, vecs, y):
    n = h2aug.shape[0]
    tc = min(COMBINE_TILE, n)
    grid_spec = pltpu.PrefetchScalarGridSpec(
        num_scalar_prefetch=1,
        grid=(n // tc,),
        in_specs=[pl.BlockSpec((tc, D_MODEL), lambda i, *_: (i, 0)),
                  pl.BlockSpec((SUBLANES, D_MODEL), lambda i, *_: (0, 0)),
                  pl.BlockSpec(memory_space=pl.ANY)],
        out_specs=pl.BlockSpec((tc, D_MODEL), lambda i, *_: (i, 0)),
        scratch_shapes=[pltpu.VMEM((2, tc, D_MODEL), F32), pltpu.SemaphoreType.DMA((2,))],
    )
    return pl.pallas_call(
        functools.partial(_combine_kernel, alpha),
        grid_spec=grid_spec,
        out_shape=jax.ShapeDtypeStruct((n, D_MODEL), F32),
        compiler_params=_params("arbitrary"),
        name="moe_combine",
    )(dest, h2aug, vecs, y)


def _pad_rows(vectors):
    rows = [v.astype(F32)[None, :] for v in vectors]
    rows.append(jnp.zeros((SUBLANES - len(rows), vectors[0].shape[0]), F32))
    return jnp.concatenate(rows, axis=0)


def _layer(alpha, h, mem2, pos2, batch, seq, mem_len, w_in, attn_sinks, conv_w, conv_b, b_igate, b_fgate,
           ml_norm_g, w_att_branch, w_ml_branch, w_mix_out, ln1_g, ln1_b, w_xq, w_xkv, w_xo, ln2_g, ln2_b,
           w_router_group, b_router_group, w_router_expert, b_router_expert, w_gate, w_up, w_down,
           ln3_g, ln3_b):
    n = h.shape[0]
    att, mq, mk, mv, gates = _in_proj(h, pos2, w_in, attn_sinks, conv_w, conv_b, b_igate, b_fgate, batch, seq)
    kx, vx = _mem_kv(mem2, w_xkv, batch, mem_len)

    o_mo = ATT_Q_W + 2 * ATT_KV_W + 2 * ML_QK_W + ML_V_W
    o_ga = o_mo + ML_V_W + 2 * ML_HEADS
    w_gates = jnp.concatenate([w_in[:, o_mo:o_mo + ML_V_W], w_in[:, o_ga:]], axis=1).astype(BF16)
    pad = LANES - N_GROUPS - N_EXPERTS
    w_router = jnp.pad(jnp.concatenate([w_router_group, w_router_expert], axis=1).astype(F32), ((0, 0), (0, pad)))
    b_router = jnp.pad(jnp.concatenate([b_router_group, b_router_expert]).astype(F32), (0, pad))[None, :]
    vecs = _pad_rows([ml_norm_g, ln1_g, ln1_b, ln2_g, ln2_b])
    h2aug, counts = _post(alpha, h, att, mq, mk, mv, gates, kx, vx, w_gates, w_att_branch.astype(BF16),
                          w_ml_branch.astype(BF16), w_mix_out.astype(BF16), w_xq.astype(BF16),
                          w_xo.astype(BF16), w_router, b_router, vecs, batch, seq, mem_len)

    e = EXPERTS_PER_GROUP
    class_count = counts[0, :N_CLASSES].astype(I32)
    group_count = jnp.sum(class_count.reshape(N_GROUPS, PAIRS_PER_GROUP), axis=1)
    padded = ((group_count + GROUP_BLOCK - 1) // GROUP_BLOCK) * GROUP_BLOCK
    pad_end = jnp.cumsum(padded).astype(I32)
    pad_start = pad_end - padded
    in_group = jnp.cumsum(class_count.reshape(N_GROUPS, PAIRS_PER_GROUP), axis=1) - class_count.reshape(N_GROUPS, -1)
    class_start = (pad_start[:, None] + in_group).reshape(N_CLASSES)
    class_end = class_start + class_count
    n_blocks = n // GROUP_BLOCK + N_GROUPS
    block_start = jnp.arange(n_blocks, dtype=I32) * GROUP_BLOCK
    block_group = jnp.sum((pad_end[None, :] <= block_start[:, None]).astype(I32), axis=1)
    block_group = jnp.minimum(block_group, N_GROUPS - 1)
    n_used = pad_end[-1:] // GROUP_BLOCK
    pairs = [(lo, hi) for lo in range(e) for hi in range(lo + 1, e)]
    uses = jnp.array([[int(x in p) for x in range(e)] for p in pairs] * N_GROUPS, I32)
    overlap = ((class_start[None, :] < block_start[:, None] + GROUP_BLOCK) & (class_end[None, :] > block_start[:, None])
               & (class_count[None, :] > 0)).astype(I32)
    block_uses = jnp.sum(overlap[:, :, None] * uses[None, :, :], axis=1)
    block_mask = jnp.sum((block_uses > 0).astype(I32) << jnp.arange(e, dtype=I32)[None, :], axis=1)
    cls = h2aug[:, D_MODEL + REC_CLASS_LANE].astype(I32)
    rank = h2aug[:, D_MODEL + REC_RANK_LANE].astype(I32)
    is_class = cls[:, None] == jnp.arange(N_CLASSES, dtype=I32)[None, :]
    dest = jnp.sum(jnp.where(is_class, class_start[None, :], 0), axis=1) + rank

    xs = _moe_scatter(dest, pad_end, h2aug, n_blocks * GROUP_BLOCK)
    w_gu = jnp.concatenate([w_gate, w_up], axis=2).astype(BF16).reshape(N_GROUPS, e, D_MODEL, 2 * D_EXPERT)
    w_dn = w_down.astype(BF16).reshape(N_GROUPS, e * D_EXPERT, D_MODEL)
    y = _moe_groups(block_group, n_used, block_mask, xs, w_gu, w_dn)
    return _moe_combine(alpha, dest, h2aug, _pad_rows([ln3_g, ln3_b]), y)


def kernel(x, mem, positions, w_in, attn_sinks, conv_w, conv_b, b_igate, b_fgate, ml_norm_g, w_att_branch, w_ml_branch, w_mix_out, ln1_g, ln1_b, w_xq, w_xkv, w_xo, ln2_g, ln2_b, w_router_group, b_router_group, w_router_expert, b_router_expert, w_gate, w_up, w_down, ln3_g, ln3_b):
    batch, seq, d = x.shape
    mem_len = mem.shape[1]
    depth = w_in.shape[0]
    alpha = (2 * depth) ** 0.25
    h = x.reshape(batch * seq, d)
    mem2 = mem.reshape(batch * mem_len, d)
    pos2 = positions.reshape(batch * seq, 1)
    stacked = (w_in, attn_sinks, conv_w, conv_b, b_igate, b_fgate, ml_norm_g, w_att_branch, w_ml_branch,
               w_mix_out, ln1_g, ln1_b, w_xq, w_xkv, w_xo, ln2_g, ln2_b, w_router_group, b_router_group,
               w_router_expert, b_router_expert, w_gate, w_up, w_down, ln3_g, ln3_b)
    for l in range(depth):
        h = _layer(alpha, h, mem2, pos2, batch, seq, mem_len, *(w[l] for w in stacked))
    return h.reshape(batch, seq, d)
```

```python
import functools

import jax
import jax.numpy as jnp
from jax import lax
from jax.experimental import pallas as pl
from jax.experimental.pallas import tpu as pltpu

F32 = jnp.float32
BF16 = jnp.bfloat16
I32 = jnp.int32

D_MODEL = 1024
ATT_HEADS = 16
ATT_KV_HEADS = 2
ATT_HEAD_DIM = 64
ATT_BLOCK = 128
ROPE_THETA = 10000.0
ML_HEADS = 4
ML_DQK = 128
ML_DV = 256
ML_CHUNK = 128
CONV_WIDTH = 4
X_HEADS = 4
X_HEAD_DIM = D_MODEL // X_HEADS
N_GROUPS = 8
EXPERTS_PER_GROUP = 8
N_EXPERTS = N_GROUPS * EXPERTS_PER_GROUP
D_EXPERT = 256
LN_EPS = 1e-5
RMS_EPS = 1e-6

ATT_Q_W = ATT_HEADS * ATT_HEAD_DIM
ATT_KV_W = ATT_KV_HEADS * ATT_HEAD_DIM
ML_QK_W = ML_HEADS * ML_DQK
ML_V_W = ML_HEADS * ML_DV

LANES = 128
SUBLANES = 8
VMEM_LIMIT_BYTES = 56 * 1024 * 1024

ROUTER_EXPERT_LANE0 = N_GROUPS
PAIRS_PER_GROUP = EXPERTS_PER_GROUP * (EXPERTS_PER_GROUP - 1) // 2
N_CLASSES = N_GROUPS * PAIRS_PER_GROUP
CLASS_LANES = 256
REC_CLASS_LANE = 0
REC_RANK_LANE = 1
REC_WEIGHT_LANE0 = 8
ROW_W = D_MODEL + LANES
GROUP_BLOCK = 512
IN_PROJ_TILE = 1024
POST_TILE = 512
SCATTER_TILE = 1024
COMBINE_TILE = 512
LOG2_E = 1.4426950408889634
POST_COLS = 512


def _params(*semantics):
    return pltpu.CompilerParams(dimension_semantics=semantics, vmem_limit_bytes=VMEM_LIMIT_BYTES)


def _const_spec(shape):
    zeros = (0,) * len(shape)
    return pl.BlockSpec(shape, lambda *_: zeros, pipeline_mode=pl.Buffered(1))


def _layer_norm(z, g, b):
    mu = jnp.mean(z, axis=-1, keepdims=True)
    zc = z - mu
    var = jnp.mean(zc * zc, axis=-1, keepdims=True)
    return zc * lax.rsqrt(var + LN_EPS) * g + b


def _inproj_kernel(sink_ref, x_ref, pos_ref, inv_ref, wrope_ref, wv_ref, wmqk_ref, wmv_ref, wg_ref,
                   convw_ref, convb_ref, gbias_ref,
                   att_out, mq_out, mk_out, mv_out, g_out, prev_ref, kvprev_ref):
    t = x_ref.shape[0]
    first = pl.program_id(1) == 0
    xb = x_ref[...].astype(BF16)

    half = ATT_HEAD_DIM // 2
    ang = pos_ref[...].astype(F32) * inv_ref[0:1, :]
    sin_cos = jnp.sin(ang + inv_ref[1:2, :])
    lane = lax.broadcasted_iota(I32, (t, LANES), 1)
    first_half = (lane % ATT_HEAD_DIM) < half
    cos = jnp.where(first_half, pltpu.roll(sin_cos, LANES - half, 1), sin_cos)
    sin_signed = jnp.where(first_half, -sin_cos, pltpu.roll(sin_cos, half, 1))

    def rope(z):
        partner = jnp.where(first_half, pltpu.roll(z, LANES - ATT_HEAD_DIM // 2, 1),
                            pltpu.roll(z, ATT_HEAD_DIM // 2, 1))
        return z * cos + partner * sin_signed

    qk = jnp.dot(xb, wrope_ref[...], preferred_element_type=F32)
    q_scale = ATT_HEAD_DIM ** -0.5 * LOG2_E
    q_groups = [(rope(qk[:, g * LANES:(g + 1) * LANES]) * q_scale).astype(BF16) for g in range(ATT_Q_W // LANES)]
    k_groups = [rope(qk[:, ATT_Q_W + g * LANES:ATT_Q_W + (g + 1) * LANES]).astype(BF16)
                for g in range(ATT_KV_HEADS)]
    v_all = jnp.dot(xb, wv_ref[...], preferred_element_type=F32).astype(BF16)
    v_groups = [v_all[:, g * LANES:(g + 1) * LANES] for g in range(ATT_KV_HEADS)]

    nk = 2 * ATT_BLOCK
    kc = lax.broadcasted_iota(I32, (nk, ATT_BLOCK), 0)
    qi = lax.broadcasted_iota(I32, (nk, ATT_BLOCK), 1)
    band = (kc > qi) & (kc <= qi + ATT_BLOCK)
    lo = lax.broadcasted_iota(I32, (nk, LANES), 1) < ATT_HEAD_DIM
    dim_lo = lax.broadcasted_iota(I32, (LANES, ATT_BLOCK), 0) < ATT_HEAD_DIM
    pairs = ATT_HEADS // ATT_KV_HEADS // 2

    @pl.when(first)
    def _():
        kvprev_ref[...] = jnp.zeros_like(kvprev_ref)

    for j in range(t // ATT_BLOCK):
        cur = slice(j * ATT_BLOCK, (j + 1) * ATT_BLOCK)
        valid = band & (kc >= jnp.where(first, ATT_BLOCK, 0)) if j == 0 else band
        for kvh in range(ATT_KV_HEADS):
            if j == 0:
                k_prev = kvprev_ref[:, kvh * LANES:(kvh + 1) * LANES]
                v_prev = kvprev_ref[:, (ATT_KV_HEADS + kvh) * LANES:(ATT_KV_HEADS + kvh + 1) * LANES]
            else:
                prev = slice((j - 1) * ATT_BLOCK, j * ATT_BLOCK)
                k_prev, v_prev = k_groups[kvh][prev], v_groups[kvh][prev]
            kk = jnp.concatenate([k_prev, k_groups[kvh][cur]], axis=0)
            vv = jnp.concatenate([v_prev, v_groups[kvh][cur]], axis=0).astype(F32)
            zero = jnp.zeros_like(kk)
            k2 = jnp.concatenate([jnp.where(lo, kk, zero), jnp.where(lo, zero, kk)], axis=0)
            v2t = jnp.concatenate([jnp.where(lo, vv, 0.0), jnp.where(lo, 0.0, vv)], axis=0).T.astype(BF16)
            for pp in range(pairs):
                p = kvh * pairs + pp
                qp = q_groups[p][cur]
                st = lax.dot_general(k2, qp, (((1,), (1,)), ((), ())), preferred_element_type=F32)
                e_heads, recip = [], []
                for hh in range(2):
                    sink = sink_ref[2 * p + hh]
                    s_h = jnp.where(valid, st[hh * nk:(hh + 1) * nk], -jnp.inf)
                    m = jnp.maximum(jnp.max(s_h, axis=0, keepdims=True), sink)
                    e = jnp.exp2(s_h - m)
                    den = jnp.sum(e, axis=0, keepdims=True) + jnp.exp2(sink - m)
                    e_heads.append(e.astype(BF16))
                    recip.append(1.0 / den)
                out_t = jnp.dot(v2t, jnp.concatenate(e_heads, axis=0), preferred_element_type=F32)
                out_t = out_t * jnp.where(dim_lo, recip[0], recip[1])
                att_out[cur, p * LANES:(p + 1) * LANES] = out_t.T.astype(BF16)
    last = slice(t - ATT_BLOCK, t)
    kvprev_ref[...] = jnp.concatenate([g[last] for g in k_groups] + [g[last] for g in v_groups], axis=1)

    @pl.when(pl.program_id(1) == 0)
    def _():
        prev_ref[...] = jnp.zeros_like(prev_ref)

    row8 = lax.broadcasted_iota(I32, (SUBLANES, ML_QK_W), 0)
    for out_ref, scale, cs in ((mq_out, 1.0, slice(0, ML_QK_W)), (mk_out, ML_DQK ** -0.5, slice(ML_QK_W, 2 * ML_QK_W))):
        pre = jnp.dot(xb, wmqk_ref[:, cs], preferred_element_type=F32)
        prev8 = prev_ref[:, cs]
        w_now = convw_ref[CONV_WIDTH - 1:CONV_WIDTH, cs]
        acc = pre * w_now + convb_ref[:, cs]
        top = pre[0:SUBLANES] * w_now + convb_ref[:, cs]
        for d in range(1, CONV_WIDTH):
            w_d = convw_ref[CONV_WIDTH - 1 - d:CONV_WIDTH - d, cs]
            shifted = pltpu.roll(pre, d, 0)
            acc = acc + shifted * w_d
            top = top + jnp.where(row8 < d, pltpu.roll(prev8, d, 0), shifted[0:SUBLANES]) * w_d
        prev_ref[:, cs] = pre[t - SUBLANES:t]
        conv = jnp.concatenate([top, acc[SUBLANES:]], axis=0)
        out_ref[...] = (conv * jax.nn.sigmoid(conv) * scale).astype(BF16)

    mv_out[...] = jnp.dot(xb, wmv_ref[...], preferred_element_type=F32).astype(BF16)

    gates = jnp.dot(xb, wg_ref[...], preferred_element_type=F32) + gbias_ref[...]
    g_out[...] = jnp.where(lane < ML_HEADS, gates, jax.nn.log_sigmoid(gates))


def _in_proj(x2, pos2, w_in, sinks, conv_w, conv_b, b_igate, b_fgate, batch, seq):
    n = x2.shape[0]
    t = min(IN_PROJ_TILE, seq)
    s_tiles = seq // t

    def dup(w):
        h = ATT_HEAD_DIM
        return jnp.concatenate([w[:, :h], w[:, :h], w[:, h:], w[:, h:]], axis=1)

    o_k = ATT_Q_W
    o_v = o_k + ATT_KV_W
    o_mq = o_v + ATT_KV_W
    o_mv = o_mq + 2 * ML_QK_W
    o_mo = o_mv + ML_V_W
    o_mi = o_mo + ML_V_W
    w_rope = jnp.concatenate([w_in[:, :o_k], dup(w_in[:, o_k:o_v])], axis=1).astype(BF16)
    w_v = dup(w_in[:, o_v:o_mq]).astype(BF16)
    w_mqk = w_in[:, o_mq:o_mv].astype(BF16)
    w_mv = w_in[:, o_mv:o_mo].astype(BF16)
    w_g = jnp.pad(w_in[:, o_mi:o_mi + 2 * ML_HEADS], ((0, 0), (0, LANES - 2 * ML_HEADS))).astype(BF16)
    gbias = jnp.pad(jnp.concatenate([b_igate, b_fgate]).astype(F32), (0, LANES - 2 * ML_HEADS))[None, :]
    half = ATT_HEAD_DIM // 2
    inv = ROPE_THETA ** (-jnp.arange(half, dtype=F32) / half)
    quarter_turn = jnp.where((jnp.arange(LANES) % ATT_HEAD_DIM) < half, 0.0, jnp.pi / 2).astype(F32)
    inv = jnp.stack([jnp.tile(inv, LANES // half), quarter_turn])

    row = lambda w: pl.BlockSpec((t, w), lambda b, s, *_: (b * s_tiles + s, 0))
    grid_spec = pltpu.PrefetchScalarGridSpec(
        num_scalar_prefetch=1,
        grid=(batch, s_tiles),
        in_specs=[row(D_MODEL), row(1), _const_spec((2, LANES)),
                  _const_spec(w_rope.shape), _const_spec(w_v.shape), _const_spec(w_mqk.shape),
                  _const_spec(w_mv.shape), _const_spec(w_g.shape),
                  _const_spec((CONV_WIDTH, 2 * ML_QK_W)), _const_spec((1, 2 * ML_QK_W)),
                  _const_spec((1, LANES))],
        out_specs=[row(ATT_Q_W), row(ML_QK_W), row(ML_QK_W), row(ML_V_W), row(LANES)],
        scratch_shapes=[pltpu.VMEM((SUBLANES, 2 * ML_QK_W), F32),
                        pltpu.VMEM((ATT_BLOCK, 4 * ATT_KV_W), BF16)],
    )
    return pl.pallas_call(
        _inproj_kernel,
        grid_spec=grid_spec,
        out_shape=[jax.ShapeDtypeStruct((n, ATT_Q_W), BF16),
                   jax.ShapeDtypeStruct((n, ML_QK_W), BF16),
                   jax.ShapeDtypeStruct((n, ML_QK_W), BF16),
                   jax.ShapeDtypeStruct((n, ML_V_W), BF16),
                   jax.ShapeDtypeStruct((n, LANES), F32)],
        compiler_params=_params("arbitrary", "arbitrary"),
        name="in_proj",
    )(sinks.astype(F32) * LOG2_E, x2, pos2, inv, w_rope, w_v, w_mqk, w_mv, w_g, conv_w.astype(F32),
      conv_b.astype(F32)[None, :], gbias)


def _mlstm_reset(c_ref, n_ref, m_ref):
    c_ref[...] = jnp.zeros_like(c_ref)
    n_ref[...] = jnp.zeros_like(n_ref)
    m_ref[...] = jnp.zeros_like(m_ref)


def _mlstm_chunk(rows, q_ref, k_ref, v_ref, g_ref, o_ref, c_ref, n_ref, m_ref):
    nb = q_ref.shape[0]
    L = ML_CHUNK
    causal = lax.broadcasted_iota(I32, (L, L), 1) <= lax.broadcasted_iota(I32, (L, L), 0)
    rowi = lax.broadcasted_iota(I32, (L, LANES), 0)

    def scan_rows(x, op, identity):
        d = 1
        while d < L:
            x = op(x, jnp.where(rowi >= d, pltpu.roll(x, d, 0), identity))
            d *= 2
        return x

    def chunk():
        heads = [(bb, h) for bb in range(nb) for h in range(ML_HEADS)]

        gate = []
        for bb in range(nb):
            g = g_ref[bb, rows, :]
            b = pltpu.roll(scan_rows(g, jnp.add, 0.0), LANES - ML_HEADS, 1)
            a = g - b
            a_max = scan_rows(a, jnp.maximum, -jnp.inf)
            m_prev = m_ref[bb, 0:1, :]
            m_t = b + jnp.maximum(m_prev, a_max)
            u = b - m_t
            b_end = b[L - 1:L, :]
            m_new = b_end + jnp.maximum(m_prev, a_max[L - 1:L, :])
            m_ref[bb, 0:1, :] = m_new
            gate.append(dict(u=u, w_inter=jnp.exp(u + m_prev), floor=jnp.exp(-m_t),
                             w_state=jnp.exp(b_end + m_prev - m_new), w_k=jnp.exp(b_end + a - m_new),
                             a_t=a.T))

        nt = (((1,), (1,)), ((), ()))
        for bb, h in heads:
            gt = gate[bb]
            idx = bb * ML_HEADS + h
            col = slice(h, h + 1)
            qh = q_ref[bb, rows, h * ML_DQK:(h + 1) * ML_DQK]
            kh = k_ref[bb, rows, h * ML_DQK:(h + 1) * ML_DQK]
            vh = v_ref[bb, rows, h * ML_DV:(h + 1) * ML_DV]
            c_prev = c_ref[idx]
            n_prev = n_ref[idx, 0:1, :]

            w_intra = jnp.exp(jnp.where(causal, gt["u"][:, col] + gt["a_t"][col, :], -jnp.inf))
            s = lax.dot_general(qh, kh, nt, preferred_element_type=F32) * w_intra
            w_inter = gt["w_inter"][:, col]
            num = (w_inter * jnp.dot(qh, c_prev.astype(BF16), preferred_element_type=F32)
                   + jnp.dot(s.astype(BF16), vh, preferred_element_type=F32))
            den = (w_inter * jnp.sum(qh.astype(F32) * n_prev, axis=1, keepdims=True)
                   + jnp.sum(s, axis=1, keepdims=True))
            hval = num / jnp.maximum(jnp.abs(den), gt["floor"][:, col])
            hval = hval * lax.rsqrt(jnp.mean(hval * hval, axis=1, keepdims=True) + RMS_EPS)
            o_ref[bb, rows, h * ML_DV:(h + 1) * ML_DV] = hval.astype(BF16)

            w_state = gt["w_state"][:, col]
            kw = kh.astype(F32) * gt["w_k"][:, col]
            c_ref[idx] = w_state * c_prev + jnp.dot(kw.T.astype(BF16), vh, preferred_element_type=F32)
            n_ref[idx, 0:1, :] = w_state * n_prev + jnp.sum(kw, axis=0, keepdims=True)

    chunk()


def _memkv_kernel(mem_ref, w_ref, k_out, v_out):
    kv = jnp.dot(mem_ref[...].astype(BF16), w_ref[...], preferred_element_type=F32)
    k_out[...] = kv[:, :D_MODEL].astype(BF16)
    v_out[...] = kv[:, D_MODEL:].astype(BF16)


def _mem_kv(mem2, w_xkv, batch, mem_len):
    blk = pl.BlockSpec((mem_len, D_MODEL), lambda b: (b, 0))
    return pl.pallas_call(
        _memkv_kernel,
        grid=(batch,),
        in_specs=[blk, _const_spec((D_MODEL, 2 * D_MODEL))],
        out_specs=[blk, blk],
        out_shape=[jax.ShapeDtypeStruct((batch * mem_len, D_MODEL), BF16)] * 2,
        compiler_params=_params("arbitrary"),
        name="mem_kv",
    )(mem2, w_xkv.astype(BF16))


def _post_kernel(alpha, x_ref, att_ref, mq_ref, mk_ref, mv_ref, g_ref, kx_ref, vx_ref, wgate_ref, watt_ref,
                 wml_ref, wmix_ref, wxq_ref, wxo_ref, wr_hi_ref, wr_lo_ref, vec_ref, rbias_ref,
                 h_out, count_out, base_ref, c_ref, n_ref, m_ref, hm_ref):
    t = x_ref.shape[0]
    first = (pl.program_id(0) == 0) & (pl.program_id(1) == 0)

    @pl.when(first)
    def _():
        base_ref[...] = jnp.zeros_like(base_ref)

    @pl.when(pl.program_id(1) == 0)
    def _():
        _mlstm_reset(c_ref, n_ref, m_ref)

    for c in range(t // ML_CHUNK):
        _mlstm_chunk(slice(c * ML_CHUNK, (c + 1) * ML_CHUNK), mq_ref, mk_ref, mv_ref, g_ref, hm_ref,
                     c_ref, n_ref, m_ref)

    x = x_ref[...]
    xb = x.astype(BF16)
    norm_g, ln1_g, ln1_b, ln2_g, ln2_b = (vec_ref[i:i + 1, :] for i in range(5))

    chunks = [slice(c, c + POST_COLS) for c in range(0, D_MODEL, POST_COLS)]
    hm = jnp.concatenate(
        [(hm_ref[0, :, cs].astype(F32) * norm_g[:, cs]
          * jax.nn.sigmoid(jnp.dot(xb, wgate_ref[:, cs], preferred_element_type=F32))).astype(BF16)
         for cs in chunks], axis=1)
    att = att_ref[...]
    y = []
    for cs in chunks:
        gm = jnp.dot(xb, wgate_ref[:, 2 * D_MODEL + cs.start:2 * D_MODEL + cs.stop], preferred_element_type=F32)
        ga = jnp.dot(xb, wgate_ref[:, D_MODEL + cs.start:D_MODEL + cs.stop], preferred_element_type=F32)
        m_out = jnp.dot(hm, wml_ref[:, cs], preferred_element_type=F32)
        a_out = jnp.dot(att, watt_ref[:, cs], preferred_element_type=F32)
        y.append((jax.nn.sigmoid(gm) * m_out + jax.nn.sigmoid(ga) * a_out).astype(BF16))
    mix = jnp.dot(jnp.concatenate(y, axis=1), wmix_ref[...], preferred_element_type=F32)
    h1 = _layer_norm(alpha * x + mix, ln1_g, ln1_b)

    q = (jnp.dot(h1.astype(BF16), wxq_ref[...], preferred_element_type=F32) * (X_HEAD_DIM ** -0.5)).astype(BF16)
    heads = []
    for h in range(X_HEADS):
        sl = slice(h * X_HEAD_DIM, (h + 1) * X_HEAD_DIM)
        sc = lax.dot_general(q[:, sl], kx_ref[:, sl], (((1,), (1,)), ((), ())), preferred_element_type=F32)
        e = jnp.exp(sc - jnp.max(sc, axis=1, keepdims=True))
        probs = (e * (1.0 / jnp.sum(e, axis=1, keepdims=True))).astype(BF16)
        heads.append(jnp.dot(probs, vx_ref[:, sl], preferred_element_type=F32).astype(BF16))
    xa = jnp.dot(jnp.concatenate(heads, axis=1), wxo_ref[...], preferred_element_type=F32)
    h2 = _layer_norm(alpha * h1 + xa, ln2_g, ln2_b)
    h_out[:, :D_MODEL] = h2

    h_hi = h2.astype(BF16)
    h_lo = (h2 - h_hi.astype(F32)).astype(BF16)
    logits = (jnp.dot(h_hi, wr_hi_ref[...], preferred_element_type=F32)
              + jnp.dot(h_lo, wr_hi_ref[...], preferred_element_type=F32)
              + jnp.dot(h_hi, wr_lo_ref[...], preferred_element_type=F32)) + rbias_ref[...]

    lane = lax.broadcasted_iota(I32, (t, LANES), 1)
    neg = -jnp.inf
    is_group = lane < N_GROUPS
    gl = jnp.where(is_group, logits, neg)
    g_max = jnp.max(gl, axis=1, keepdims=True)
    lane_f = lane.astype(F32)
    no_lane = float(LANES)
    g_sel = jnp.min(jnp.where(gl == g_max, lane_f, no_lane), axis=1, keepdims=True)
    p_group = 1.0 / jnp.sum(jnp.where(is_group, jnp.exp(logits - g_max), 0.0), axis=1, keepdims=True)
    lane_group = ((lane - ROUTER_EXPERT_LANE0) >> (EXPERTS_PER_GROUP.bit_length() - 1)).astype(F32)
    in_group = (lane >= ROUTER_EXPERT_LANE0) & (lane_group == g_sel)
    el = jnp.where(in_group, logits, neg)
    v1 = jnp.max(el, axis=1, keepdims=True)
    i1 = jnp.min(jnp.where(el == v1, lane_f, no_lane), axis=1, keepdims=True)
    el2 = jnp.where(lane_f == i1, neg, el)
    v2 = jnp.max(el2, axis=1, keepdims=True)
    i2 = jnp.min(jnp.where(el2 == v2, lane_f, no_lane), axis=1, keepdims=True)
    e2 = jnp.exp(v2 - v1)
    w1 = p_group * (1.0 / (1.0 + e2))
    w2 = p_group * (e2 / (1.0 + e2))

    first_logit = ROUTER_EXPERT_LANE0 + EXPERTS_PER_GROUP * g_sel
    lo = jnp.minimum(i1, i2) - first_logit
    hi = jnp.maximum(i1, i2) - first_logit
    lo_odd = lo - 2.0 * jnp.floor(0.5 * lo)
    in_row = jnp.where(lo_odd == 1.0, (EXPERTS_PER_GROUP - 1) - hi, hi - lo - 1.0)
    cls = PAIRS_PER_GROUP * g_sel + lo * (2 * EXPERTS_PER_GROUP - 1 - lo) * 0.5 + in_row
    class_lane = lax.broadcasted_iota(I32, (t, CLASS_LANES), 1).astype(F32)
    onehot = jnp.where(class_lane == cls, 1.0, 0.0)
    tri = jnp.where(lax.broadcasted_iota(I32, (t, t), 1) < lax.broadcasted_iota(I32, (t, t), 0), 1.0, 0.0).astype(BF16)
    before = jnp.dot(tri, onehot.astype(BF16), preferred_element_type=F32) + base_ref[0:1, :]
    rank = jnp.sum(jnp.where(class_lane == cls, before, 0.0), axis=1, keepdims=True)
    total = base_ref[0:1, :] + jnp.sum(onehot, axis=0, keepdims=True)
    base_ref[...] = jnp.broadcast_to(total, base_ref.shape)
    count_out[...] = jnp.broadcast_to(total, count_out.shape)

    logit_lane = lane_f + (ROUTER_EXPERT_LANE0 - REC_WEIGHT_LANE0) + EXPERTS_PER_GROUP * g_sel
    rec = jnp.where(logit_lane == i1, w1, jnp.where(logit_lane == i2, w2, 0.0))
    rec = jnp.where((lane >= REC_WEIGHT_LANE0) & (lane < REC_WEIGHT_LANE0 + EXPERTS_PER_GROUP), rec, 0.0)
    rec = jnp.where(lane == REC_CLASS_LANE, cls, rec)
    rec = jnp.where(lane == REC_RANK_LANE, rank, rec)
    h_out[:, D_MODEL:] = rec


def _post(alpha, x2, att, mq, mk, mv, gates, kx, vx, w_gates, w_att, w_ml, w_mix, w_xq, w_xo, w_router, b_router,
          vecs, batch, seq, mem_len):
    n = x2.shape[0]
    t = min(POST_TILE, seq)
    s_tiles = seq // t
    row = lambda w: pl.BlockSpec((t, w), lambda b, s: (b * s_tiles + s, 0))
    memblk = pl.BlockSpec((mem_len, D_MODEL), lambda b, s: (b, 0))
    seq_blk = lambda w: pl.BlockSpec((1, t, w), lambda b, s: (b, s, 0))
    seqs = lambda a: a.reshape(batch, seq, a.shape[-1])
    w_hi = w_router.astype(BF16)
    w_lo = (w_router - w_hi.astype(F32)).astype(BF16)
    sq = _const_spec((D_MODEL, D_MODEL))
    return pl.pallas_call(
        functools.partial(_post_kernel, alpha),
        grid=(batch, s_tiles),
        in_specs=[row(D_MODEL), row(ATT_Q_W), seq_blk(ML_QK_W), seq_blk(ML_QK_W), seq_blk(ML_V_W), seq_blk(LANES),
                  memblk, memblk,
                  _const_spec((D_MODEL, 3 * D_MODEL)), sq, sq, sq, sq, sq,
                  _const_spec((D_MODEL, LANES)), _const_spec((D_MODEL, LANES)),
                  _const_spec((SUBLANES, D_MODEL)), _const_spec((1, LANES))],
        out_specs=[row(ROW_W), _const_spec((SUBLANES, CLASS_LANES))],
        out_shape=[jax.ShapeDtypeStruct((n, ROW_W), F32),
                   jax.ShapeDtypeStruct((SUBLANES, CLASS_LANES), F32)],
        scratch_shapes=[pltpu.VMEM((SUBLANES, CLASS_LANES), F32),
                        pltpu.VMEM((ML_HEADS, ML_DQK, ML_DV), F32),
                        pltpu.VMEM((ML_HEADS, SUBLANES, ML_DQK), F32),
                        pltpu.VMEM((1, SUBLANES, LANES), F32),
                        pltpu.VMEM((1, t, ML_V_W), BF16)],
        compiler_params=_params("arbitrary", "arbitrary"),
        name="post",
    )(x2, att, seqs(mq), seqs(mk), seqs(mv), seqs(gates), kx, vx, w_gates, w_att, w_ml, w_mix, w_xq, w_xo, w_hi, w_lo, vecs, b_router)


DMA_UNROLL = 32


def _scatter_kernel(dest_ref, padend_ref, h_ref, xs_out, zero_buf, sem, zero_sem):
    i = pl.program_id(0)
    ts = h_ref.shape[0]

    @pl.when(i == 0)
    def _():
        zero_buf[...] = jnp.zeros_like(zero_buf)

        def clear(start):
            start = pl.multiple_of(start, GROUP_BLOCK)
            return pltpu.make_async_copy(zero_buf, xs_out.at[pl.ds(start, GROUP_BLOCK), :], zero_sem)

        def nonempty(g):
            return padend_ref[g] > (padend_ref[g - 1] if g else 0)

        used = padend_ref[N_GROUPS - 1]

        def unused(b):
            return used + b * GROUP_BLOCK < xs_out.shape[0]

        for phase in ("start", "wait"):
            for g in range(N_GROUPS):
                @pl.when(nonempty(g))
                def _():
                    getattr(clear(padend_ref[g] - GROUP_BLOCK), phase)()

                @pl.when(unused(g))
                def _():
                    getattr(clear(used + g * GROUP_BLOCK), phase)()

    def issue(j, carry):
        for u in range(DMA_UNROLL):
            t = j * DMA_UNROLL + u
            pltpu.make_async_copy(h_ref.at[pl.ds(t, 1), :], xs_out.at[pl.ds(dest_ref[i * ts + t], 1), :], sem).start()
        return carry

    lax.fori_loop(0, ts // DMA_UNROLL, issue, 0)

    pltpu.make_async_copy(h_ref, xs_out.at[pl.ds(0, ts), :], sem).wait()


def _moe_scatter(dest, pad_end, h2aug, rows):
    n = h2aug.shape[0]
    ts = min(SCATTER_TILE, n)
    grid_spec = pltpu.PrefetchScalarGridSpec(
        num_scalar_prefetch=2,
        grid=(n // ts,),
        in_specs=[pl.BlockSpec((ts, ROW_W), lambda i, *_: (i, 0))],
        out_specs=pl.BlockSpec(memory_space=pl.ANY),
        scratch_shapes=[pltpu.VMEM((GROUP_BLOCK, ROW_W), F32),
                        pltpu.SemaphoreType.DMA(()), pltpu.SemaphoreType.DMA(())],
    )
    return pl.pallas_call(
        _scatter_kernel,
        grid_spec=grid_spec,
        out_shape=jax.ShapeDtypeStruct((rows, ROW_W), F32),
        compiler_params=_params("arbitrary"),
        name="moe_scatter",
    )(dest, pad_end, h2aug)


def _group_kernel(bgrp_ref, nused_ref, bmask_ref, x_ref, wgu_ref, wd_ref, y_ref, xb_ref):
    del bgrp_ref
    i = pl.program_id(0)
    y_ref[...] = jnp.zeros_like(y_ref)

    @pl.when(i < nused_ref[0])
    def _():
        xb_ref[...] = x_ref[:, :D_MODEL].astype(BF16)
        mask = bmask_ref[i]
        for e in range(EXPERTS_PER_GROUP):
            @pl.when(((mask >> e) & 1) == 1)
            def _():
                gu = jnp.dot(xb_ref[...], wgu_ref[0, e], preferred_element_type=F32)
                gate = gu[:, :D_EXPERT]
                lane = D_MODEL + REC_WEIGHT_LANE0 + e
                w = x_ref[:, lane:lane + 1]
                hw = jnp.where(w != 0.0, w * (gate * jax.nn.sigmoid(gate) * gu[:, D_EXPERT:]), 0.0)
                y_ref[...] += jnp.dot(hw.astype(BF16), wd_ref[0, e * D_EXPERT:(e + 1) * D_EXPERT, :],
                                      preferred_element_type=F32)


def _moe_groups(block_group, n_used, block_mask, xs, w_gu, w_down):
    rows = xs.shape[0]
    e = EXPERTS_PER_GROUP
    grid_spec = pltpu.PrefetchScalarGridSpec(
        num_scalar_prefetch=3,
        grid=(rows // GROUP_BLOCK,),
        in_specs=[pl.BlockSpec((GROUP_BLOCK, ROW_W), lambda i, *_: (i, 0)),
                  pl.BlockSpec((1, e, D_MODEL, 2 * D_EXPERT), lambda i, bg, *_: (bg[i], 0, 0, 0)),
                  pl.BlockSpec((1, e * D_EXPERT, D_MODEL), lambda i, bg, *_: (bg[i], 0, 0))],
        out_specs=pl.BlockSpec((GROUP_BLOCK, D_MODEL), lambda i, *_: (i, 0)),
        scratch_shapes=[pltpu.VMEM((GROUP_BLOCK, D_MODEL), BF16)],
    )
    return pl.pallas_call(
        _group_kernel,
        grid_spec=grid_spec,
        out_shape=jax.ShapeDtypeStruct((rows, D_MODEL), F32),
        compiler_params=_params("arbitrary"),
        name="moe_groups",
    )(block_group, n_used, block_mask, xs, w_gu, w_down)


def _combine_kernel(alpha, dest_ref, h_ref, vec_ref, y_hbm, o_ref, ybuf, sems):
    tc = h_ref.shape[0]
    i = pl.program_id(0)
    steps = pl.num_programs(0)

    def gather(tile, slot):
        def issue(j, carry):
            for u in range(DMA_UNROLL):
                t = j * DMA_UNROLL + u
                pltpu.make_async_copy(y_hbm.at[pl.ds(dest_ref[tile * tc + t], 1), :],
                                      ybuf.at[slot, pl.ds(t, 1), :], sems.at[slot]).start()
            return carry
        lax.fori_loop(0, tc // DMA_UNROLL, issue, 0)

    @pl.when(i == 0)
    def _():
        gather(0, 0)

    @pl.when(i + 1 < steps)
    def _():
        gather(i + 1, (i + 1) % 2)

    slot = i % 2
    pltpu.make_async_copy(y_hbm.at[pl.ds(0, tc), :], ybuf.at[slot], sems.at[slot]).wait()
    o_ref[...] = _layer_norm(alpha * h_ref[...] + ybuf[slot], vec_ref[0:1, :], vec_ref[1:2, :])


def _moe_combine(alpha, dest, h2aug, vecs, y):
    n = h2aug.shape[0]
    tc = min(COMBINE_TILE, n)
    grid_spec = pltpu.PrefetchScalarGridSpec(
        num_scalar_prefetch=1,
        grid=(n // tc,),
        in_specs=[pl.BlockSpec((tc, D_MODEL), lambda i, *_: (i, 0)),
                  pl.BlockSpec((SUBLANES, D_MODEL), lambda i, *_: (0, 0)),
                  pl.BlockSpec(memory_space=pl.ANY)],
        out_specs=pl.BlockSpec((tc, D_MODEL), lambda i, *_: (i, 0)),
        scratch_shapes=[pltpu.VMEM((2, tc, D_MODEL), F32), pltpu.SemaphoreType.DMA((2,))],
    )
    return pl.pallas_call(
        functools.partial(_combine_kernel, alpha),
        grid_spec=grid_spec,
        out_shape=jax.ShapeDtypeStruct((n, D_MODEL), F32),
        compiler_params=_params("arbitrary"),
        name="moe_combine",
    )(dest, h2aug, vecs, y)


def _pad_rows(vectors):
    rows = [v.astype(F32)[None, :] for v in vectors]
    rows.append(jnp.zeros((SUBLANES - len(rows), vectors[0].shape[0]), F32))
    return jnp.concatenate(rows, axis=0)


def _layer(alpha, h, mem2, pos2, batch, seq, mem_len, w_in, attn_sinks, conv_w, conv_b, b_igate, b_fgate,
           ml_norm_g, w_att_branch, w_ml_branch, w_mix_out, ln1_g, ln1_b, w_xq, w_xkv, w_xo, ln2_g, ln2_b,
           w_router_group, b_router_group, w_router_expert, b_router_expert, w_gate, w_up, w_down,
           ln3_g, ln3_b):
    n = h.shape[0]
    att, mq, mk, mv, gates = _in_proj(h, pos2, w_in, attn_sinks, conv_w, conv_b, b_igate, b_fgate, batch, seq)
    kx, vx = _mem_kv(mem2, w_xkv, batch, mem_len)

    o_mo = ATT_Q_W + 2 * ATT_KV_W + 2 * ML_QK_W + ML_V_W
    o_ga = o_mo + ML_V_W + 2 * ML_HEADS
    w_gates = jnp.concatenate([w_in[:, o_mo:o_mo + ML_V_W], w_in[:, o_ga:]], axis=1).astype(BF16)
    pad = LANES - N_GROUPS - N_EXPERTS
    w_router = jnp.pad(jnp.concatenate([w_router_group, w_router_expert], axis=1).astype(F32), ((0, 0), (0, pad)))
    b_router = jnp.pad(jnp.concatenate([b_router_group, b_router_expert]).astype(F32), (0, pad))[None, :]
    vecs = _pad_rows([ml_norm_g, ln1_g, ln1_b, ln2_g, ln2_b])
    h2aug, counts = _post(alpha, h, att, mq, mk, mv, gates, kx, vx, w_gates, w_att_branch.astype(BF16),
                          w_ml_branch.astype(BF16), w_mix_out.astype(BF16), w_xq.astype(BF16),
                          w_xo.astype(BF16), w_router, b_router, vecs, batch, seq, mem_len)

    e = EXPERTS_PER_GROUP
    class_count = counts[0, :N_CLASSES].astype(I32)
    group_count = jnp.sum(class_count.reshape(N_GROUPS, PAIRS_PER_GROUP), axis=1)
    padded = ((group_count + GROUP_BLOCK - 1) // GROUP_BLOCK) * GROUP_BLOCK
    pad_end = jnp.cumsum(padded).astype(I32)
    pad_start = pad_end - padded
    in_group = jnp.cumsum(class_count.reshape(N_GROUPS, PAIRS_PER_GROUP), axis=1) - class_count.reshape(N_GROUPS, -1)
    class_start = (pad_start[:, None] + in_group).reshape(N_CLASSES)
    class_end = class_start + class_count
    n_blocks = n // GROUP_BLOCK + N_GROUPS
    block_start = jnp.arange(n_blocks, dtype=I32) * GROUP_BLOCK
    block_group = jnp.sum((pad_end[None, :] <= block_start[:, None]).astype(I32), axis=1)
    block_group = jnp.minimum(block_group, N_GROUPS - 1)
    n_used = pad_end[-1:] // GROUP_BLOCK
    pairs = [(lo, hi) for lo in range(e) for hi in list(range(lo + 1, e))[::1 - 2 * (lo % 2)]]
    uses = jnp.array([[int(x in p) for x in range(e)] for p in pairs] * N_GROUPS, I32)
    overlap = ((class_start[None, :] < block_start[:, None] + GROUP_BLOCK) & (class_end[None, :] > block_start[:, None])
               & (class_count[None, :] > 0)).astype(I32)
    block_uses = jnp.sum(overlap[:, :, None] * uses[None, :, :], axis=1)
    block_mask = jnp.sum((block_uses > 0).astype(I32) << jnp.arange(e, dtype=I32)[None, :], axis=1)
    cls = h2aug[:, D_MODEL + REC_CLASS_LANE].astype(I32)
    rank = h2aug[:, D_MODEL + REC_RANK_LANE].astype(I32)
    is_class = cls[:, None] == jnp.arange(N_CLASSES, dtype=I32)[None, :]
    dest = jnp.sum(jnp.where(is_class, class_start[None, :], 0), axis=1) + rank

    xs = _moe_scatter(dest, pad_end, h2aug, n_blocks * GROUP_BLOCK)
    w_gu = jnp.concatenate([w_gate, w_up], axis=2).astype(BF16).reshape(N_GROUPS, e, D_MODEL, 2 * D_EXPERT)
    w_dn = w_down.astype(BF16).reshape(N_GROUPS, e * D_EXPERT, D_MODEL)
    y = _moe_groups(block_group, n_used, block_mask, xs, w_gu, w_dn)
    return _moe_combine(alpha, dest, h2aug, _pad_rows([ln3_g, ln3_b]), y)


def kernel(x, mem, positions, w_in, attn_sinks, conv_w, conv_b, b_igate, b_fgate, ml_norm_g, w_att_branch, w_ml_branch, w_mix_out, ln1_g, ln1_b, w_xq, w_xkv, w_xo, ln2_g, ln2_b, w_router_group, b_router_group, w_router_expert, b_router_expert, w_gate, w_up, w_down, ln3_g, ln3_b):
    batch, seq, d = x.shape
    mem_len = mem.shape[1]
    depth = w_in.shape[0]
    alpha = (2 * depth) ** 0.25
    h = x.reshape(batch * seq, d)
    mem2 = mem.reshape(batch * mem_len, d)
    pos2 = positions.reshape(batch * seq, 1)
    stacked = (w_in, attn_sinks, conv_w, conv_b, b_igate, b_fgate, ml_norm_g, w_att_branch, w_ml_branch,
               w_mix_out, ln1_g, ln1_b, w_xq, w_xkv, w_xo, ln2_g, ln2_b, w_router_group, b_router_group,
               w_router_expert, b_router_expert, w_gate, w_up, w_down, ln3_g, ln3_b)
    for l in range(depth):
        h = _layer(alpha, h, mem2, pos2, batch, seq, mem_len, *(w[l] for w in stacked))
    return h.reshape(batch, seq, d)
```

```python
import functools

import jax
import jax.numpy as jnp
from jax import lax
from jax.experimental import pallas as pl
from jax.experimental.pallas import tpu as pltpu

F32 = jnp.float32
BF16 = jnp.bfloat16
I32 = jnp.int32

D_MODEL = 1024
ATT_HEADS = 16
ATT_KV_HEADS = 2
ATT_HEAD_DIM = 64
ATT_BLOCK = 128
ROPE_THETA = 10000.0
ML_HEADS = 4
ML_DQK = 128
ML_DV = 256
ML_CHUNK = 128
CONV_WIDTH = 4
X_HEADS = 4
X_HEAD_DIM = D_MODEL // X_HEADS
N_GROUPS = 8
EXPERTS_PER_GROUP = 8
N_EXPERTS = N_GROUPS * EXPERTS_PER_GROUP
D_EXPERT = 256
LN_EPS = 1e-5
RMS_EPS = 1e-6

ATT_Q_W = ATT_HEADS * ATT_HEAD_DIM
ATT_KV_W = ATT_KV_HEADS * ATT_HEAD_DIM
ML_QK_W = ML_HEADS * ML_DQK
ML_V_W = ML_HEADS * ML_DV

LANES = 128
SUBLANES = 8
VMEM_LIMIT_BYTES = 56 * 1024 * 1024

ROUTER_EXPERT_LANE0 = N_GROUPS
PAIRS_PER_GROUP = EXPERTS_PER_GROUP * (EXPERTS_PER_GROUP - 1) // 2
N_CLASSES = N_GROUPS * PAIRS_PER_GROUP
CLASS_LANES = 256
REC_CLASS_LANE = 0
REC_RANK_LANE = 1
REC_WEIGHT_LANE0 = 8
ROW_W = D_MODEL + LANES
GROUP_BLOCK = 512
IN_PROJ_TILE = 1024
POST_TILE = 512
SCATTER_TILE = 1024
COMBINE_TILE = 512
LOG2_E = 1.4426950408889634
POST_COLS = 512


def _params(*semantics):
    return pltpu.CompilerParams(dimension_semantics=semantics, vmem_limit_bytes=VMEM_LIMIT_BYTES)


def _const_spec(shape):
    zeros = (0,) * len(shape)
    return pl.BlockSpec(shape, lambda *_: zeros, pipeline_mode=pl.Buffered(1))


def _layer_norm(z, g, b):
    mu = jnp.mean(z, axis=-1, keepdims=True)
    zc = z - mu
    var = jnp.mean(zc * zc, axis=-1, keepdims=True)
    return zc * lax.rsqrt(var + LN_EPS) * g + b


def _inproj_kernel(sink_ref, x_ref, pos_ref, inv_ref, wrope_ref, wv_ref, wmqk_ref, wmv_ref, wg_ref,
                   convw_ref, convb_ref, gbias_ref,
                   att_out, mq_out, mk_out, mv_out, g_out, prev_ref, kvprev_ref):
    t = x_ref.shape[0]
    first = pl.program_id(1) == 0
    xb = x_ref[...].astype(BF16)

    half = ATT_HEAD_DIM // 2
    ang = pos_ref[...].astype(F32) * inv_ref[0:1, :]
    sin_cos = jnp.sin(ang + inv_ref[1:2, :])
    lane = lax.broadcasted_iota(I32, (t, LANES), 1)
    first_half = (lane % ATT_HEAD_DIM) < half
    cos = jnp.where(first_half, pltpu.roll(sin_cos, LANES - half, 1), sin_cos)
    sin_signed = jnp.where(first_half, -sin_cos, pltpu.roll(sin_cos, half, 1))

    def rope(z):
        partner = jnp.where(first_half, pltpu.roll(z, LANES - ATT_HEAD_DIM // 2, 1),
                            pltpu.roll(z, ATT_HEAD_DIM // 2, 1))
        return z * cos + partner * sin_signed

    qk = jnp.dot(xb, wrope_ref[...], preferred_element_type=F32)
    q_scale = ATT_HEAD_DIM ** -0.5 * LOG2_E
    q_groups = [(rope(qk[:, g * LANES:(g + 1) * LANES]) * q_scale).astype(BF16) for g in range(ATT_Q_W // LANES)]
    k_groups = [rope(qk[:, ATT_Q_W + g * LANES:ATT_Q_W + (g + 1) * LANES]).astype(BF16)
                for g in range(ATT_KV_HEADS)]
    v_all = jnp.dot(xb, wv_ref[...], preferred_element_type=F32).astype(BF16)
    v_groups = [v_all[:, g * LANES:(g + 1) * LANES] for g in range(ATT_KV_HEADS)]

    nk = 2 * ATT_BLOCK
    kc = lax.broadcasted_iota(I32, (nk, ATT_BLOCK), 0)
    qi = lax.broadcasted_iota(I32, (nk, ATT_BLOCK), 1)
    band = (kc > qi) & (kc <= qi + ATT_BLOCK)
    lo = lax.broadcasted_iota(I32, (nk, LANES), 1) < ATT_HEAD_DIM
    dim_lo = lax.broadcasted_iota(I32, (LANES, ATT_BLOCK), 0) < ATT_HEAD_DIM
    pairs = ATT_HEADS // ATT_KV_HEADS // 2

    @pl.when(first)
    def _():
        kvprev_ref[...] = jnp.zeros_like(kvprev_ref)

    for j in range(t // ATT_BLOCK):
        cur = slice(j * ATT_BLOCK, (j + 1) * ATT_BLOCK)
        valid = band & (kc >= jnp.where(first, ATT_BLOCK, 0)) if j == 0 else band
        for kvh in range(ATT_KV_HEADS):
            if j == 0:
                k_prev = kvprev_ref[:, kvh * LANES:(kvh + 1) * LANES]
                v_prev = kvprev_ref[:, (ATT_KV_HEADS + kvh) * LANES:(ATT_KV_HEADS + kvh + 1) * LANES]
            else:
                prev = slice((j - 1) * ATT_BLOCK, j * ATT_BLOCK)
                k_prev, v_prev = k_groups[kvh][prev], v_groups[kvh][prev]
            kk = jnp.concatenate([k_prev, k_groups[kvh][cur]], axis=0)
            vv = jnp.concatenate([v_prev, v_groups[kvh][cur]], axis=0).astype(F32)
            zero = jnp.zeros_like(kk)
            k2 = jnp.concatenate([jnp.where(lo, kk, zero), jnp.where(lo, zero, kk)], axis=0)
            v2t = jnp.concatenate([jnp.where(lo, vv, 0.0), jnp.where(lo, 0.0, vv)], axis=0).T.astype(BF16)
            for pp in range(pairs):
                p = kvh * pairs + pp
                qp = q_groups[p][cur]
                st = lax.dot_general(k2, qp, (((1,), (1,)), ((), ())), preferred_element_type=F32)
                e_heads, recip = [], []
                for hh in range(2):
                    sink = sink_ref[2 * p + hh]
                    s_h = jnp.where(valid, st[hh * nk:(hh + 1) * nk], -jnp.inf)
                    m = jnp.maximum(jnp.max(s_h, axis=0, keepdims=True), sink)
                    e = jnp.exp2(s_h - m)
                    den = jnp.sum(e, axis=0, keepdims=True) + jnp.exp2(sink - m)
                    e_heads.append(e.astype(BF16))
                    recip.append(1.0 / den)
                out_t = jnp.dot(v2t, jnp.concatenate(e_heads, axis=0), preferred_element_type=F32)
                out_t = out_t * jnp.where(dim_lo, recip[0], recip[1])
                att_out[cur, p * LANES:(p + 1) * LANES] = out_t.T.astype(BF16)
    last = slice(t - ATT_BLOCK, t)
    kvprev_ref[...] = jnp.concatenate([g[last] for g in k_groups] + [g[last] for g in v_groups], axis=1)

    @pl.when(pl.program_id(1) == 0)
    def _():
        prev_ref[...] = jnp.zeros_like(prev_ref)

    row8 = lax.broadcasted_iota(I32, (SUBLANES, ML_QK_W), 0)
    for out_ref, scale, cs in ((mq_out, 1.0, slice(0, ML_QK_W)), (mk_out, ML_DQK ** -0.5, slice(ML_QK_W, 2 * ML_QK_W))):
        pre = jnp.dot(xb, wmqk_ref[:, cs], preferred_element_type=F32)
        prev8 = prev_ref[:, cs]
        w_now = convw_ref[CONV_WIDTH - 1:CONV_WIDTH, cs]
        acc = pre * w_now + convb_ref[:, cs]
        top = pre[0:SUBLANES] * w_now + convb_ref[:, cs]
        for d in range(1, CONV_WIDTH):
            w_d = convw_ref[CONV_WIDTH - 1 - d:CONV_WIDTH - d, cs]
            shifted = pltpu.roll(pre, d, 0)
            acc = acc + shifted * w_d
            top = top + jnp.where(row8 < d, pltpu.roll(prev8, d, 0), shifted[0:SUBLANES]) * w_d
        prev_ref[:, cs] = pre[t - SUBLANES:t]
        conv = jnp.concatenate([top, acc[SUBLANES:]], axis=0)
        out_ref[...] = (conv * jax.nn.sigmoid(conv) * scale).astype(BF16)

    mv_out[...] = jnp.dot(xb, wmv_ref[...], preferred_element_type=F32).astype(BF16)

    gates = jnp.dot(xb, wg_ref[...], preferred_element_type=F32) + gbias_ref[...]
    g_out[...] = jnp.where(lane < ML_HEADS, gates, jax.nn.log_sigmoid(gates))


def _in_proj(x2, pos2, w_in, sinks, conv_w, conv_b, b_igate, b_fgate, batch, seq):
    n = x2.shape[0]
    t = min(IN_PROJ_TILE, seq)
    s_tiles = seq // t

    def dup(w):
        h = ATT_HEAD_DIM
        return jnp.concatenate([w[:, :h], w[:, :h], w[:, h:], w[:, h:]], axis=1)

    o_k = ATT_Q_W
    o_v = o_k + ATT_KV_W
    o_mq = o_v + ATT_KV_W
    o_mv = o_mq + 2 * ML_QK_W
    o_mo = o_mv + ML_V_W
    o_mi = o_mo + ML_V_W
    w_rope = jnp.concatenate([w_in[:, :o_k], dup(w_in[:, o_k:o_v])], axis=1).astype(BF16)
    w_v = dup(w_in[:, o_v:o_mq]).astype(BF16)
    w_mqk = w_in[:, o_mq:o_mv].astype(BF16)
    w_mv = w_in[:, o_mv:o_mo].astype(BF16)
    w_g = jnp.pad(w_in[:, o_mi:o_mi + 2 * ML_HEADS], ((0, 0), (0, LANES - 2 * ML_HEADS))).astype(BF16)
    gbias = jnp.pad(jnp.concatenate([b_igate, b_fgate]).astype(F32), (0, LANES - 2 * ML_HEADS))[None, :]
    half = ATT_HEAD_DIM // 2
    inv = ROPE_THETA ** (-jnp.arange(half, dtype=F32) / half)
    quarter_turn = jnp.where((jnp.arange(LANES) % ATT_HEAD_DIM) < half, 0.0, jnp.pi / 2).astype(F32)
    inv = jnp.stack([jnp.tile(inv, LANES // half), quarter_turn])

    row = lambda w: pl.BlockSpec((t, w), lambda b, s, *_: (b * s_tiles + s, 0))
    grid_spec = pltpu.PrefetchScalarGridSpec(
        num_scalar_prefetch=1,
        grid=(batch, s_tiles),
        in_specs=[row(D_MODEL), row(1), _const_spec((2, LANES)),
                  _const_spec(w_rope.shape), _const_spec(w_v.shape), _const_spec(w_mqk.shape),
                  _const_spec(w_mv.shape), _const_spec(w_g.shape),
                  _const_spec((CONV_WIDTH, 2 * ML_QK_W)), _const_spec((1, 2 * ML_QK_W)),
                  _const_spec((1, LANES))],
        out_specs=[row(ATT_Q_W), row(ML_QK_W), row(ML_QK_W), row(ML_V_W), row(LANES)],
        scratch_shapes=[pltpu.VMEM((SUBLANES, 2 * ML_QK_W), F32),
                        pltpu.VMEM((ATT_BLOCK, 4 * ATT_KV_W), BF16)],
    )
    return pl.pallas_call(
        _inproj_kernel,
        grid_spec=grid_spec,
        out_shape=[jax.ShapeDtypeStruct((n, ATT_Q_W), BF16),
                   jax.ShapeDtypeStruct((n, ML_QK_W), BF16),
                   jax.ShapeDtypeStruct((n, ML_QK_W), BF16),
                   jax.ShapeDtypeStruct((n, ML_V_W), BF16),
                   jax.ShapeDtypeStruct((n, LANES), F32)],
        compiler_params=_params("arbitrary", "arbitrary"),
        name="in_proj",
    )(sinks.astype(F32) * LOG2_E, x2, pos2, inv, w_rope, w_v, w_mqk, w_mv, w_g, conv_w.astype(F32),
      conv_b.astype(F32)[None, :], gbias)


def _mlstm_reset(c_ref, n_ref, m_ref):
    c_ref[...] = jnp.zeros_like(c_ref)
    n_ref[...] = jnp.zeros_like(n_ref)
    m_ref[...] = jnp.zeros_like(m_ref)


def _mlstm_chunk(rows, q_ref, k_ref, v_ref, g_ref, o_ref, c_ref, n_ref, m_ref):
    nb = q_ref.shape[0]
    L = ML_CHUNK
    causal = lax.broadcasted_iota(I32, (L, L), 1) <= lax.broadcasted_iota(I32, (L, L), 0)
    rowi = lax.broadcasted_iota(I32, (L, LANES), 0)

    def scan_rows(x, op, identity):
        d = 1
        while d < L:
            x = op(x, jnp.where(rowi >= d, pltpu.roll(x, d, 0), identity))
            d *= 2
        return x

    def chunk():
        heads = [(bb, h) for bb in range(nb) for h in range(ML_HEADS)]

        gate = []
        for bb in range(nb):
            g = g_ref[bb, rows, :]
            b = pltpu.roll(scan_rows(g, jnp.add, 0.0), LANES - ML_HEADS, 1)
            a = g - b
            a_max = scan_rows(a, jnp.maximum, -jnp.inf)
            m_prev = m_ref[bb, 0:1, :]
            m_t = b + jnp.maximum(m_prev, a_max)
            u = b - m_t
            b_end = b[L - 1:L, :]
            m_new = b_end + jnp.maximum(m_prev, a_max[L - 1:L, :])
            m_ref[bb, 0:1, :] = m_new
            gate.append(dict(u=u, w_inter=jnp.exp(u + m_prev), floor=jnp.exp(-m_t),
                             w_state=jnp.exp(b_end + m_prev - m_new), w_k=jnp.exp(b_end + a - m_new),
                             a_t=a.T))

        nt = (((1,), (1,)), ((), ()))
        for bb, h in heads:
            gt = gate[bb]
            idx = bb * ML_HEADS + h
            col = slice(h, h + 1)
            qh = q_ref[bb, rows, h * ML_DQK:(h + 1) * ML_DQK]
            kh = k_ref[bb, rows, h * ML_DQK:(h + 1) * ML_DQK]
            vh = v_ref[bb, rows, h * ML_DV:(h + 1) * ML_DV]
            c_prev = c_ref[idx]
            n_prev = n_ref[idx, 0:1, :]

            w_intra = jnp.exp(jnp.where(causal, gt["u"][:, col] + gt["a_t"][col, :], -jnp.inf))
            s = lax.dot_general(qh, kh, nt, preferred_element_type=F32) * w_intra
            w_inter = gt["w_inter"][:, col]
            num = (w_inter * jnp.dot(qh, c_prev.astype(BF16), preferred_element_type=F32)
                   + jnp.dot(s.astype(BF16), vh, preferred_element_type=F32))
            den = (w_inter * jnp.sum(qh.astype(F32) * n_prev, axis=1, keepdims=True)
                   + jnp.sum(s, axis=1, keepdims=True))
            hval = num / jnp.maximum(jnp.abs(den), gt["floor"][:, col])
            hval = hval * lax.rsqrt(jnp.mean(hval * hval, axis=1, keepdims=True) + RMS_EPS)
            o_ref[bb, rows, h * ML_DV:(h + 1) * ML_DV] = hval.astype(BF16)

            w_state = gt["w_state"][:, col]
            kw = kh.astype(F32) * gt["w_k"][:, col]
            c_ref[idx] = w_state * c_prev + jnp.dot(kw.T.astype(BF16), vh, preferred_element_type=F32)
            n_ref[idx, 0:1, :] = w_state * n_prev + jnp.sum(kw, axis=0, keepdims=True)

    chunk()


def _memkv_kernel(mem_ref, w_ref, k_out, v_out):
    kv = jnp.dot(mem_ref[...].astype(BF16), w_ref[...], preferred_element_type=F32)
    k_out[...] = kv[:, :D_MODEL].astype(BF16)
    v_out[...] = kv[:, D_MODEL:].astype(BF16)


def _mem_kv(mem2, w_xkv, batch, mem_len):
    blk = pl.BlockSpec((mem_len, D_MODEL), lambda b: (b, 0))
    return pl.pallas_call(
        _memkv_kernel,
        grid=(batch,),
        in_specs=[blk, _const_spec((D_MODEL, 2 * D_MODEL))],
        out_specs=[blk, blk],
        out_shape=[jax.ShapeDtypeStruct((batch * mem_len, D_MODEL), BF16)] * 2,
        compiler_params=_params("arbitrary"),
        name="mem_kv",
    )(mem2, w_xkv.astype(BF16))


def _post_kernel(alpha, x_ref, att_ref, mq_ref, mk_ref, mv_ref, g_ref, kx_ref, vx_ref, wgate_ref, watt_ref,
                 wml_ref, wmix_ref, wxq_ref, wxo_ref, wr_hi_ref, wr_lo_ref, vec_ref, rbias_ref,
                 h_out, count_out, base_ref, c_ref, n_ref, m_ref, hm_ref):
    t = x_ref.shape[0]
    first = (pl.program_id(0) == 0) & (pl.program_id(1) == 0)

    @pl.when(first)
    def _():
        base_ref[...] = jnp.zeros_like(base_ref)

    @pl.when(pl.program_id(1) == 0)
    def _():
        _mlstm_reset(c_ref, n_ref, m_ref)

    for c in range(t // ML_CHUNK):
        _mlstm_chunk(slice(c * ML_CHUNK, (c + 1) * ML_CHUNK), mq_ref, mk_ref, mv_ref, g_ref, hm_ref,
                     c_ref, n_ref, m_ref)

    x = x_ref[...]
    xb = x.astype(BF16)
    norm_g, ln1_g, ln1_b, ln2_g, ln2_b = (vec_ref[i:i + 1, :] for i in range(5))

    chunks = [slice(c, c + POST_COLS) for c in range(0, D_MODEL, POST_COLS)]
    hm = jnp.concatenate(
        [(hm_ref[0, :, cs].astype(F32) * norm_g[:, cs]
          * jax.nn.sigmoid(jnp.dot(xb, wgate_ref[:, cs], preferred_element_type=F32))).astype(BF16)
         for cs in chunks], axis=1)
    att = att_ref[...]
    y = []
    for cs in chunks:
        gm = jnp.dot(xb, wgate_ref[:, 2 * D_MODEL + cs.start:2 * D_MODEL + cs.stop], preferred_element_type=F32)
        ga = jnp.dot(xb, wgate_ref[:, D_MODEL + cs.start:D_MODEL + cs.stop], preferred_element_type=F32)
        m_out = jnp.dot(hm, wml_ref[:, cs], preferred_element_type=F32)
        a_out = jnp.dot(att, watt_ref[:, cs], preferred_element_type=F32)
        y.append((jax.nn.sigmoid(gm) * m_out + jax.nn.sigmoid(ga) * a_out).astype(BF16))
    mix = jnp.dot(jnp.concatenate(y, axis=1), wmix_ref[...], preferred_element_type=F32)
    h1 = _layer_norm(alpha * x + mix, ln1_g, ln1_b)

    q = (jnp.dot(h1.astype(BF16), wxq_ref[...], preferred_element_type=F32) * (X_HEAD_DIM ** -0.5)).astype(BF16)
    heads = []
    for h in range(X_HEADS):
        sl = slice(h * X_HEAD_DIM, (h + 1) * X_HEAD_DIM)
        sc = lax.dot_general(q[:, sl], kx_ref[:, sl], (((1,), (1,)), ((), ())), preferred_element_type=F32)
        e = jnp.exp(sc - jnp.max(sc, axis=1, keepdims=True))
        probs = (e * (1.0 / jnp.sum(e, axis=1, keepdims=True))).astype(BF16)
        heads.append(jnp.dot(probs, vx_ref[:, sl], preferred_element_type=F32).astype(BF16))
    xa = jnp.dot(jnp.concatenate(heads, axis=1), wxo_ref[...], preferred_element_type=F32)
    h2 = _layer_norm(alpha * h1 + xa, ln2_g, ln2_b)
    h_out[:, :D_MODEL] = h2

    h_hi = h2.astype(BF16)
    h_lo = (h2 - h_hi.astype(F32)).astype(BF16)
    logits = (jnp.dot(h_hi, wr_hi_ref[...], preferred_element_type=F32)
              + jnp.dot(h_lo, wr_hi_ref[...], preferred_element_type=F32)
              + jnp.dot(h_hi, wr_lo_ref[...], preferred_element_type=F32)) + rbias_ref[...]

    lane = lax.broadcasted_iota(I32, (t, LANES), 1)
    neg = -jnp.inf
    is_group = lane < N_GROUPS
    gl = jnp.where(is_group, logits, neg)
    g_max = jnp.max(gl, axis=1, keepdims=True)
    lane_f = lane.astype(F32)
    no_lane = float(LANES)
    g_sel = jnp.min(jnp.where(gl == g_max, lane_f, no_lane), axis=1, keepdims=True)
    p_group = 1.0 / jnp.sum(jnp.where(is_group, jnp.exp(logits - g_max), 0.0), axis=1, keepdims=True)
    lane_group = ((lane - ROUTER_EXPERT_LANE0) >> (EXPERTS_PER_GROUP.bit_length() - 1)).astype(F32)
    in_group = (lane >= ROUTER_EXPERT_LANE0) & (lane_group == g_sel)
    el = jnp.where(in_group, logits, neg)
    v1 = jnp.max(el, axis=1, keepdims=True)
    i1 = jnp.min(jnp.where(el == v1, lane_f, no_lane), axis=1, keepdims=True)
    el2 = jnp.where(lane_f == i1, neg, el)
    v2 = jnp.max(el2, axis=1, keepdims=True)
    i2 = jnp.min(jnp.where(el2 == v2, lane_f, no_lane), axis=1, keepdims=True)
    e2 = jnp.exp(v2 - v1)
    w1 = p_group * (1.0 / (1.0 + e2))
    w2 = p_group * (e2 / (1.0 + e2))

    first_logit = ROUTER_EXPERT_LANE0 + EXPERTS_PER_GROUP * g_sel
    lo = jnp.minimum(i1, i2) - first_logit
    hi = jnp.maximum(i1, i2) - first_logit
    lo_odd = lo - 2.0 * jnp.floor(0.5 * lo)
    in_row = jnp.where(lo_odd == 1.0, (EXPERTS_PER_GROUP - 1) - hi, hi - lo - 1.0)
    cls = PAIRS_PER_GROUP * g_sel + lo * (2 * EXPERTS_PER_GROUP - 1 - lo) * 0.5 + in_row
    class_lane = lax.broadcasted_iota(I32, (t, CLASS_LANES), 1).astype(F32)
    onehot = jnp.where(class_lane == cls, 1.0, 0.0)
    tri = jnp.where(lax.broadcasted_iota(I32, (t, t), 1) < lax.broadcasted_iota(I32, (t, t), 0), 1.0, 0.0).astype(BF16)
    before = jnp.dot(tri, onehot.astype(BF16), preferred_element_type=F32) + base_ref[0:1, :]
    rank = jnp.sum(jnp.where(class_lane == cls, before, 0.0), axis=1, keepdims=True)
    total = base_ref[0:1, :] + jnp.sum(onehot, axis=0, keepdims=True)
    base_ref[...] = jnp.broadcast_to(total, base_ref.shape)
    count_out[...] = jnp.broadcast_to(total, count_out.shape)

    logit_lane = lane_f + (ROUTER_EXPERT_LANE0 - REC_WEIGHT_LANE0) + EXPERTS_PER_GROUP * g_sel
    rec = jnp.where(logit_lane == i1, w1, jnp.where(logit_lane == i2, w2, 0.0))
    rec = jnp.where((lane >= REC_WEIGHT_LANE0) & (lane < REC_WEIGHT_LANE0 + EXPERTS_PER_GROUP), rec, 0.0)
    rec = jnp.where(lane == REC_CLASS_LANE, cls, rec)
    rec = jnp.where(lane == REC_RANK_LANE, rank, rec)
    h_out[:, D_MODEL:] = rec


def _post(alpha, x2, att, mq, mk, mv, gates, kx, vx, w_gates, w_att, w_ml, w_mix, w_xq, w_xo, w_router, b_router,
          vecs, batch, seq, mem_len):
    n = x2.shape[0]
    t = min(POST_TILE, seq)
    s_tiles = seq // t
    row = lambda w: pl.BlockSpec((t, w), lambda b, s: (b * s_tiles + s, 0))
    memblk = pl.BlockSpec((mem_len, D_MODEL), lambda b, s: (b, 0))
    seq_blk = lambda w: pl.BlockSpec((1, t, w), lambda b, s: (b, s, 0))
    seqs = lambda a: a.reshape(batch, seq, a.shape[-1])
    w_hi = w_router.astype(BF16)
    w_lo = (w_router - w_hi.astype(F32)).astype(BF16)
    sq = _const_spec((D_MODEL, D_MODEL))
    return pl.pallas_call(
        functools.partial(_post_kernel, alpha),
        grid=(batch, s_tiles),
        in_specs=[row(D_MODEL), row(ATT_Q_W), seq_blk(ML_QK_W), seq_blk(ML_QK_W), seq_blk(ML_V_W), seq_blk(LANES),
                  memblk, memblk,
                  _const_spec((D_MODEL, 3 * D_MODEL)), sq, sq, sq, sq, sq,
                  _const_spec((D_MODEL, LANES)), _const_spec((D_MODEL, LANES)),
                  _const_spec((SUBLANES, D_MODEL)), _const_spec((1, LANES))],
        out_specs=[row(ROW_W), _const_spec((SUBLANES, CLASS_LANES))],
        out_shape=[jax.ShapeDtypeStruct((n, ROW_W), F32),
                   jax.ShapeDtypeStruct((SUBLANES, CLASS_LANES), F32)],
        scratch_shapes=[pltpu.VMEM((SUBLANES, CLASS_LANES), F32),
                        pltpu.VMEM((ML_HEADS, ML_DQK, ML_DV), F32),
                        pltpu.VMEM((ML_HEADS, SUBLANES, ML_DQK), F32),
                        pltpu.VMEM((1, SUBLANES, LANES), F32),
                        pltpu.VMEM((1, t, ML_V_W), BF16)],
        compiler_params=_params("arbitrary", "arbitrary"),
        name="post",
    )(x2, att, seqs(mq), seqs(mk), seqs(mv), seqs(gates), kx, vx, w_gates, w_att, w_ml, w_mix, w_xq, w_xo, w_hi, w_lo, vecs, b_router)


DMA_UNROLL = 32


def _scatter_kernel(dest_ref, padend_ref, h_ref, xs_out, zero_buf, sem, zero_sem):
    i = pl.program_id(0)
    ts = h_ref.shape[0]

    @pl.when(i == 0)
    def _():
        zero_buf[...] = jnp.zeros_like(zero_buf)

        def clear(start):
            start = pl.multiple_of(start, GROUP_BLOCK)
            return pltpu.make_async_copy(zero_buf, xs_out.at[pl.ds(start, GROUP_BLOCK), :], zero_sem)

        def nonempty(g):
            return padend_ref[g] > (padend_ref[g - 1] if g else 0)

        used = padend_ref[N_GROUPS - 1]

        def unused(b):
            return used + b * GROUP_BLOCK < xs_out.shape[0]

        for phase in ("start", "wait"):
            for g in range(N_GROUPS):
                @pl.when(nonempty(g))
                def _():
                    getattr(clear(padend_ref[g] - GROUP_BLOCK), phase)()

                @pl.when(unused(g))
                def _():
                    getattr(clear(used + g * GROUP_BLOCK), phase)()

    def issue(j, carry):
        for u in range(DMA_UNROLL):
            t = j * DMA_UNROLL + u
            pltpu.make_async_copy(h_ref.at[pl.ds(t, 1), :], xs_out.at[pl.ds(dest_ref[i * ts + t], 1), :], sem).start(priority=u % 2)
        return carry

    lax.fori_loop(0, ts // DMA_UNROLL, issue, 0)

    pltpu.make_async_copy(h_ref, xs_out.at[pl.ds(0, ts), :], sem).wait()


def _moe_scatter(dest, pad_end, h2aug, rows):
    n = h2aug.shape[0]
    ts = min(SCATTER_TILE, n)
    grid_spec = pltpu.PrefetchScalarGridSpec(
        num_scalar_prefetch=2,
        grid=(n // ts,),
        in_specs=[pl.BlockSpec((ts, ROW_W), lambda i, *_: (i, 0))],
        out_specs=pl.BlockSpec(memory_space=pl.ANY),
        scratch_shapes=[pltpu.VMEM((GROUP_BLOCK, ROW_W), F32),
                        pltpu.SemaphoreType.DMA(()), pltpu.SemaphoreType.DMA(())],
    )
    return pl.pallas_call(
        _scatter_kernel,
        grid_spec=grid_spec,
        out_shape=jax.ShapeDtypeStruct((rows, ROW_W), F32),
        compiler_params=_params("arbitrary"),
        name="moe_scatter",
    )(dest, pad_end, h2aug)


def _group_kernel(bgrp_ref, nused_ref, bmask_ref, x_ref, wgu_ref, wd_ref, y_ref, xb_ref):
    del bgrp_ref
    i = pl.program_id(0)
    y_ref[...] = jnp.zeros_like(y_ref)

    @pl.when(i < nused_ref[0])
    def _():
        xb_ref[...] = x_ref[:, :D_MODEL].astype(BF16)
        mask = bmask_ref[i]
        for e in range(EXPERTS_PER_GROUP):
            @pl.when(((mask >> e) & 1) == 1)
            def _():
                gu = jnp.dot(xb_ref[...], wgu_ref[0, e], preferred_element_type=F32)
                gate = gu[:, :D_EXPERT]
                lane = D_MODEL + REC_WEIGHT_LANE0 + e
                w = x_ref[:, lane:lane + 1]
                hw = jnp.where(w != 0.0, w * (gate * jax.nn.sigmoid(gate) * gu[:, D_EXPERT:]), 0.0)
                y_ref[...] += jnp.dot(hw.astype(BF16), wd_ref[0, e * D_EXPERT:(e + 1) * D_EXPERT, :],
                                      preferred_element_type=F32)


def _moe_groups(block_group, n_used, block_mask, xs, w_gu, w_down):
    rows = xs.shape[0]
    e = EXPERTS_PER_GROUP
    grid_spec = pltpu.PrefetchScalarGridSpec(
        num_scalar_prefetch=3,
        grid=(rows // GROUP_BLOCK,),
        in_specs=[pl.BlockSpec((GROUP_BLOCK, ROW_W), lambda i, *_: (i, 0)),
                  pl.BlockSpec((1, e, D_MODEL, 2 * D_EXPERT), lambda i, bg, *_: (bg[i], 0, 0, 0)),
                  pl.BlockSpec((1, e * D_EXPERT, D_MODEL), lambda i, bg, *_: (bg[i], 0, 0))],
        out_specs=pl.BlockSpec((GROUP_BLOCK, D_MODEL), lambda i, *_: (i, 0)),
        scratch_shapes=[pltpu.VMEM((GROUP_BLOCK, D_MODEL), BF16)],
    )
    return pl.pallas_call(
        _group_kernel,
        grid_spec=grid_spec,
        out_shape=jax.ShapeDtypeStruct((rows, D_MODEL), F32),
        compiler_params=_params("arbitrary"),
        name="moe_groups",
    )(block_group, n_used, block_mask, xs, w_gu, w_down)


def _combine_kernel(alpha, dest_ref, h_ref, vec_ref, y_hbm, o_ref, ybuf, sems):
    tc = h_ref.shape[0]
    i = pl.program_id(0)
    steps = pl.num_programs(0)

    def gather(tile, slot):
        def issue(j, carry):
            for u in range(DMA_UNROLL):
                t = j * DMA_UNROLL + u
                pltpu.make_async_copy(y_hbm.at[pl.ds(dest_ref[tile * tc + t], 1), :],
                                      ybuf.at[slot, pl.ds(t, 1), :], sems.at[slot]).start(priority=u % 2)
            return carry
        lax.fori_loop(0, tc // DMA_UNROLL, issue, 0)

    @pl.when(i == 0)
    def _():
        gather(0, 0)

    @pl.when(i + 1 < steps)
    def _():
        gather(i + 1, (i + 1) % 2)

    slot = i % 2
    pltpu.make_async_copy(y_hbm.at[pl.ds(0, tc), :], ybuf.at[slot], sems.at[slot]).wait()
    o_ref[...] = _layer_norm(alpha * h_ref[...] + ybuf[slot], vec_ref[0:1, :], vec_ref[1:2, :])


def _moe_combine(alpha, dest, h2aug, vecs, y):
    n = h2aug.shape[0]
    tc = min(COMBINE_TILE, n)
    grid_spec = pltpu.PrefetchScalarGridSpec(
        num_scalar_prefetch=1,
        grid=(n // tc,),
        in_specs=[pl.BlockSpec((tc, D_MODEL), lambda i, *_: (i, 0)),
                  pl.BlockSpec((SUBLANES, D_MODEL), lambda i, *_: (0, 0)),
                  pl.BlockSpec(memory_space=pl.ANY)],
        out_specs=pl.BlockSpec((tc, D_MODEL), lambda i, *_: (i, 0)),
        scratch_shapes=[pltpu.VMEM((2, tc, D_MODEL), F32), pltpu.SemaphoreType.DMA((2,))],
    )
    return pl.pallas_call(
        functools.partial(_combine_kernel, alpha),
        grid_spec=grid_spec,
        out_shape=jax.ShapeDtypeStruct((n, D_MODEL), F32),
        compiler_params=_params("arbitrary"),
        name="moe_combine",
    )(dest, h2aug, vecs, y)


def _pad_rows(vectors):
    rows = [v.astype(F32)[None, :] for v in vectors]
    rows.append(jnp.zeros((SUBLANES - len(rows), vectors[0].shape[0]), F32))
    return jnp.concatenate(rows, axis=0)


def _layer(alpha, h, mem2, pos2, batch, seq, mem_len, w_in, attn_sinks, conv_w, conv_b, b_igate, b_fgate,
           ml_norm_g, w_att_branch, w_ml_branch, w_mix_out, ln1_g, ln1_b, w_xq, w_xkv, w_xo, ln2_g, ln2_b,
           w_router_group, b_router_group, w_router_expert, b_router_expert, w_gate, w_up, w_down,
           ln3_g, ln3_b):
    n = h.shape[0]
    att, mq, mk, mv, gates = _in_proj(h, pos2, w_in, attn_sinks, conv_w, conv_b, b_igate, b_fgate, batch, seq)
    kx, vx = _mem_kv(mem2, w_xkv, batch, mem_len)

    o_mo = ATT_Q_W + 2 * ATT_KV_W + 2 * ML_QK_W + ML_V_W
    o_ga = o_mo + ML_V_W + 2 * ML_HEADS
    w_gates = jnp.concatenate([w_in[:, o_mo:o_mo + ML_V_W], w_in[:, o_ga:]], axis=1).astype(BF16)
    pad = LANES - N_GROUPS - N_EXPERTS
    w_router = jnp.pad(jnp.concatenate([w_router_group, w_router_expert], axis=1).astype(F32), ((0, 0), (0, pad)))
    b_router = jnp.pad(jnp.concatenate([b_router_group, b_router_expert]).astype(F32), (0, pad))[None, :]
    vecs = _pad_rows([ml_norm_g, ln1_g, ln1_b, ln2_g, ln2_b])
    h2aug, counts = _post(alpha, h, att, mq, mk, mv, gates, kx, vx, w_gates, w_att_branch.astype(BF16),
                          w_ml_branch.astype(BF16), w_mix_out.astype(BF16), w_xq.astype(BF16),
                          w_xo.astype(BF16), w_router, b_router, vecs, batch, seq, mem_len)

    e = EXPERTS_PER_GROUP
    class_count = counts[0, :N_CLASSES].astype(I32)
    group_count = jnp.sum(class_count.reshape(N_GROUPS, PAIRS_PER_GROUP), axis=1)
    padded = ((group_count + GROUP_BLOCK - 1) // GROUP_BLOCK) * GROUP_BLOCK
    pad_end = jnp.cumsum(padded).astype(I32)
    pad_start = pad_end - padded
    in_group = jnp.cumsum(class_count.reshape(N_GROUPS, PAIRS_PER_GROUP), axis=1) - class_count.reshape(N_GROUPS, -1)
    class_start = (pad_start[:, None] + in_group).reshape(N_CLASSES)
    class_end = class_start + class_count
    n_blocks = n // GROUP_BLOCK + N_GROUPS
    block_start = jnp.arange(n_blocks, dtype=I32) * GROUP_BLOCK
    block_group = jnp.sum((pad_end[None, :] <= block_start[:, None]).astype(I32), axis=1)
    block_group = jnp.minimum(block_group, N_GROUPS - 1)
    n_used = pad_end[-1:] // GROUP_BLOCK
    pairs = [(lo, hi) for lo in range(e) for hi in list(range(lo + 1, e))[::1 - 2 * (lo % 2)]]
    uses = jnp.array([[int(x in p) for x in range(e)] for p in pairs] * N_GROUPS, I32)
    overlap = ((class_start[None, :] < block_start[:, None] + GROUP_BLOCK) & (class_end[None, :] > block_start[:, None])
               & (class_count[None, :] > 0)).astype(I32)
    block_uses = jnp.sum(overlap[:, :, None] * uses[None, :, :], axis=1)
    block_mask = jnp.sum((block_uses > 0).astype(I32) << jnp.arange(e, dtype=I32)[None, :], axis=1)
    cls = h2aug[:, D_MODEL + REC_CLASS_LANE].astype(I32)
    rank = h2aug[:, D_MODEL + REC_RANK_LANE].astype(I32)
    is_class = cls[:, None] == jnp.arange(N_CLASSES, dtype=I32)[None, :]
    dest = jnp.sum(jnp.where(is_class, class_start[None, :], 0), axis=1) + rank

    xs = _moe_scatter(dest, pad_end, h2aug, n_blocks * GROUP_BLOCK)
    w_gu = jnp.concatenate([w_gate, w_up], axis=2).astype(BF16).reshape(N_GROUPS, e, D_MODEL, 2 * D_EXPERT)
    w_dn = w_down.astype(BF16).reshape(N_GROUPS, e * D_EXPERT, D_MODEL)
    y = _moe_groups(block_group, n_used, block_mask, xs, w_gu, w_dn)
    return _moe_combine(alpha, dest, h2aug, _pad_rows([ln3_g, ln3_b]), y)


def kernel(x, mem, positions, w_in, attn_sinks, conv_w, conv_b, b_igate, b_fgate, ml_norm_g, w_att_branch, w_ml_branch, w_mix_out, ln1_g, ln1_b, w_xq, w_xkv, w_xo, ln2_g, ln2_b, w_router_group, b_router_group, w_router_expert, b_router_expert, w_gate, w_up, w_down, ln3_g, ln3_b):
    batch, seq, d = x.shape
    mem_len = mem.shape[1]
    depth = w_in.shape[0]
    alpha = (2 * depth) ** 0.25
    h = x.reshape(batch * seq, d)
    mem2 = mem.reshape(batch * mem_len, d)
    pos2 = positions.reshape(batch * seq, 1)
    stacked = (w_in, attn_sinks, conv_w, conv_b, b_igate, b_fgate, ml_norm_g, w_att_branch, w_ml_branch,
               w_mix_out, ln1_g, ln1_b, w_xq, w_xkv, w_xo, ln2_g, ln2_b, w_router_group, b_router_group,
               w_router_expert, b_router_expert, w_gate, w_up, w_down, ln3_g, ln3_b)
    for l in range(depth):
        h = _layer(alpha, h, mem2, pos2, batch, seq, mem_len, *(w[l] for w in stacked))
    return h.reshape(batch, seq, d)
```
